```python
import math
import jax, jax.numpy as jnp
from jax import lax
import numpy as np


D_MODEL = 1024
BATCH = 8
SEQ = 2048
DEPTH = 2

SWA_Q_HEADS = 8
SWA_KV_HEADS = 2
SWA_HEAD_DIM = 64
SWA_WINDOW = 128
SWA_BLOCK = 128
ROPE_THETA = 500000.0
ROPE_DIM = SWA_HEAD_DIM // 4
DN_HEADS = 4
DN_HEAD_DIM = 128
DN_CONV = 4
DN_CHUNK = 64
POOL_WINDOWS = (2, 4, 8, 16)
POOL_GROUP = D_MODEL // 4
N_EXPERTS = 32
TOP_K = 4
D_EXPERT = D_MODEL
SWIGLU_LIMIT = 7.0
SWIGLU_ALPHA = 1.702
LN_EPS = 1e-5
RMS_EPS = 1e-6
DEEPNORM_ALPHA = (2 * DEPTH) ** 0.25
DEEPNORM_BETA = (8 * DEPTH) ** -0.25
SWA_Q_WIDTH = SWA_Q_HEADS * SWA_HEAD_DIM
SWA_KV_WIDTH = SWA_KV_HEADS * SWA_HEAD_DIM
DN_WIDTH = DN_HEADS * DN_HEAD_DIM
IN_SIZES = (SWA_Q_WIDTH, SWA_KV_WIDTH, SWA_KV_WIDTH, 3 * DN_WIDTH, DN_WIDTH, DN_HEADS, DN_HEADS)
IN_WIDTH = sum(IN_SIZES)
MIX_WIDTH = SWA_Q_WIDTH + DN_WIDTH
N_EVEN = (DEPTH + 1) // 2
N_ODD = DEPTH // 2

kernel_name = 'hybrid_swa_deltanet_pool_moe'


def split_cols(t, sizes):
    out, start = [], 0
    for s in sizes:
        out.append(t[..., start:start + s])
        start += s
    return out


def layer_norm(x, g, b):
    xf = x.astype(jnp.float32)
    mu = xf.mean(-1, keepdims=True)
    var = jnp.square(xf - mu).mean(-1, keepdims=True)
    return ((xf - mu) * lax.rsqrt(var + LN_EPS) * g.astype(jnp.float32) + b.astype(jnp.float32)).astype(x.dtype)


def partial_rotary(x, positions):
    half = ROPE_DIM // 2
    inv_freq = ROPE_THETA ** (-jnp.arange(half, dtype=jnp.float32) / half)
    ang = positions.astype(jnp.float32)[:, :, None] * inv_freq
    cos = jnp.cos(ang)[:, :, None, :]
    sin = jnp.sin(ang)[:, :, None, :]
    xr = x[..., :ROPE_DIM].astype(jnp.float32)
    x1, x2 = xr[..., :half], xr[..., half:]
    rot = jnp.concatenate([x1 * cos - x2 * sin, x2 * cos + x1 * sin], axis=-1).astype(x.dtype)
    return jnp.concatenate([rot, x[..., ROPE_DIM:]], axis=-1)


def sliding_window_attention(q, k, v, sinks):
    B, T = q.shape[0], q.shape[1]
    nb = T // SWA_BLOCK
    G = SWA_Q_HEADS // SWA_KV_HEADS
    qb = q.reshape(B, nb, SWA_BLOCK, SWA_KV_HEADS, G, SWA_HEAD_DIM).astype(jnp.float32)

    def band(t):
        tb = t.reshape(B, nb, SWA_BLOCK, SWA_KV_HEADS, SWA_HEAD_DIM)
        prev = jnp.pad(tb, ((0, 0), (1, 0), (0, 0), (0, 0), (0, 0)))[:, :-1]
        return jnp.concatenate([prev, tb], axis=2)

    kb = band(k).astype(jnp.float32)
    vb = band(v)
    s = jnp.einsum('bnqhgd,bnkhd->bnhgqk', qb, kb) * (SWA_HEAD_DIM ** -0.5)
    qi = jnp.arange(SWA_BLOCK)[:, None]
    kj = jnp.arange(2 * SWA_BLOCK)[None, :]
    rel = qi + SWA_BLOCK - kj
    in_window = (rel >= 0) & (rel < SWA_WINDOW)
    key_pos = jnp.arange(nb)[:, None, None] * SWA_BLOCK - SWA_BLOCK + kj[None]
    valid = in_window[None] & (key_pos >= 0)
    s = jnp.where(valid[None, :, None, None], s, -jnp.inf)
    sink = sinks.astype(jnp.float32).reshape(SWA_KV_HEADS, G)[None, None, :, :, None, None]
    m = jnp.maximum(s.max(-1, keepdims=True), sink)
    p = jnp.exp(s - m)
    p = p / (p.sum(-1, keepdims=True) + jnp.exp(sink - m))
    o = jnp.einsum('bnhgqk,bnkhd->bnqhgd', p.astype(vb.dtype), vb)
    return o.reshape(B, T, SWA_Q_WIDTH)


def causal_depthwise_conv(x, w):
    K = w.shape[0]
    return lax.conv_general_dilated(x, w[:, None, :].astype(x.dtype), window_strides=(1,),
                                    padding=((K - 1, 0),), dimension_numbers=('NWC', 'WIO', 'NWC'),
                                    feature_group_count=x.shape[-1])


def l2norm(x):
    return x * lax.rsqrt(jnp.sum(x * x, axis=-1, keepdims=True) + RMS_EPS)


def gated_delta_rule(q, k, v, g, beta):
    B, T, H, Dk = k.shape
    Dv = v.shape[-1]
    C = DN_CHUNK
    n = T // C

    def chunks(t):
        return t.reshape((B, n, C) + t.shape[2:]).swapaxes(2, 3)

    q, k, v, g, beta = chunks(q), chunks(k), chunks(v), chunks(g), chunks(beta)
    gc = jnp.cumsum(g, axis=-1)
    causal = jnp.tril(jnp.ones((C, C), dtype=bool))
    strict = jnp.tril(jnp.ones((C, C), dtype=bool), -1)
    decay = jnp.exp(jnp.where(causal, gc[..., :, None] - gc[..., None, :], -jnp.inf))
    kk = jnp.einsum('bnhid,bnhjd->bnhij', k, k)
    a_strict = jnp.where(strict, beta[..., :, None] * kk * decay, 0.0)
    rhs = jnp.concatenate([v * beta[..., None], k * (beta * jnp.exp(gc))[..., None]], axis=-1)
    sol = lax.linalg.triangular_solve(a_strict, rhs, left_side=True, lower=True, unit_diagonal=True)
    u, w = sol[..., :Dv], sol[..., Dv:]
    qk = jnp.einsum('bnhid,bnhjd->bnhij', q, k) * decay
    q_dec = q * jnp.exp(gc)[..., None]
    k_dec = k * jnp.exp(gc[..., -1:] - gc)[..., None]
    g_last = jnp.exp(gc[..., -1])

    def step(S, inp):
        w_c, u_c, q_c, k_c, qk_c, gl = inp
        v_new = u_c - jnp.einsum('bhcd,bhde->bhce', w_c, S)
        o = jnp.einsum('bhcd,bhde->bhce', q_c, S) + jnp.einsum('bhij,bhje->bhie', qk_c, v_new)
        S = S * gl[..., None, None] + jnp.einsum('bhcd,bhce->bhde', k_c, v_new)
        return S, o

    xs = (w.swapaxes(0, 1), u.swapaxes(0, 1), q_dec.swapaxes(0, 1), k_dec.swapaxes(0, 1),
          qk.swapaxes(0, 1), g_last.swapaxes(0, 1))
    S0 = jnp.zeros((B, H, Dk, Dv), jnp.float32)
    _, o = lax.scan(step, S0, xs)
    return o.transpose(1, 0, 3, 2, 4).reshape(B, T, H, Dv)


def mixer_swa_deltanet(x, positions, w_in, b_in, conv_w, a_log, dt_bias, norm_w, sinks, w_out, b_out):
    B, T, _ = x.shape
    proj = x @ w_in + b_in
    sq, sk, sv, dqkv, dz, da, db = split_cols(proj, IN_SIZES)
    q = partial_rotary(sq.reshape(B, T, SWA_Q_HEADS, SWA_HEAD_DIM), positions)
    k = partial_rotary(sk.reshape(B, T, SWA_KV_HEADS, SWA_HEAD_DIM), positions)
    v = sv.reshape(B, T, SWA_KV_HEADS, SWA_HEAD_DIM)
    a_out = sliding_window_attention(q, k, v, sinks)
    dqkv = jax.nn.silu(causal_depthwise_conv(dqkv, conv_w)).astype(jnp.float32)
    dq, dk, dv = split_cols(dqkv, (DN_WIDTH, DN_WIDTH, DN_WIDTH))
    dq = l2norm(dq.reshape(B, T, DN_HEADS, DN_HEAD_DIM)) * (DN_HEAD_DIM ** -0.5)
    dk = l2norm(dk.reshape(B, T, DN_HEADS, DN_HEAD_DIM))
    dv = dv.reshape(B, T, DN_HEADS, DN_HEAD_DIM)
    beta = jax.nn.sigmoid(db.astype(jnp.float32))
    g = -jnp.exp(a_log.astype(jnp.float32)) * jax.nn.softplus(da.astype(jnp.float32) + dt_bias.astype(jnp.float32))
    o = gated_delta_rule(dq, dk, dv, g, beta)
    z = dz.reshape(B, T, DN_HEADS, DN_HEAD_DIM).astype(jnp.float32)
    o = o * lax.rsqrt(jnp.mean(o * o, axis=-1, keepdims=True) + RMS_EPS) * norm_w.astype(jnp.float32) * jax.nn.silu(z)
    dn_out = o.reshape(B, T, DN_WIDTH).astype(x.dtype)
    return jnp.concatenate([a_out, dn_out], axis=-1) @ w_out + b_out


def multiscale_pool(x, pool_w, pool_b, pool_scale):
    B, T, D = x.shape
    xf = x.astype(jnp.float32)
    cs = jnp.pad(jnp.cumsum(xf, axis=1), ((0, 0), (1, 0), (0, 0)))
    t = jnp.arange(T)
    outs = []
    for gi, win in enumerate(POOL_WINDOWS):
        lo = jnp.maximum(t + 1 - win, 0)
        cnt = (t + 1 - lo).astype(jnp.float32)[None, :, None]
        csg = cs[..., gi * POOL_GROUP:(gi + 1) * POOL_GROUP]
        mean = (csg[:, 1:] - csg[:, lo]) / cnt
        outs.append(mean - xf[..., gi * POOL_GROUP:(gi + 1) * POOL_GROUP])
    pooled = jnp.stack(outs, axis=2).astype(x.dtype)
    y = jnp.einsum('btgc,gcd->btgd', pooled, pool_w) + pool_b
    return y.reshape(B, T, D) * pool_scale


def moe(x, router_w, router_b, w1, b1, w2, b2):
    B, T, D = x.shape
    xt = x.reshape(B * T, D)
    logits = (xt @ router_w + router_b).astype(jnp.float32)
    top_v, top_i = lax.top_k(logits, TOP_K)
    top_w = jax.nn.softmax(top_v, axis=-1)
    y = jnp.zeros((B * T, D), jnp.float32)
    for e in range(N_EXPERTS):
        gate = jnp.sum(jnp.where(top_i == e, top_w, 0.0), axis=-1)
        h = xt @ w1[e] + b1[e]
        h_glu = jnp.minimum(h[:, 0::2], SWIGLU_LIMIT)
        h_lin = jnp.clip(h[:, 1::2], -SWIGLU_LIMIT, SWIGLU_LIMIT)
        act = h_glu * jax.nn.sigmoid(SWIGLU_ALPHA * h_glu) * (h_lin + 1.0)
        y = y + gate[:, None] * (act @ w2[e] + b2[e]).astype(jnp.float32)
    return y.astype(x.dtype).reshape(B, T, D)


def setup_inputs(seed: int = 0) -> dict:
    key = jax.random.key(seed)
    ks = jax.random.split(key, 24)
    f32 = jnp.float32

    def nrm(k, shape, scale):
        return jax.random.normal(k, shape, f32) * scale

    x = jax.random.normal(ks[0], (BATCH, SEQ, D_MODEL), f32)
    offset = jax.random.randint(ks[1], (BATCH, 1), 0, 4096, dtype=jnp.int32)
    positions = offset + jnp.arange(SEQ, dtype=jnp.int32)[None, :]
    mix_w_in = nrm(ks[2], (N_EVEN, D_MODEL, IN_WIDTH), D_MODEL ** -0.5)
    mix_b_in = nrm(ks[3], (N_EVEN, IN_WIDTH), 0.01)
    dn_conv_w = nrm(ks[4], (N_EVEN, DN_CONV, 3 * DN_WIDTH), DN_CONV ** -0.5)
    dn_a_log = jnp.log(jax.random.uniform(ks[5], (N_EVEN, DN_HEADS), f32, 1.0, 16.0))
    dt = jnp.exp(jax.random.uniform(ks[6], (N_EVEN, DN_HEADS), f32, math.log(1e-3), math.log(1e-1)))
    dn_dt_bias = dt + jnp.log(-jnp.expm1(-dt))
    dn_norm_w = 1.0 + nrm(ks[7], (N_EVEN, DN_HEAD_DIM), 0.02)
    swa_sinks = nrm(ks[8], (N_EVEN, SWA_Q_HEADS), 1.0)
    mix_w_out = nrm(ks[9], (N_EVEN, MIX_WIDTH, D_MODEL), DEEPNORM_BETA * MIX_WIDTH ** -0.5)
    mix_b_out = nrm(ks[10], (N_EVEN, D_MODEL), 0.01)
    pool_w = nrm(ks[11], (N_ODD, len(POOL_WINDOWS), POOL_GROUP, POOL_GROUP), DEEPNORM_BETA * POOL_GROUP ** -0.5)
    pool_b = nrm(ks[12], (N_ODD, len(POOL_WINDOWS), POOL_GROUP), 0.01)
    pool_scale = 1.0 + nrm(ks[13], (N_ODD, D_MODEL), 0.02)
    ln1_g = 1.0 + nrm(ks[14], (DEPTH, D_MODEL), 0.02)
    ln1_b = nrm(ks[15], (DEPTH, D_MODEL), 0.01)
    router_w = nrm(ks[16], (DEPTH, D_MODEL, N_EXPERTS), D_MODEL ** -0.5)
    router_b = nrm(ks[17], (DEPTH, N_EXPERTS), 0.01)
    moe_w1 = nrm(ks[18], (DEPTH, N_EXPERTS, D_MODEL, 2 * D_EXPERT), D_MODEL ** -0.5)
    moe_b1 = nrm(ks[19], (DEPTH, N_EXPERTS, 2 * D_EXPERT), 0.01)
    moe_w2 = nrm(ks[20], (DEPTH, N_EXPERTS, D_EXPERT, D_MODEL), DEEPNORM_BETA * D_EXPERT ** -0.5)
    moe_b2 = nrm(ks[21], (DEPTH, N_EXPERTS, D_MODEL), 0.01)
    ln2_g = 1.0 + nrm(ks[22], (DEPTH, D_MODEL), 0.02)
    ln2_b = nrm(ks[23], (DEPTH, D_MODEL), 0.01)
    return {'x': x, 'positions': positions, 'mix_w_in': mix_w_in, 'mix_b_in': mix_b_in,
            'dn_conv_w': dn_conv_w, 'dn_a_log': dn_a_log, 'dn_dt_bias': dn_dt_bias, 'dn_norm_w': dn_norm_w,
            'swa_sinks': swa_sinks, 'mix_w_out': mix_w_out, 'mix_b_out': mix_b_out,
            'pool_w': pool_w, 'pool_b': pool_b, 'pool_scale': pool_scale,
            'ln1_g': ln1_g, 'ln1_b': ln1_b, 'router_w': router_w, 'router_b': router_b,
            'moe_w1': moe_w1, 'moe_b1': moe_b1, 'moe_w2': moe_w2, 'moe_b2': moe_b2,
            'ln2_g': ln2_g, 'ln2_b': ln2_b}


def reference(x, positions, mix_w_in, mix_b_in, dn_conv_w, dn_a_log, dn_dt_bias, dn_norm_w,
              swa_sinks, mix_w_out, mix_b_out, pool_w, pool_b, pool_scale,
              ln1_g, ln1_b, router_w, router_b, moe_w1, moe_b1, moe_w2, moe_b2, ln2_g, ln2_b):
    for layer in range(DEPTH):
        i = layer // 2
        if layer % 2 == 0:
            h = mixer_swa_deltanet(x, positions, mix_w_in[i], mix_b_in[i], dn_conv_w[i], dn_a_log[i],
                                   dn_dt_bias[i], dn_norm_w[i], swa_sinks[i], mix_w_out[i], mix_b_out[i])
        else:
            h = multiscale_pool(x, pool_w[i], pool_b[i], pool_scale[i])
        x = layer_norm(DEEPNORM_ALPHA * x + h, ln1_g[layer], ln1_b[layer])
        f = moe(x, router_w[layer], router_b[layer], moe_w1[layer], moe_b1[layer], moe_w2[layer], moe_b2[layer])
        x = layer_norm(DEEPNORM_ALPHA * x + f, ln2_g[layer], ln2_b[layer])
    return x
```

```python
import functools
import math

import numpy as np
import jax
import jax.numpy as jnp
from jax import lax
from jax.experimental import pallas as pl
from jax.experimental.pallas import tpu as pltpu

D_MODEL = 1024
DEPTH = 2
SWA_Q_HEADS = 8
SWA_KV_HEADS = 2
SWA_HEAD_DIM = 64
SWA_BLOCK = 128
ROPE_THETA = 500000.0
ROPE_DIM = SWA_HEAD_DIM // 4
DN_HEADS = 4
DN_HEAD_DIM = 128
DN_CONV = 4
POOL_WINDOWS = (2, 4, 8, 16)
POOL_GROUP = D_MODEL // 4
N_EXPERTS = 32
TOP_K = 4
D_EXPERT = D_MODEL
SWIGLU_LIMIT = 7.0
SWIGLU_ALPHA = 1.702
LN_EPS = 1e-5
RMS_EPS = 1e-6
DEEPNORM_ALPHA = (2 * DEPTH) ** 0.25
SWA_Q_WIDTH = SWA_Q_HEADS * SWA_HEAD_DIM
SWA_KV_WIDTH = SWA_KV_HEADS * SWA_HEAD_DIM
DN_WIDTH = DN_HEADS * DN_HEAD_DIM

LANES = 128
SUBLANES = 8
SLAB = D_MODEL // LANES
IN_PAD_WIDTH = 3328
VMEM_LIMIT = 56 * 1024 * 1024

F32 = jnp.float32
BF16 = jnp.bfloat16
HIGHEST = lax.Precision.HIGHEST

MOE_TM = 256
DN_CHUNK = 128


def _sigmoid(x):
    return 1.0 / (1.0 + jnp.exp(-x))


def _layer_norm(z, g, b):
    mu = jnp.mean(z, axis=-1, keepdims=True)
    zc = z - mu
    var = jnp.mean(zc * zc, axis=-1, keepdims=True)
    return zc * lax.rsqrt(var + LN_EPS) * g + b


def _dot(a, b):
    return jnp.dot(a, b, preferred_element_type=F32)


def _dot_nt(a, b):
    return lax.dot_general(a, b, (((1,), (1,)), ((), ())), preferred_element_type=F32)


def _route_tail(x1, rw_ref, rb_ref, carry_ref, x1_ref, slab_ref, ti_ref, gate_ref, rank_ref, cnt_ref):
    tm = x1.shape[0]
    x1_ref[...] = x1
    for s in range(SLAB):
        slab_ref[pl.ds(s, tm, stride=SLAB), :] = x1[:, s * LANES:(s + 1) * LANES]

    logits = jnp.dot(x1, rw_ref[...], precision=HIGHEST, preferred_element_type=F32) + rb_ref[...]
    lane = lax.broadcasted_iota(jnp.int32, (tm, LANES), 1)
    lane_f = lane.astype(F32)
    l = jnp.where(lane < N_EXPERTS, logits, -jnp.inf)
    vals, hits = [], []
    ti_out = jnp.zeros((tm, LANES), jnp.int32)
    for k in range(TOP_K):
        m = jnp.max(l, axis=-1, keepdims=True)
        idx = jnp.min(jnp.where(l == m, lane_f, float(LANES)), axis=-1, keepdims=True)
        hit = lane_f == idx
        l = jnp.where(hit, -jnp.inf, l)
        vals.append(m)
        hits.append(hit)
        ti_out = jnp.where(lane == k, idx.astype(jnp.int32), ti_out)
    exps = [jnp.exp(v - vals[0]) for v in vals]
    den = exps[0] + exps[1] + exps[2] + exps[3]
    gate_out = jnp.zeros((tm, LANES), F32)
    for k in range(TOP_K):
        gate_out = jnp.where(lane == k, exps[k] / den, gate_out)

    sel = jnp.zeros((tm, LANES), F32)
    for k in range(TOP_K):
        sel = sel + hits[k].astype(F32)
    ri = lax.broadcasted_iota(jnp.int32, (tm, tm), 0)
    ci = lax.broadcasted_iota(jnp.int32, (tm, tm), 1)
    tri = (ri > ci).astype(BF16)
    carry = carry_ref[...]
    prefix = _dot(tri, sel.astype(BF16)) + carry
    rank_out = jnp.zeros((tm, LANES), jnp.int32)
    for k in range(TOP_K):
        r = jnp.sum(jnp.where(hits[k], prefix, 0.0), axis=-1, keepdims=True)
        rank_out = jnp.where(lane == k, r.astype(jnp.int32), rank_out)
    new_carry = carry + jnp.sum(sel, axis=0, keepdims=True)
    carry_ref[...] = new_carry
    cnt_ref[...] = new_carry.astype(jnp.int32)
    ti_ref[...] = ti_out[:, :TOP_K]
    gate_ref[...] = gate_out[:, :TOP_K]
    rank_ref[...] = rank_out[:, :TOP_K]


def _route_out_shapes(n):
    return (
        jax.ShapeDtypeStruct((n, D_MODEL), F32),
        jax.ShapeDtypeStruct((n * SLAB, LANES), F32),
        jax.ShapeDtypeStruct((n, TOP_K), jnp.int32),
        jax.ShapeDtypeStruct((n, TOP_K), F32),
        jax.ShapeDtypeStruct((n, TOP_K), jnp.int32),
        jax.ShapeDtypeStruct((1, LANES), jnp.int32),
    )


def _route_out_specs(tm, row_map):
    return (
        pl.BlockSpec((tm, D_MODEL), lambda *a: (row_map(*a), 0)),
        pl.BlockSpec((tm * SLAB, LANES), lambda *a: (row_map(*a), 0)),
        pl.BlockSpec((tm, TOP_K), lambda *a: (row_map(*a), 0)),
        pl.BlockSpec((tm, TOP_K), lambda *a: (row_map(*a), 0)),
        pl.BlockSpec((tm, TOP_K), lambda *a: (row_map(*a), 0)),
        pl.BlockSpec((1, LANES), lambda *a: (0, 0)),
    )


def _inproj_kernel(x_ref, pos_ref, invf_ref, w_ref, b_ref,
                   q_ref, kv_ref, dq_ref, dk_ref, dv_ref, dz_ref, ab_ref):
    x = x_ref[...].astype(BF16)
    proj = _dot(x, w_ref[...]) + b_ref[...]
    tm = x.shape[0]
    ang = pos_ref[...].astype(F32) * invf_ref[...]
    cos = jnp.cos(ang)
    sin = jnp.sin(ang)
    d = lax.broadcasted_iota(jnp.int32, (tm, LANES), 1) % SWA_HEAD_DIM
    half = ROPE_DIM // 2
    c_tab = jnp.where(d < ROPE_DIM, cos, 1.0)
    s_lo = jnp.where(d < half, -sin, 0.0)
    s_hi = jnp.where((d >= half) & (d < ROPE_DIM), sin, 0.0)

    def rot(xc):
        return (xc * c_tab + pltpu.roll(xc, LANES - half, 1) * s_lo
                + pltpu.roll(xc, half, 1) * s_hi)

    for c in range(SWA_Q_WIDTH // LANES):
        q_ref[:, c * LANES:(c + 1) * LANES] = rot(proj[:, c * LANES:(c + 1) * LANES])
    o = SWA_Q_WIDTH
    kv_ref[:, :LANES] = rot(proj[:, o:o + LANES])
    kv_ref[:, LANES:] = proj[:, o + LANES:o + 2 * LANES]
    o += 2 * SWA_KV_WIDTH
    dq_ref[...] = proj[:, o:o + DN_WIDTH]
    dk_ref[...] = proj[:, o + DN_WIDTH:o + 2 * DN_WIDTH]
    dv_ref[...] = proj[:, o + 2 * DN_WIDTH:o + 3 * DN_WIDTH]
    o += 3 * DN_WIDTH
    dz_ref[...] = proj[:, o:o + DN_WIDTH]
    o += DN_WIDTH
    ab_ref[...] = proj[:, o:o + LANES]


def _inproj(x2d, pos2d, w_in, b_in):
    n = x2d.shape[0]
    tm = 256
    in_width = w_in.shape[1]
    w = jnp.pad(w_in, ((0, 0), (0, IN_PAD_WIDTH - in_width))).astype(BF16)
    b = jnp.pad(b_in, (0, IN_PAD_WIDTH - in_width)).reshape(1, IN_PAD_WIDTH)
    half = ROPE_DIM // 2
    lane_d = np.arange(LANES) % SWA_HEAD_DIM
    invf = (ROPE_THETA ** (-(lane_d % half).astype(np.float64) / half)).astype(np.float32)
    invf = jnp.asarray(invf.reshape(1, LANES))
    row = lambda i: (i, 0)
    const = lambda i: (0, 0)
    widths = (SWA_Q_WIDTH, 2 * SWA_KV_WIDTH, DN_WIDTH, DN_WIDTH, DN_WIDTH, DN_WIDTH, LANES)
    return pl.pallas_call(
        _inproj_kernel,
        grid=(n // tm,),
        in_specs=[
            pl.BlockSpec((tm, D_MODEL), row),
            pl.BlockSpec((tm, 1), row),
            pl.BlockSpec((1, LANES), const),
            pl.BlockSpec((D_MODEL, IN_PAD_WIDTH), const),
            pl.BlockSpec((1, IN_PAD_WIDTH), const),
        ],
        out_specs=tuple(pl.BlockSpec((tm, wd), row) for wd in widths),
        out_shape=tuple(jax.ShapeDtypeStruct((n, wd), F32) for wd in widths),
        compiler_params=pltpu.CompilerParams(dimension_semantics=("arbitrary",),
                                             vmem_limit_bytes=VMEM_LIMIT),
        name="inproj_rotary",
    )(x2d, pos2d, invf, w, b)


def _swa_kernel(sink_ref, q_ref, kv_ref, o_ref):
    t = q_ref.shape[0]
    blk = SWA_BLOCK
    lane = lax.broadcasted_iota(jnp.int32, (2 * blk, LANES), 1)
    qi = lax.broadcasted_iota(jnp.int32, (blk, 2 * blk), 0)
    kj = lax.broadcasted_iota(jnp.int32, (blk, 2 * blk), 1)
    rel = qi + blk - kj
    in_window = (rel >= 0) & (rel < blk)
    is_cur = kj >= blk
    scale = SWA_HEAD_DIM ** -0.5

    def body(n, carry):
        r0 = pl.multiple_of(n * blk, blk)
        p0 = pl.multiple_of(jnp.maximum(n - 1, 0) * blk, blk)
        kvc = kv_ref[pl.ds(r0, blk), :]
        kvp = kv_ref[pl.ds(p0, blk), :]
        kband = jnp.concatenate([kvp[:, :LANES], kvc[:, :LANES]], axis=0)
        vband = jnp.concatenate([kvp[:, LANES:], kvc[:, LANES:]], axis=0)
        valid = in_window & (is_cur | (n > 0))
        k_rhs, v_rhs = [], []
        for h in range(SWA_KV_HEADS):
            in_head = (lane >= h * SWA_HEAD_DIM) & (lane < (h + 1) * SWA_HEAD_DIM)
            km = jnp.where(in_head, kband, 0.0)
            vm = jnp.where(in_head, vband, 0.0)
            kr = pltpu.roll(km, SWA_HEAD_DIM, 1)
            vr = pltpu.roll(vm, SWA_HEAD_DIM, 1)
            lo_k, hi_k = (km, kr) if h == 0 else (kr, km)
            lo_v, hi_v = (vm, vr) if h == 0 else (vr, vm)
            k_rhs.append(jnp.concatenate([lo_k, hi_k], axis=0).astype(BF16))
            v_rhs.append(jnp.concatenate([lo_v, hi_v], axis=0).astype(BF16))
        group = SWA_Q_HEADS // SWA_KV_HEADS
        for c in range(SWA_Q_WIDTH // LANES):
            h = (2 * c) // group
            qc = (q_ref[pl.ds(r0, blk), c * LANES:(c + 1) * LANES] * scale).astype(BF16)
            s_both = _dot_nt(qc, k_rhs[h])
            ps = []
            for j in range(2):
                s = jnp.where(valid, s_both[:, j * 2 * blk:(j + 1) * 2 * blk], -jnp.inf)
                sink = sink_ref[2 * c + j]
                m = jnp.maximum(jnp.max(s, axis=-1, keepdims=True), sink)
                p = jnp.exp(s - m)
                den = jnp.sum(p, axis=-1, keepdims=True) + jnp.exp(sink - m)
                ps.append((p / den).astype(BF16))
            p_both = jnp.concatenate(ps, axis=1)
            o_ref[pl.ds(r0, blk), c * LANES:(c + 1) * LANES] = _dot(p_both, v_rhs[h])
        return carry

    lax.fori_loop(0, t // blk, body, 0)


def _swa(q, kv, sinks, batch, t):
    return pl.pallas_call(
        _swa_kernel,
        grid=(batch,),
        in_specs=[
            pl.BlockSpec(memory_space=pltpu.SMEM),
            pl.BlockSpec((t, SWA_Q_WIDTH), lambda b: (b, 0)),
            pl.BlockSpec((t, 2 * SWA_KV_WIDTH), lambda b: (b, 0)),
        ],
        out_specs=pl.BlockSpec((t, SWA_Q_WIDTH), lambda b: (b, 0)),
        out_shape=jax.ShapeDtypeStruct((batch * t, SWA_Q_WIDTH), F32),
        compiler_params=pltpu.CompilerParams(dimension_semantics=("arbitrary",),
                                             vmem_limit_bytes=VMEM_LIMIT),
        name="swa_attention",
    )(sinks.astype(F32), q, kv)


def _dn_kernel(convw_ref, hp_ref, normw_ref, dq_ref, dk_ref, dv_ref, dz_ref, ab_ref, o_ref,
               state_ref, halo_ref, pad_ref):
    tc = dq_ref.shape[0]
    c = DN_CHUNK
    halo = SUBLANES

    @pl.when(pl.program_id(1) == 0)
    def _():
        state_ref[...] = jnp.zeros_like(state_ref)
        halo_ref[...] = jnp.zeros_like(halo_ref)

    def conv_silu(x_ref, idx):
        pad_ref[0:halo, :] = halo_ref[idx]
        pad_ref[halo:, :] = x_ref[...]
        halo_ref[idx] = x_ref[tc - halo:tc, :]
        acc = jnp.zeros((tc, DN_WIDTH), F32)
        for j in range(DN_CONV):
            wj = convw_ref[j:j + 1, idx * DN_WIDTH:(idx + 1) * DN_WIDTH]
            acc = acc + wj * pad_ref[pl.ds(halo - DN_CONV + 1 + j, tc), :]
        return acc * _sigmoid(acc)

    pad_q = conv_silu(dq_ref, 0)
    q_all = pad_q
    k_all = conv_silu(dk_ref, 1)
    v_all = conv_silu(dv_ref, 2)

    ab = ab_ref[...]
    neg_a = hp_ref[0:1, :]
    dt_b = hp_ref[1:2, :]
    sp_arg = ab + dt_b
    softplus = jnp.maximum(sp_arg, 0.0) + jnp.log(1.0 + jnp.exp(-jnp.abs(sp_arg)))
    g_all = neg_a * softplus
    beta_all = _sigmoid(ab)

    ri = lax.broadcasted_iota(jnp.int32, (c, c), 0)
    ci = lax.broadcasted_iota(jnp.int32, (c, c), 1)
    lower_incl = ri >= ci
    lower_strict = ri > ci
    ltri = lower_incl.astype(F32)
    eye = (ri == ci).astype(F32)

    for ch in range(tc // c):
        rows = slice(ch * c, (ch + 1) * c)
        gc = jnp.dot(ltri, g_all[rows, :], precision=HIGHEST, preferred_element_type=F32)
        gct = gc.T
        for h in range(DN_HEADS):
            cols = slice(h * DN_HEAD_DIM, (h + 1) * DN_HEAD_DIM)
            qh = q_all[rows, cols]
            kh = k_all[rows, cols]
            vh = v_all[rows, cols]
            qh = qh * lax.rsqrt(jnp.sum(qh * qh, axis=-1, keepdims=True) + RMS_EPS) * (DN_HEAD_DIM ** -0.5)
            kh = kh * lax.rsqrt(jnp.sum(kh * kh, axis=-1, keepdims=True) + RMS_EPS)
            gcol = gc[:, h:h + 1]
            grow = gct[h:h + 1, :]
            bcol = beta_all[rows, DN_HEADS + h:DN_HEADS + h + 1]
            glast = gc[c - 1:c, h:h + 1]
            decay = jnp.exp(jnp.where(lower_incl, gcol - grow, -jnp.inf))
            eg = jnp.exp(gcol)
            ek = jnp.exp(glast - gcol)
            kb = kh * bcol
            kh_b = kh.astype(BF16)
            a = jnp.where(lower_strict, _dot_nt(kb.astype(BF16), kh_b) * decay, 0.0)
            tinv = eye - a
            apow = a
            steps = int(math.log2(c)) - 1
            for _ in range(steps):
                ab16 = apow.astype(BF16)
                apow = _dot(ab16, ab16)
                tinv = tinv + _dot(tinv.astype(BF16), apow.astype(BF16))
            rhs = jnp.concatenate([vh * bcol, kb * eg], axis=1).astype(BF16)
            sol = _dot(tinv.astype(BF16), rhs)
            u = sol[:, :DN_HEAD_DIM]
            w = sol[:, DN_HEAD_DIM:]
            qk = jnp.where(lower_incl, _dot_nt(qh.astype(BF16), kh_b) * decay, 0.0)
            q_dec = (qh * eg).astype(BF16)
            k_dec = kh * ek
            s = state_ref[h]
            s16 = s.astype(BF16)
            v_new = u - _dot(w.astype(BF16), s16)
            v16 = v_new.astype(BF16)
            o = _dot(q_dec, s16) + _dot(qk.astype(BF16), v16)
            state_ref[h] = s * jnp.exp(glast) + _dot(k_dec.T.astype(BF16), v16)
            z = dz_ref[rows, cols]
            o = o * lax.rsqrt(jnp.mean(o * o, axis=-1, keepdims=True) + RMS_EPS) * normw_ref[...]
            o_ref[rows, cols] = o * (z * _sigmoid(z))


def _deltanet(dq, dk, dv, dz, ab, conv_w, a_log, dt_bias, norm_w, batch, t):
    tc = 512
    hp = jnp.zeros((SUBLANES, LANES), F32)
    hp = hp.at[0, :DN_HEADS].set(-jnp.exp(a_log.astype(F32)))
    hp = hp.at[1, :DN_HEADS].set(dt_bias.astype(F32))
    steps = t // tc
    row = lambda b, s: (b * steps + s, 0)
    const = lambda b, s: (0, 0)
    return pl.pallas_call(
        _dn_kernel,
        grid=(batch, steps),
        in_specs=[
            pl.BlockSpec((DN_CONV, 3 * DN_WIDTH), const),
            pl.BlockSpec((SUBLANES, LANES), const),
            pl.BlockSpec((1, DN_HEAD_DIM), const),
            pl.BlockSpec((tc, DN_WIDTH), row),
            pl.BlockSpec((tc, DN_WIDTH), row),
            pl.BlockSpec((tc, DN_WIDTH), row),
            pl.BlockSpec((tc, DN_WIDTH), row),
            pl.BlockSpec((tc, LANES), row),
        ],
        out_specs=pl.BlockSpec((tc, DN_WIDTH), row),
        out_shape=jax.ShapeDtypeStruct((batch * t, DN_WIDTH), F32),
        scratch_shapes=[
            pltpu.VMEM((DN_HEADS, DN_HEAD_DIM, DN_HEAD_DIM), F32),
            pltpu.VMEM((3, SUBLANES, DN_WIDTH), F32),
            pltpu.VMEM((tc + SUBLANES, DN_WIDTH), F32),
        ],
        compiler_params=pltpu.CompilerParams(dimension_semantics=("arbitrary", "arbitrary"),
                                             vmem_limit_bytes=VMEM_LIMIT),
        name="gated_deltanet",
    )(conv_w.astype(F32), hp, norm_w.astype(F32).reshape(1, DN_HEAD_DIM), dq, dk, dv, dz, ab)


def _post_kernel(a_ref, dn_ref, x_ref, wo_ref, bo_ref, g_ref, b_ref, rw_ref, rb_ref,
                 x1_ref, slab_ref, ti_ref, gate_ref, rank_ref, cnt_ref, carry_ref):
    @pl.when(pl.program_id(0) == 0)
    def _():
        carry_ref[...] = jnp.zeros_like(carry_ref)

    mix = jnp.concatenate([a_ref[...], dn_ref[...]], axis=1).astype(BF16)
    h = _dot(mix, wo_ref[...]) + bo_ref[...]
    x1 = _layer_norm(DEEPNORM_ALPHA * x_ref[...] + h, g_ref[...], b_ref[...])
    _route_tail(x1, rw_ref, rb_ref, carry_ref, x1_ref, slab_ref, ti_ref, gate_ref, rank_ref, cnt_ref)


def _router_params(router_w, router_b):
    rw = jnp.pad(router_w.astype(F32), ((0, 0), (0, LANES - N_EXPERTS)))
    rb = jnp.pad(router_b.astype(F32), (0, LANES - N_EXPERTS)).reshape(1, LANES)
    return rw, rb


def _post(a_out, dn_out, x2d, w_out, b_out, ln_g, ln_b, router_w, router_b):
    n = x2d.shape[0]
    tm = 256
    rw, rb = _router_params(router_w, router_b)
    row = lambda i: (i, 0)
    const = lambda i: (0, 0)
    vec = lambda v: v.astype(F32).reshape(1, D_MODEL)
    return pl.pallas_call(
        _post_kernel,
        grid=(n // tm,),
        in_specs=[
            pl.BlockSpec((tm, SWA_Q_WIDTH), row),
            pl.BlockSpec((tm, DN_WIDTH), row),
            pl.BlockSpec((tm, D_MODEL), row),
            pl.BlockSpec((SWA_Q_WIDTH + DN_WIDTH, D_MODEL), const),
            pl.BlockSpec((1, D_MODEL), const),
            pl.BlockSpec((1, D_MODEL), const),
            pl.BlockSpec((1, D_MODEL), const),
            pl.BlockSpec((D_MODEL, LANES), const),
            pl.BlockSpec((1, LANES), const),
        ],
        out_specs=_route_out_specs(tm, lambda i: i),
        out_shape=_route_out_shapes(n),
        scratch_shapes=[pltpu.VMEM((1, LANES), F32)],
        compiler_params=pltpu.CompilerParams(dimension_semantics=("arbitrary",),
                                             vmem_limit_bytes=VMEM_LIMIT),
        name="outproj_ln_router",
    )(a_out, dn_out, x2d, w_out.astype(BF16), vec(b_out), vec(ln_g), vec(ln_b), rw, rb)


def _pool_kernel(x_ref, pw_ref, pb_ref, ps_ref, g_ref, b_ref, rw_ref, rb_ref,
                 x1_ref, slab_ref, ti_ref, gate_ref, rank_ref, cnt_ref,
                 carry_ref, halo_ref, pad_ref):
    tm = x_ref.shape[0]
    halo = 2 * SUBLANES
    b = pl.program_id(0)
    s = pl.program_id(1)

    @pl.when((b == 0) & (s == 0))
    def _():
        carry_ref[...] = jnp.zeros_like(carry_ref)

    @pl.when(s == 0)
    def _():
        halo_ref[...] = jnp.zeros_like(halo_ref)

    x = x_ref[...]
    pad_ref[0:halo, :] = halo_ref[...]
    pad_ref[halo:, :] = x
    halo_ref[...] = x[tm - halo:tm, :]
    tpos = s * tm + lax.broadcasted_iota(jnp.int32, (tm, 1), 0)
    outs = []
    for gi, win in enumerate(POOL_WINDOWS):
        cols = slice(gi * POOL_GROUP, (gi + 1) * POOL_GROUP)
        acc = x[:, cols]
        for j in range(1, win):
            acc = acc + pad_ref[pl.ds(halo - j, tm), cols]
        cnt = jnp.minimum(tpos + 1, win).astype(F32)
        pooled = acc / cnt - x[:, cols]
        y = _dot(pooled.astype(BF16), pw_ref[gi]) + pb_ref[gi:gi + 1, :]
        outs.append(y)
    h = jnp.concatenate(outs, axis=1) * ps_ref[...]
    x1 = _layer_norm(DEEPNORM_ALPHA * x + h, g_ref[...], b_ref[...])
    _route_tail(x1, rw_ref, rb_ref, carry_ref, x1_ref, slab_ref, ti_ref, gate_ref, rank_ref, cnt_ref)


def _pool(x2d, pool_w, pool_b, pool_scale, ln_g, ln_b, router_w, router_b, batch, t):
    n = x2d.shape[0]
    tm = 256
    steps = t // tm
    rw, rb = _router_params(router_w, router_b)
    row = lambda b, s: (b * steps + s, 0)
    const2 = lambda b, s: (0, 0)
    vec = lambda v: v.astype(F32).reshape(1, D_MODEL)
    ng = len(POOL_WINDOWS)
    return pl.pallas_call(
        _pool_kernel,
        grid=(batch, steps),
        in_specs=[
            pl.BlockSpec((tm, D_MODEL), row),
            pl.BlockSpec((ng, POOL_GROUP, POOL_GROUP), lambda b, s: (0, 0, 0)),
            pl.BlockSpec((ng, POOL_GROUP), const2),
            pl.BlockSpec((1, D_MODEL), const2),
            pl.BlockSpec((1, D_MODEL), const2),
            pl.BlockSpec((1, D_MODEL), const2),
            pl.BlockSpec((D_MODEL, LANES), const2),
            pl.BlockSpec((1, LANES), const2),
        ],
        out_specs=_route_out_specs(tm, lambda b, s: b * steps + s),
        out_shape=_route_out_shapes(n),
        scratch_shapes=[
            pltpu.VMEM((1, LANES), F32),
            pltpu.VMEM((2 * SUBLANES, D_MODEL), F32),
            pltpu.VMEM((tm + 2 * SUBLANES, D_MODEL), F32),
        ],
        compiler_params=pltpu.CompilerParams(dimension_semantics=("arbitrary", "arbitrary"),
                                             vmem_limit_bytes=VMEM_LIMIT),
        name="pool_ln_router",
    )(x2d, pool_w.astype(BF16), pool_b.astype(F32), vec(pool_scale), vec(ln_g), vec(ln_b), rw, rb)


def _dispatch_kernel(starts_ref, ti_ref, rank_ref, x_hbm, xs_in_hbm, pos_ref, xs_hbm, sem):
    del xs_in_hbm
    td = ti_ref.shape[0] // TOP_K
    base = pl.program_id(0) * td

    def body(t, carry):
        for k in range(TOP_K):
            a = t * TOP_K + k
            p = starts_ref[ti_ref[a]] + rank_ref[a]
            pos_ref[a] = p
            pltpu.make_async_copy(x_hbm.at[base + t], xs_hbm.at[p], sem).start()
        return carry

    lax.fori_loop(0, td, body, 0)
    pltpu.make_async_copy(xs_hbm.at[pl.ds(0, td * TOP_K)], xs_hbm.at[pl.ds(0, td * TOP_K)], sem).wait()


def _dispatch(starts, ti_flat, rank_flat, x_slab3, rows):
    n = x_slab3.shape[0]
    td = 512
    zeros = jnp.zeros((rows, SLAB, LANES), F32)
    grid_spec = pltpu.PrefetchScalarGridSpec(
        num_scalar_prefetch=1,
        grid=(n // td,),
        in_specs=[
            pl.BlockSpec((td * TOP_K,), lambda i, st: (i,), memory_space=pltpu.SMEM),
            pl.BlockSpec((td * TOP_K,), lambda i, st: (i,), memory_space=pltpu.SMEM),
            pl.BlockSpec(memory_space=pl.ANY),
            pl.BlockSpec(memory_space=pl.ANY),
        ],
        out_specs=(
            pl.BlockSpec((td * TOP_K,), lambda i, st: (i,), memory_space=pltpu.SMEM),
            pl.BlockSpec(memory_space=pl.ANY),
        ),
        scratch_shapes=[pltpu.SemaphoreType.DMA(())],
    )
    return pl.pallas_call(
        _dispatch_kernel,
        grid_spec=grid_spec,
        out_shape=(
            jax.ShapeDtypeStruct((n * TOP_K,), jnp.int32),
            jax.ShapeDtypeStruct((rows, SLAB, LANES), F32),
        ),
        input_output_aliases={4: 1},
        compiler_params=pltpu.CompilerParams(dimension_semantics=("arbitrary",),
                                             has_side_effects=True),
        name="moe_dispatch",
    )(starts, ti_flat, rank_flat, x_slab3, zeros)


def _moe_kernel(te_ref, nu_ref, x_ref, w1_ref, b1_ref, w2_ref, b2_ref, o_ref, w1p_ref, w2b_ref):
    i = pl.program_id(0)
    tm = MOE_TM
    used = i < nu_ref[0]
    first = (i == 0) | (te_ref[i] != te_ref[jnp.maximum(i - 1, 0)])
    pw = 2 * LANES

    @pl.when(used & first)
    def _():
        r = lax.broadcasted_iota(jnp.int32, (pw, pw), 0)
        cidx = lax.broadcasted_iota(jnp.int32, (pw, pw), 1)
        src = jnp.where(cidx < LANES, 2 * cidx, 2 * (cidx - LANES) + 1)
        perm = (r == src).astype(BF16)
        for blk in range(2 * D_EXPERT // pw):
            wb = w1_ref[0, :, blk * pw:(blk + 1) * pw].astype(BF16)
            w1p_ref[:, blk * pw:(blk + 1) * pw] = _dot(wb, perm).astype(BF16)
        w2b_ref[...] = w2_ref[0].astype(BF16)

    @pl.when(used)
    def _():
        x = jnp.concatenate([x_ref[pl.ds(s, tm, stride=SLAB), :] for s in range(SLAB)],
                            axis=1).astype(BF16)
        acc = jnp.zeros((tm, D_MODEL), F32)
        for ch in range(D_EXPERT // pw):
            cols = slice(ch * 2 * pw, (ch + 1) * 2 * pw)
            h = _dot(x, w1p_ref[:, cols]) + b1_ref[0, :, cols]
            acts = []
            for j in range(2):
                hg = jnp.minimum(h[:, j * pw:j * pw + LANES], SWIGLU_LIMIT)
                hl = jnp.clip(h[:, j * pw + LANES:(j + 1) * pw], -SWIGLU_LIMIT, SWIGLU_LIMIT)
                acts.append(hg * _sigmoid(SWIGLU_ALPHA * hg) * (hl + 1.0))
            act = jnp.concatenate(acts, axis=1).astype(BF16)
            acc = acc + _dot(act, w2b_ref[ch * pw:(ch + 1) * pw, :])
        y = acc + b2_ref[0]
        for s in range(SLAB):
            o_ref[pl.ds(s, tm, stride=SLAB), :] = y[:, s * LANES:(s + 1) * LANES]

    @pl.when(jnp.logical_not(used))
    def _():
        o_ref[...] = jnp.zeros_like(o_ref)


def _moe_mlp(tile_expert, n_used, xs2d, w1, b1p, w2, b2, n_tiles):
    tm = MOE_TM
    last = lambda i, te, nu: jnp.minimum(i, nu[0] - 1)
    grid_spec = pltpu.PrefetchScalarGridSpec(
        num_scalar_prefetch=2,
        grid=(n_tiles,),
        in_specs=[
            pl.BlockSpec((tm * SLAB, LANES), lambda i, te, nu: (last(i, te, nu), 0)),
            pl.BlockSpec((1, D_MODEL, 2 * D_EXPERT), lambda i, te, nu: (te[i], 0, 0)),
            pl.BlockSpec((1, 1, 2 * D_EXPERT), lambda i, te, nu: (te[i], 0, 0)),
            pl.BlockSpec((1, D_EXPERT, D_MODEL), lambda i, te, nu: (te[i], 0, 0)),
            pl.BlockSpec((1, 1, D_MODEL), lambda i, te, nu: (te[i], 0, 0)),
        ],
        out_specs=pl.BlockSpec((tm * SLAB, LANES), lambda i, te, nu: (i, 0)),
        scratch_shapes=[
            pltpu.VMEM((D_MODEL, 2 * D_EXPERT), BF16),
            pltpu.VMEM((D_EXPERT, D_MODEL), BF16),
        ],
    )
    return pl.pallas_call(
        _moe_kernel,
        grid_spec=grid_spec,
        out_shape=jax.ShapeDtypeStruct((n_tiles * tm * SLAB, LANES), F32),
        compiler_params=pltpu.CompilerParams(dimension_semantics=("arbitrary",),
                                             vmem_limit_bytes=VMEM_LIMIT),
        name="moe_grouped_mlp",
    )(tile_expert, n_used, xs2d, w1, b1p, w2, b2)


def _combine_kernel(pos_ref, os_hbm, gate_ref, x1_ref, g_ref, b_ref, o_ref, buf_ref, sem):
    tc = x1_ref.shape[0]
    na = tc * TOP_K

    def body(a, carry):
        dst = pl.multiple_of(a * SLAB, SLAB)
        pltpu.make_async_copy(os_hbm.at[pos_ref[a]], buf_ref.at[pl.ds(dst, SLAB)], sem).start()
        return carry

    lax.fori_loop(0, na, body, 0)
    pltpu.make_async_copy(buf_ref, buf_ref, sem).wait()

    gate = gate_ref[...]
    cols = []
    for s in range(SLAB):
        acc = jnp.zeros((tc, LANES), F32)
        for k in range(TOP_K):
            rows = buf_ref[pl.ds(k * SLAB + s, tc, stride=TOP_K * SLAB), :]
            acc = acc + gate[:, k:k + 1] * rows
        cols.append(acc)
    f = jnp.concatenate(cols, axis=1)
    o_ref[...] = _layer_norm(DEEPNORM_ALPHA * x1_ref[...] + f, g_ref[...], b_ref[...])


def _combine(pos_flat, out_sorted3, gate, x1, ln_g, ln_b):
    n = x1.shape[0]
    tc = 256
    vec = lambda v: v.astype(F32).reshape(1, D_MODEL)
    return pl.pallas_call(
        _combine_kernel,
        grid=(n // tc,),
        in_specs=[
            pl.BlockSpec((tc * TOP_K,), lambda i: (i,), memory_space=pltpu.SMEM),
            pl.BlockSpec(memory_space=pl.ANY),
            pl.BlockSpec((tc, TOP_K), lambda i: (i, 0)),
            pl.BlockSpec((tc, D_MODEL), lambda i: (i, 0)),
            pl.BlockSpec((1, D_MODEL), lambda i: (0, 0)),
            pl.BlockSpec((1, D_MODEL), lambda i: (0, 0)),
        ],
        out_specs=pl.BlockSpec((tc, D_MODEL), lambda i: (i, 0)),
        out_shape=jax.ShapeDtypeStruct((n, D_MODEL), F32),
        scratch_shapes=[
            pltpu.VMEM((tc * TOP_K * SLAB, LANES), F32),
            pltpu.SemaphoreType.DMA(()),
        ],
        compiler_params=pltpu.CompilerParams(dimension_semantics=("arbitrary",),
                                             vmem_limit_bytes=VMEM_LIMIT),
        name="moe_combine_ln",
    )(pos_flat, out_sorted3, gate, x1, vec(ln_g), vec(ln_b))


def _moe_layer(routed, w1, b1, w2, b2, ln_g, ln_b):
    x1, x1_slab, ti, gate, rank, counts = routed
    n = x1.shape[0]
    tm = MOE_TM
    n_tiles = n * TOP_K // tm + N_EXPERTS
    rows = n_tiles * tm
    cnt = counts[0, :N_EXPERTS]
    tiles_per = (cnt + tm - 1) // tm
    tile_end = jnp.cumsum(tiles_per)
    starts = ((tile_end - tiles_per) * tm).astype(jnp.int32)
    n_used = tile_end[-1:].astype(jnp.int32)
    tidx = jnp.arange(n_tiles, dtype=jnp.int32)
    te = jnp.sum((tidx[:, None] >= tile_end[None, :]).astype(jnp.int32), axis=1)
    te_last = jnp.max(jnp.where(tiles_per > 0, jnp.arange(N_EXPERTS, dtype=jnp.int32), 0))
    tile_expert = jnp.where(tidx < n_used[0], jnp.minimum(te, N_EXPERTS - 1), te_last).astype(jnp.int32)

    pos, xs = _dispatch(starts, ti.reshape(-1), rank.reshape(-1),
                        x1_slab.reshape(n, SLAB, LANES), rows)
    pw = 2 * LANES
    b1p = b1.astype(F32).reshape(N_EXPERTS, 2 * D_EXPERT // pw, LANES, 2)
    b1p = jnp.swapaxes(b1p, 2, 3).reshape(N_EXPERTS, 1, 2 * D_EXPERT)
    out_sorted = _moe_mlp(tile_expert, n_used, xs.reshape(rows * SLAB, LANES), w1, b1p, w2,
                          b2.astype(F32).reshape(N_EXPERTS, 1, D_MODEL), n_tiles)
    return _combine(pos, out_sorted.reshape(rows, SLAB, LANES), gate, x1, ln_g, ln_b)


def kernel(x, positions, mix_w_in, mix_b_in, dn_conv_w, dn_a_log, dn_dt_bias, dn_norm_w, swa_sinks,
           mix_w_out, mix_b_out, pool_w, pool_b, pool_scale, ln1_g, ln1_b, router_w, router_b,
           moe_w1, moe_b1, moe_w2, moe_b2, ln2_g, ln2_b):
    batch, t, d = x.shape
    n = batch * t
    x2d = x.reshape(n, d)
    pos2d = positions.reshape(n, 1).astype(jnp.int32)
    for layer in range(DEPTH):
        i = layer // 2
        if layer % 2 == 0:
            q, kv, dq, dk, dv, dz, ab = _inproj(x2d, pos2d, mix_w_in[i], mix_b_in[i])
            a_out = _swa(q, kv, swa_sinks[i], batch, t)
            dn_out = _deltanet(dq, dk, dv, dz, ab, dn_conv_w[i], dn_a_log[i], dn_dt_bias[i],
                               dn_norm_w[i], batch, t)
            routed = _post(a_out, dn_out, x2d, mix_w_out[i], mix_b_out[i], ln1_g[layer], ln1_b[layer],
                           router_w[layer], router_b[layer])
        else:
            routed = _pool(x2d, pool_w[i], pool_b[i], pool_scale[i], ln1_g[layer], ln1_b[layer],
                           router_w[layer], router_b[layer], batch, t)
        x2d = _moe_layer(routed, moe_w1[layer], moe_b1[layer], moe_w2[layer], moe_b2[layer],
                         ln2_g[layer], ln2_b[layer])
    return x2d.reshape(batch, t, d)
```

```python
import functools
import math

import numpy as np
import jax
import jax.numpy as jnp
from jax import lax
from jax.experimental import pallas as pl
from jax.experimental.pallas import tpu as pltpu

D_MODEL = 1024
DEPTH = 2
SWA_Q_HEADS = 8
SWA_KV_HEADS = 2
SWA_HEAD_DIM = 64
SWA_BLOCK = 128
ROPE_THETA = 500000.0
ROPE_DIM = SWA_HEAD_DIM // 4
DN_HEADS = 4
DN_HEAD_DIM = 128
DN_CONV = 4
POOL_WINDOWS = (2, 4, 8, 16)
POOL_GROUP = D_MODEL // 4
N_EXPERTS = 32
TOP_K = 4
D_EXPERT = D_MODEL
SWIGLU_LIMIT = 7.0
SWIGLU_ALPHA = 1.702
LN_EPS = 1e-5
RMS_EPS = 1e-6
DEEPNORM_ALPHA = (2 * DEPTH) ** 0.25
SWA_Q_WIDTH = SWA_Q_HEADS * SWA_HEAD_DIM
SWA_KV_WIDTH = SWA_KV_HEADS * SWA_HEAD_DIM
DN_WIDTH = DN_HEADS * DN_HEAD_DIM

LANES = 128
SUBLANES = 8
SLAB = D_MODEL // LANES
IN_PAD_WIDTH = 3328
VMEM_LIMIT = 56 * 1024 * 1024

F32 = jnp.float32
BF16 = jnp.bfloat16
HIGHEST = lax.Precision.HIGHEST

MOE_TM = 256
DN_CHUNK = 128
POOL_HALO = SUBLANES * len(POOL_WINDOWS)


def _sigmoid(x):
    return 1.0 / (1.0 + jnp.exp(-x))


def _layer_norm(z, g, b):
    mu = jnp.mean(z, axis=-1, keepdims=True)
    zc = z - mu
    var = jnp.mean(zc * zc, axis=-1, keepdims=True)
    return zc * lax.rsqrt(var + LN_EPS) * g + b


def _dot(a, b):
    return jnp.dot(a, b, preferred_element_type=F32)


def _dot_nt(a, b):
    return lax.dot_general(a, b, (((1,), (1,)), ((), ())), preferred_element_type=F32)


def _route_tail(x1, rw_ref, rb_ref, carry_ref, x1_ref, slab_ref, ti_ref, gate_ref, rank_ref, cnt_ref):
    tm = x1.shape[0]
    x1_ref[...] = x1
    for s in range(SLAB):
        slab_ref[pl.ds(s, tm, stride=SLAB), :] = x1[:, s * LANES:(s + 1) * LANES]

    xh = x1.astype(BF16)
    xl = (x1 - xh.astype(F32)).astype(BF16)
    logits = _dot(xh, rw_ref[0]) + (_dot(xh, rw_ref[1]) + _dot(xl, rw_ref[0])) + rb_ref[...]
    lane = lax.broadcasted_iota(jnp.int32, (tm, LANES), 1)
    lane_f = lane.astype(F32)
    l = jnp.where(lane < N_EXPERTS, logits, -jnp.inf)
    vals, hits = [], []
    ti_out = jnp.zeros((tm, LANES), jnp.int32)
    for k in range(TOP_K):
        m = jnp.max(l, axis=-1, keepdims=True)
        idx = jnp.min(jnp.where(l == m, lane_f, float(LANES)), axis=-1, keepdims=True)
        hit = lane_f == idx
        l = jnp.where(hit, -jnp.inf, l)
        vals.append(m)
        hits.append(hit)
        ti_out = jnp.where(lane == k, idx.astype(jnp.int32), ti_out)
    exps = [jnp.exp(v - vals[0]) for v in vals]
    den = exps[0] + exps[1] + exps[2] + exps[3]
    gate_out = jnp.zeros((tm, LANES), F32)
    for k in range(TOP_K):
        gate_out = jnp.where(lane == k, exps[k] / den, gate_out)

    sel = jnp.zeros((tm, LANES), F32)
    for k in range(TOP_K):
        sel = sel + hits[k].astype(F32)
    ri = lax.broadcasted_iota(jnp.int32, (tm, tm), 0)
    ci = lax.broadcasted_iota(jnp.int32, (tm, tm), 1)
    tri = (ri > ci).astype(BF16)
    carry = carry_ref[...]
    prefix = _dot(tri, sel.astype(BF16)) + carry
    rank_out = jnp.zeros((tm, LANES), jnp.int32)
    for k in range(TOP_K):
        r = jnp.sum(jnp.where(hits[k], prefix, 0.0), axis=-1, keepdims=True)
        rank_out = jnp.where(lane == k, r.astype(jnp.int32), rank_out)
    new_carry = carry + jnp.sum(sel, axis=0, keepdims=True)
    carry_ref[...] = new_carry
    cnt_ref[...] = new_carry.astype(jnp.int32)
    ti_ref[...] = ti_out[:, :TOP_K]
    gate_ref[...] = gate_out[:, :TOP_K]
    rank_ref[...] = rank_out[:, :TOP_K]


def _route_out_shapes(n):
    return (
        jax.ShapeDtypeStruct((n, D_MODEL), F32),
        jax.ShapeDtypeStruct((n * SLAB, LANES), F32),
        jax.ShapeDtypeStruct((n, TOP_K), jnp.int32),
        jax.ShapeDtypeStruct((n, TOP_K), F32),
        jax.ShapeDtypeStruct((n, TOP_K), jnp.int32),
        jax.ShapeDtypeStruct((1, LANES), jnp.int32),
    )


def _route_out_specs(tm, row_map):
    return (
        pl.BlockSpec((tm, D_MODEL), lambda *a: (row_map(*a), 0)),
        pl.BlockSpec((tm * SLAB, LANES), lambda *a: (row_map(*a), 0)),
        pl.BlockSpec((tm, TOP_K), lambda *a: (row_map(*a), 0)),
        pl.BlockSpec((tm, TOP_K), lambda *a: (row_map(*a), 0)),
        pl.BlockSpec((tm, TOP_K), lambda *a: (row_map(*a), 0)),
        pl.BlockSpec((1, LANES), lambda *a: (0, 0)),
    )


def _inproj_kernel(x_ref, pos_ref, invf_ref, w_ref, b_ref,
                   q_ref, kv_ref, dq_ref, dk_ref, dv_ref, dz_ref, ab_ref):
    x = x_ref[...].astype(BF16)
    proj = _dot(x, w_ref[...]) + b_ref[...]
    tm = x.shape[0]
    ang = pos_ref[...].astype(F32) * invf_ref[...]
    cos = jnp.cos(ang)
    sin = jnp.sin(ang)
    d = lax.broadcasted_iota(jnp.int32, (tm, LANES), 1) % SWA_HEAD_DIM
    half = ROPE_DIM // 2
    c_tab = jnp.where(d < ROPE_DIM, cos, 1.0)
    s_lo = jnp.where(d < half, -sin, 0.0)
    s_hi = jnp.where((d >= half) & (d < ROPE_DIM), sin, 0.0)

    def rot(xc):
        return (xc * c_tab + pltpu.roll(xc, LANES - half, 1) * s_lo
                + pltpu.roll(xc, half, 1) * s_hi)

    for c in range(SWA_Q_WIDTH // LANES):
        q_ref[:, c * LANES:(c + 1) * LANES] = rot(proj[:, c * LANES:(c + 1) * LANES])
    o = SWA_Q_WIDTH
    kv_ref[:, :LANES] = rot(proj[:, o:o + LANES])
    kv_ref[:, LANES:] = proj[:, o + LANES:o + 2 * LANES]
    o += 2 * SWA_KV_WIDTH
    dq_ref[...] = proj[:, o:o + DN_WIDTH]
    dk_ref[...] = proj[:, o + DN_WIDTH:o + 2 * DN_WIDTH]
    dv_ref[...] = proj[:, o + 2 * DN_WIDTH:o + 3 * DN_WIDTH]
    o += 3 * DN_WIDTH
    dz_ref[...] = proj[:, o:o + DN_WIDTH]
    o += DN_WIDTH
    ab_ref[...] = proj[:, o:o + LANES]


def _inproj(x2d, pos2d, w_in, b_in):
    n = x2d.shape[0]
    tm = 256
    in_width = w_in.shape[1]
    w = jnp.pad(w_in, ((0, 0), (0, IN_PAD_WIDTH - in_width))).astype(BF16)
    b = jnp.pad(b_in, (0, IN_PAD_WIDTH - in_width)).reshape(1, IN_PAD_WIDTH)
    half = ROPE_DIM // 2
    lane_d = np.arange(LANES) % SWA_HEAD_DIM
    invf = (ROPE_THETA ** (-(lane_d % half).astype(np.float64) / half)).astype(np.float32)
    invf = jnp.asarray(invf.reshape(1, LANES))
    row = lambda i: (i, 0)
    const = lambda i: (0, 0)
    widths = (SWA_Q_WIDTH, 2 * SWA_KV_WIDTH, DN_WIDTH, DN_WIDTH, DN_WIDTH, DN_WIDTH, LANES)
    return pl.pallas_call(
        _inproj_kernel,
        grid=(n // tm,),
        in_specs=[
            pl.BlockSpec((tm, D_MODEL), row),
            pl.BlockSpec((tm, 1), row),
            pl.BlockSpec((1, LANES), const),
            pl.BlockSpec((D_MODEL, IN_PAD_WIDTH), const),
            pl.BlockSpec((1, IN_PAD_WIDTH), const),
        ],
        out_specs=tuple(pl.BlockSpec((tm, wd), row) for wd in widths),
        out_shape=tuple(jax.ShapeDtypeStruct((n, wd), F32) for wd in widths),
        compiler_params=pltpu.CompilerParams(dimension_semantics=("arbitrary",),
                                             vmem_limit_bytes=VMEM_LIMIT),
        name="inproj_rotary",
    )(x2d, pos2d, invf, w, b)


def _swa_kernel(sink_ref, q_ref, kv_ref, o_ref):
    t = q_ref.shape[0]
    blk = SWA_BLOCK
    lane = lax.broadcasted_iota(jnp.int32, (2 * blk, LANES), 1)
    qi = lax.broadcasted_iota(jnp.int32, (blk, 2 * blk), 0)
    kj = lax.broadcasted_iota(jnp.int32, (blk, 2 * blk), 1)
    rel = qi + blk - kj
    in_window = (rel >= 0) & (rel < blk)
    is_cur = kj >= blk
    scale = SWA_HEAD_DIM ** -0.5

    def body(n, carry):
        r0 = pl.multiple_of(n * blk, blk)
        p0 = pl.multiple_of(jnp.maximum(n - 1, 0) * blk, blk)
        kvc = kv_ref[pl.ds(r0, blk), :]
        kvp = kv_ref[pl.ds(p0, blk), :]
        kband = jnp.concatenate([kvp[:, :LANES], kvc[:, :LANES]], axis=0)
        vband = jnp.concatenate([kvp[:, LANES:], kvc[:, LANES:]], axis=0)
        valid = in_window & (is_cur | (n > 0))
        k_rhs, v_rhs = [], []
        for h in range(SWA_KV_HEADS):
            in_head = (lane >= h * SWA_HEAD_DIM) & (lane < (h + 1) * SWA_HEAD_DIM)
            km = jnp.where(in_head, kband, 0.0)
            vm = jnp.where(in_head, vband, 0.0)
            kr = pltpu.roll(km, SWA_HEAD_DIM, 1)
            vr = pltpu.roll(vm, SWA_HEAD_DIM, 1)
            lo_k, hi_k = (km, kr) if h == 0 else (kr, km)
            lo_v, hi_v = (vm, vr) if h == 0 else (vr, vm)
            k_rhs.append(jnp.concatenate([lo_k, hi_k], axis=0).astype(BF16))
            v_rhs.append(jnp.concatenate([lo_v, hi_v], axis=0).astype(BF16))
        group = SWA_Q_HEADS // SWA_KV_HEADS
        for c in range(SWA_Q_WIDTH // LANES):
            h = (2 * c) // group
            qc = (q_ref[pl.ds(r0, blk), c * LANES:(c + 1) * LANES] * scale).astype(BF16)
            s_both = _dot_nt(qc, k_rhs[h])
            ps = []
            for j in range(2):
                s = jnp.where(valid, s_both[:, j * 2 * blk:(j + 1) * 2 * blk], -jnp.inf)
                sink = sink_ref[2 * c + j]
                m = jnp.maximum(jnp.max(s, axis=-1, keepdims=True), sink)
                p = jnp.exp(s - m)
                den = jnp.sum(p, axis=-1, keepdims=True) + jnp.exp(sink - m)
                ps.append((p / den).astype(BF16))
            p_both = jnp.concatenate(ps, axis=1)
            o_ref[pl.ds(r0, blk), c * LANES:(c + 1) * LANES] = _dot(p_both, v_rhs[h])
        return carry

    lax.fori_loop(0, t // blk, body, 0)


def _swa(q, kv, sinks, batch, t):
    return pl.pallas_call(
        _swa_kernel,
        grid=(batch,),
        in_specs=[
            pl.BlockSpec(memory_space=pltpu.SMEM),
            pl.BlockSpec((t, SWA_Q_WIDTH), lambda b: (b, 0)),
            pl.BlockSpec((t, 2 * SWA_KV_WIDTH), lambda b: (b, 0)),
        ],
        out_specs=pl.BlockSpec((t, SWA_Q_WIDTH), lambda b: (b, 0)),
        out_shape=jax.ShapeDtypeStruct((batch * t, SWA_Q_WIDTH), F32),
        compiler_params=pltpu.CompilerParams(dimension_semantics=("arbitrary",),
                                             vmem_limit_bytes=VMEM_LIMIT),
        name="swa_attention",
    )(sinks.astype(F32), q, kv)


def _dn_kernel(convw_ref, hp_ref, normw_ref, dq_ref, dk_ref, dv_ref, dz_ref, ab_ref, o_ref,
               state_ref, halo_ref, pad_ref):
    tc = dq_ref.shape[0]
    c = DN_CHUNK
    halo = SUBLANES

    @pl.when(pl.program_id(1) == 0)
    def _():
        state_ref[...] = jnp.zeros_like(state_ref)
        halo_ref[...] = jnp.zeros_like(halo_ref)

    def conv_silu(x_ref, idx):
        pad_ref[0:halo, :] = halo_ref[idx]
        pad_ref[halo:, :] = x_ref[...]
        halo_ref[idx] = x_ref[tc - halo:tc, :]
        acc = jnp.zeros((tc, DN_WIDTH), F32)
        for j in range(DN_CONV):
            wj = convw_ref[j:j + 1, idx * DN_WIDTH:(idx + 1) * DN_WIDTH]
            acc = acc + wj * pad_ref[pl.ds(halo - DN_CONV + 1 + j, tc), :]
        return acc * _sigmoid(acc)

    pad_q = conv_silu(dq_ref, 0)
    q_all = pad_q
    k_all = conv_silu(dk_ref, 1)
    v_all = conv_silu(dv_ref, 2)

    ab = ab_ref[...]
    neg_a = hp_ref[0:1, :]
    dt_b = hp_ref[1:2, :]
    sp_arg = ab + dt_b
    softplus = jnp.maximum(sp_arg, 0.0) + jnp.log(1.0 + jnp.exp(-jnp.abs(sp_arg)))
    g_all = neg_a * softplus
    beta_all = _sigmoid(ab)

    ri = lax.broadcasted_iota(jnp.int32, (c, c), 0)
    ci = lax.broadcasted_iota(jnp.int32, (c, c), 1)
    lower_incl = ri >= ci
    lower_strict = ri > ci
    ltri = lower_incl.astype(F32)
    eye = (ri == ci).astype(F32)

    for ch in range(tc // c):
        rows = slice(ch * c, (ch + 1) * c)
        gc = jnp.dot(ltri, g_all[rows, :], precision=HIGHEST, preferred_element_type=F32)
        gct = gc.T
        for h in range(DN_HEADS):
            cols = slice(h * DN_HEAD_DIM, (h + 1) * DN_HEAD_DIM)
            qh = q_all[rows, cols]
            kh = k_all[rows, cols]
            vh = v_all[rows, cols]
            qh = qh * lax.rsqrt(jnp.sum(qh * qh, axis=-1, keepdims=True) + RMS_EPS) * (DN_HEAD_DIM ** -0.5)
            kh = kh * lax.rsqrt(jnp.sum(kh * kh, axis=-1, keepdims=True) + RMS_EPS)
            gcol = gc[:, h:h + 1]
            grow = gct[h:h + 1, :]
            bcol = beta_all[rows, DN_HEADS + h:DN_HEADS + h + 1]
            glast = gc[c - 1:c, h:h + 1]
            decay = jnp.exp(jnp.where(lower_incl, gcol - grow, -jnp.inf))
            eg = jnp.exp(gcol)
            ek = jnp.exp(glast - gcol)
            kb = kh * bcol
            kh_b = kh.astype(BF16)
            a = jnp.where(lower_strict, _dot_nt(kb.astype(BF16), kh_b) * decay, 0.0)
            tinv = eye - a
            apow = a
            steps = int(math.log2(c)) - 1
            for _ in range(steps):
                ab16 = apow.astype(BF16)
                apow = _dot(ab16, ab16)
                tinv = tinv + _dot(tinv.astype(BF16), apow.astype(BF16))
            rhs = jnp.concatenate([vh * bcol, kb * eg], axis=1).astype(BF16)
            sol = _dot(tinv.astype(BF16), rhs)
            u = sol[:, :DN_HEAD_DIM]
            w = sol[:, DN_HEAD_DIM:]
            qk = jnp.where(lower_incl, _dot_nt(qh.astype(BF16), kh_b) * decay, 0.0)
            q_dec = (qh * eg).astype(BF16)
            k_dec = kh * ek
            s = state_ref[h]
            s16 = s.astype(BF16)
            v_new = u - _dot(w.astype(BF16), s16)
            v16 = v_new.astype(BF16)
            o = _dot(q_dec, s16) + _dot(qk.astype(BF16), v16)
            state_ref[h] = s * jnp.exp(glast) + _dot(k_dec.T.astype(BF16), v16)
            z = dz_ref[rows, cols]
            o = o * lax.rsqrt(jnp.mean(o * o, axis=-1, keepdims=True) + RMS_EPS) * normw_ref[...]
            o_ref[rows, cols] = o * (z * _sigmoid(z))


def _deltanet(dq, dk, dv, dz, ab, conv_w, a_log, dt_bias, norm_w, batch, t):
    tc = 512
    hp = jnp.zeros((SUBLANES, LANES), F32)
    hp = hp.at[0, :DN_HEADS].set(-jnp.exp(a_log.astype(F32)))
    hp = hp.at[1, :DN_HEADS].set(dt_bias.astype(F32))
    steps = t // tc
    row = lambda b, s: (b * steps + s, 0)
    const = lambda b, s: (0, 0)
    return pl.pallas_call(
        _dn_kernel,
        grid=(batch, steps),
        in_specs=[
            pl.BlockSpec((DN_CONV, 3 * DN_WIDTH), const),
            pl.BlockSpec((SUBLANES, LANES), const),
            pl.BlockSpec((1, DN_HEAD_DIM), const),
            pl.BlockSpec((tc, DN_WIDTH), row),
            pl.BlockSpec((tc, DN_WIDTH), row),
            pl.BlockSpec((tc, DN_WIDTH), row),
            pl.BlockSpec((tc, DN_WIDTH), row),
            pl.BlockSpec((tc, LANES), row),
        ],
        out_specs=pl.BlockSpec((tc, DN_WIDTH), row),
        out_shape=jax.ShapeDtypeStruct((batch * t, DN_WIDTH), F32),
        scratch_shapes=[
            pltpu.VMEM((DN_HEADS, DN_HEAD_DIM, DN_HEAD_DIM), F32),
            pltpu.VMEM((3, SUBLANES, DN_WIDTH), F32),
            pltpu.VMEM((tc + SUBLANES, DN_WIDTH), F32),
        ],
        compiler_params=pltpu.CompilerParams(dimension_semantics=("arbitrary", "arbitrary"),
                                             vmem_limit_bytes=VMEM_LIMIT),
        name="gated_deltanet",
    )(conv_w.astype(F32), hp, norm_w.astype(F32).reshape(1, DN_HEAD_DIM), dq, dk, dv, dz, ab)


def _post_kernel(a_ref, dn_ref, x_ref, wo_ref, bo_ref, g_ref, b_ref, rw_ref, rb_ref,
                 x1_ref, slab_ref, ti_ref, gate_ref, rank_ref, cnt_ref, carry_ref):
    @pl.when(pl.program_id(0) == 0)
    def _():
        carry_ref[...] = jnp.zeros_like(carry_ref)

    mix = jnp.concatenate([a_ref[...], dn_ref[...]], axis=1).astype(BF16)
    h = _dot(mix, wo_ref[...]) + bo_ref[...]
    x1 = _layer_norm(DEEPNORM_ALPHA * x_ref[...] + h, g_ref[...], b_ref[...])
    _route_tail(x1, rw_ref, rb_ref, carry_ref, x1_ref, slab_ref, ti_ref, gate_ref, rank_ref, cnt_ref)


def _router_params(router_w, router_b):
    rw = jnp.pad(router_w.astype(F32), ((0, 0), (0, LANES - N_EXPERTS)))
    rw_hi = rw.astype(BF16)
    rw_lo = (rw - rw_hi.astype(F32)).astype(BF16)
    rb = jnp.pad(router_b.astype(F32), (0, LANES - N_EXPERTS)).reshape(1, LANES)
    return jnp.stack([rw_hi, rw_lo]), rb


def _post(a_out, dn_out, x2d, w_out, b_out, ln_g, ln_b, router_w, router_b):
    n = x2d.shape[0]
    tm = 256
    rw, rb = _router_params(router_w, router_b)
    row = lambda i: (i, 0)
    const = lambda i: (0, 0)
    vec = lambda v: v.astype(F32).reshape(1, D_MODEL)
    return pl.pallas_call(
        _post_kernel,
        grid=(n // tm,),
        in_specs=[
            pl.BlockSpec((tm, SWA_Q_WIDTH), row),
            pl.BlockSpec((tm, DN_WIDTH), row),
            pl.BlockSpec((tm, D_MODEL), row),
            pl.BlockSpec((SWA_Q_WIDTH + DN_WIDTH, D_MODEL), const),
            pl.BlockSpec((1, D_MODEL), const),
            pl.BlockSpec((1, D_MODEL), const),
            pl.BlockSpec((1, D_MODEL), const),
            pl.BlockSpec((2, D_MODEL, LANES), lambda i: (0, 0, 0)),
            pl.BlockSpec((1, LANES), const),
        ],
        out_specs=_route_out_specs(tm, lambda i: i),
        out_shape=_route_out_shapes(n),
        scratch_shapes=[pltpu.VMEM((1, LANES), F32)],
        compiler_params=pltpu.CompilerParams(dimension_semantics=("arbitrary",),
                                             vmem_limit_bytes=VMEM_LIMIT),
        name="outproj_ln_router",
    )(a_out, dn_out, x2d, w_out.astype(BF16), vec(b_out), vec(ln_g), vec(ln_b), rw, rb)


def _pool_kernel(x_ref, pw_ref, pb_ref, ps_ref, g_ref, b_ref, rw_ref, rb_ref,
                 x1_ref, slab_ref, ti_ref, gate_ref, rank_ref, cnt_ref,
                 carry_ref, halo_ref, pad_ref):
    tm = x_ref.shape[0]
    halo = POOL_HALO
    b = pl.program_id(0)
    s = pl.program_id(1)

    @pl.when((b == 0) & (s == 0))
    def _():
        carry_ref[...] = jnp.zeros_like(carry_ref)

    @pl.when(s == 0)
    def _():
        halo_ref[...] = jnp.zeros_like(halo_ref)

    x = x_ref[...]
    pad_ref[0:halo, :] = halo_ref[...]
    pad_ref[halo:, :] = x
    halo_ref[...] = x[tm - halo:tm, :]
    tpos = s * tm + lax.broadcasted_iota(jnp.int32, (tm, 1), 0)
    outs = []
    for level, win in enumerate(POOL_WINDOWS, start=1):
        lo = SUBLANES * level
        rows = tm + halo - lo
        cols = slice((level - 1) * POOL_GROUP, D_MODEL)
        shift = win // 2
        pad_ref[pl.ds(lo, rows), cols] = (pad_ref[pl.ds(lo, rows), cols]
                                          + pad_ref[pl.ds(lo - shift, rows), cols])
        gcols = slice((level - 1) * POOL_GROUP, level * POOL_GROUP)
        cnt = jnp.minimum(tpos + 1, win).astype(F32)
        pooled = pad_ref[pl.ds(halo, tm), gcols] / cnt - x[:, gcols]
        y = _dot(pooled.astype(BF16), pw_ref[level - 1]) + pb_ref[level - 1:level, :]
        outs.append(y)
    h = jnp.concatenate(outs, axis=1) * ps_ref[...]
    x1 = _layer_norm(DEEPNORM_ALPHA * x + h, g_ref[...], b_ref[...])
    _route_tail(x1, rw_ref, rb_ref, carry_ref, x1_ref, slab_ref, ti_ref, gate_ref, rank_ref, cnt_ref)


def _pool(x2d, pool_w, pool_b, pool_scale, ln_g, ln_b, router_w, router_b, batch, t):
    n = x2d.shape[0]
    tm = 256
    steps = t // tm
    rw, rb = _router_params(router_w, router_b)
    row = lambda b, s: (b * steps + s, 0)
    const2 = lambda b, s: (0, 0)
    vec = lambda v: v.astype(F32).reshape(1, D_MODEL)
    ng = len(POOL_WINDOWS)
    return pl.pallas_call(
        _pool_kernel,
        grid=(batch, steps),
        in_specs=[
            pl.BlockSpec((tm, D_MODEL), row),
            pl.BlockSpec((ng, POOL_GROUP, POOL_GROUP), lambda b, s: (0, 0, 0)),
            pl.BlockSpec((ng, POOL_GROUP), const2),
            pl.BlockSpec((1, D_MODEL), const2),
            pl.BlockSpec((1, D_MODEL), const2),
            pl.BlockSpec((1, D_MODEL), const2),
            pl.BlockSpec((2, D_MODEL, LANES), lambda b, s: (0, 0, 0)),
            pl.BlockSpec((1, LANES), const2),
        ],
        out_specs=_route_out_specs(tm, lambda b, s: b * steps + s),
        out_shape=_route_out_shapes(n),
        scratch_shapes=[
            pltpu.VMEM((1, LANES), F32),
            pltpu.VMEM((POOL_HALO, D_MODEL), F32),
            pltpu.VMEM((tm + POOL_HALO, D_MODEL), F32),
        ],
        compiler_params=pltpu.CompilerParams(dimension_semantics=("arbitrary", "arbitrary"),
                                             vmem_limit_bytes=VMEM_LIMIT),
        name="pool_ln_router",
    )(x2d, pool_w.astype(BF16), pool_b.astype(F32), vec(pool_scale), vec(ln_g), vec(ln_b), rw, rb)


def _swap_vreg_sublane(vs):
    sub = lax.broadcasted_iota(jnp.int32, vs[0].shape, 1)
    vs = list(vs)
    for d in (4, 2, 1):
        keep = (sub & d) == 0
        nxt = list(vs)
        for j in range(SUBLANES):
            if j & d:
                continue
            a, b = vs[j], vs[j + d]
            nxt[j] = jnp.where(keep, a, pltpu.roll(b, d, 1))
            nxt[j + d] = jnp.where(keep, pltpu.roll(a, SUBLANES - d, 1), b)
        vs = nxt
    return vs


def _dispatch_kernel(starts_ref, cnt_ref, ti_ref, rank_ref, xv_ref, pos_ref, xs_hbm,
                     zero_ref, sem, zsem):
    td = xv_ref.shape[0]

    @pl.when(pl.program_id(0) == 0)
    def _():
        zero_ref[...] = jnp.zeros_like(zero_ref)

        def per_expert(e, carry):
            c0 = cnt_ref[e]
            c1 = jnp.bitwise_and(c0 + (MOE_TM - 1), -MOE_TM)
            base = starts_ref[e]

            def start(r, c):
                pltpu.make_async_copy(zero_ref.at[0], xs_hbm.at[base + r], zsem).start()
                return c

            def wait(r, c):
                pltpu.make_async_copy(zero_ref.at[0], xs_hbm.at[base + r], zsem).wait()
                return c

            lax.fori_loop(c0, c1, start, 0)
            lax.fori_loop(c0, c1, wait, 0)
            return carry

        lax.fori_loop(0, N_EXPERTS, per_expert, 0)

        last = N_EXPERTS - 1
        used_rows = starts_ref[last] + jnp.bitwise_and(cnt_ref[last] + (MOE_TM - 1), -MOE_TM)
        first_free = used_rows // MOE_TM
        n_tiles = xs_hbm.shape[0] // MOE_TM

        def tail_copy(tile):
            dst = xs_hbm.at[pl.ds(pl.multiple_of(tile * MOE_TM, MOE_TM), MOE_TM)]
            return pltpu.make_async_copy(zero_ref, dst, zsem)

        def tail_start(tile, c):
            tail_copy(tile).start()
            return c

        def tail_wait(tile, c):
            tail_copy(tile).wait()
            return c

        lax.fori_loop(first_free, n_tiles, tail_start, 0)
        lax.fori_loop(first_free, n_tiles, tail_wait, 0)

    def body(t, carry):
        for k in range(TOP_K):
            a = t * TOP_K + k
            p = starts_ref[ti_ref[a]] + rank_ref[a]
            pos_ref[a] = p
            pltpu.make_async_copy(xv_ref.at[t], xs_hbm.at[p], sem).start(priority=k % 2)
        return carry

    lax.fori_loop(0, td, body, 0)
    pltpu.make_async_copy(xs_hbm.at[pl.ds(0, td * TOP_K)], xs_hbm.at[pl.ds(0, td * TOP_K)], sem).wait()


def _dispatch(starts, counts, ti_flat, rank_flat, x_slab3, rows):
    n = x_slab3.shape[0]
    td = 512
    grid_spec = pltpu.PrefetchScalarGridSpec(
        num_scalar_prefetch=2,
        grid=(n // td,),
        in_specs=[
            pl.BlockSpec((td * TOP_K,), lambda i, st, ct: (i,), memory_space=pltpu.SMEM),
            pl.BlockSpec((td * TOP_K,), lambda i, st, ct: (i,), memory_space=pltpu.SMEM),
            pl.BlockSpec((td, SLAB, LANES), lambda i, st, ct: (i, 0, 0)),
        ],
        out_specs=(
            pl.BlockSpec((td * TOP_K,), lambda i, st, ct: (i,), memory_space=pltpu.SMEM),
            pl.BlockSpec(memory_space=pl.ANY),
        ),
        scratch_shapes=[
            pltpu.VMEM((MOE_TM, SLAB, LANES), F32),
            pltpu.SemaphoreType.DMA(()),
            pltpu.SemaphoreType.DMA(()),
        ],
    )
    return pl.pallas_call(
        _dispatch_kernel,
        grid_spec=grid_spec,
        out_shape=(
            jax.ShapeDtypeStruct((n * TOP_K,), jnp.int32),
            jax.ShapeDtypeStruct((rows, SLAB, LANES), F32),
        ),
        compiler_params=pltpu.CompilerParams(dimension_semantics=("arbitrary",),
                                             has_side_effects=True),
        name="moe_dispatch",
    )(starts, counts, ti_flat, rank_flat, x_slab3)


def _moe_kernel(layer, te_ref, first_ref, slot_ref, nxt_ref, nu_ref,
                x_ref, b1_ref, b2_ref, w1_hbm, w2_hbm, o_ref,
                w1s_ref, w2s_ref, w1p_ref, w2b_ref, sem):
    i = pl.program_id(0)
    tm = MOE_TM
    g = tm // SUBLANES
    used = i < nu_ref[0]
    e = te_ref[i]
    pw = 2 * LANES

    def weight_copies(expert, slot):
        return (pltpu.make_async_copy(w1_hbm.at[layer, expert], w1s_ref.at[slot], sem.at[0, slot]),
                pltpu.make_async_copy(w2_hbm.at[layer, expert], w2s_ref.at[slot], sem.at[1, slot]))

    @pl.when(i == 0)
    def _():
        for c in weight_copies(e, 0):
            c.start()

    @pl.when(used & (first_ref[i] == 1))
    def _():
        slot = slot_ref[i]
        nxt = nxt_ref[i]

        @pl.when(nxt >= 0)
        def _():
            for c in weight_copies(nxt, 1 - slot):
                c.start()

        for c in weight_copies(e, slot):
            c.wait()
        r = lax.broadcasted_iota(jnp.int32, (pw, pw), 0)
        cidx = lax.broadcasted_iota(jnp.int32, (pw, pw), 1)
        src = jnp.where(cidx < LANES, 2 * cidx, 2 * (cidx - LANES) + 1)
        perm = (r == src).astype(BF16)
        for blk in range(2 * D_EXPERT // pw):
            wb = w1s_ref[slot, :, blk * pw:(blk + 1) * pw].astype(BF16)
            w1p_ref[:, blk * pw:(blk + 1) * pw] = _dot(wb, perm).astype(BF16)
        w2b_ref[...] = w2s_ref[slot].astype(BF16)

    @pl.when(used)
    def _():
        slabs = [x_ref[pl.ds(j, g, stride=SUBLANES)] for j in range(SUBLANES)]
        chunks = _swap_vreg_sublane(slabs)
        x = jnp.concatenate([c.reshape(tm, LANES) for c in chunks], axis=1).astype(BF16)
        acc = jnp.zeros((tm, D_MODEL), F32)
        for ch in range(D_EXPERT // pw):
            cols = slice(ch * 2 * pw, (ch + 1) * 2 * pw)
            h = _dot(x, w1p_ref[:, cols]) + b1_ref[0, 0, :, cols]
            acts = []
            for j in range(2):
                hg = jnp.minimum(h[:, j * pw:j * pw + LANES], SWIGLU_LIMIT)
                hl = jnp.clip(h[:, j * pw + LANES:(j + 1) * pw], -SWIGLU_LIMIT, SWIGLU_LIMIT)
                acts.append(hg * _sigmoid(SWIGLU_ALPHA * hg) * (hl + 1.0))
            act = jnp.concatenate(acts, axis=1).astype(BF16)
            acc = acc + _dot(act, w2b_ref[ch * pw:(ch + 1) * pw, :])
        y = acc + b2_ref[0, 0]
        ys = [y[:, s * LANES:(s + 1) * LANES].reshape(g, SUBLANES, LANES) for s in range(SLAB)]
        out_slabs = _swap_vreg_sublane(ys)
        for j in range(SUBLANES):
            o_ref[pl.ds(j, g, stride=SUBLANES)] = out_slabs[j]

    @pl.when(jnp.logical_not(used))
    def _():
        o_ref[...] = jnp.zeros_like(o_ref)


def _moe_mlp(layer, meta, xs3, w1, b1p, w2, b2, n_tiles):
    tm = MOE_TM
    tile_expert, first, slot, nxt, n_used = meta

    def x_map(i, te, fi, sl, nx, nu):
        return (jnp.minimum(i, nu[0] - 1), 0, 0)

    def bias_map(i, te, fi, sl, nx, nu):
        return (layer, te[i], 0, 0)

    grid_spec = pltpu.PrefetchScalarGridSpec(
        num_scalar_prefetch=5,
        grid=(n_tiles,),
        in_specs=[
            pl.BlockSpec((tm, SLAB, LANES), x_map),
            pl.BlockSpec((1, 1, 1, 2 * D_EXPERT), bias_map),
            pl.BlockSpec((1, 1, 1, D_MODEL), bias_map),
            pl.BlockSpec(memory_space=pl.ANY),
            pl.BlockSpec(memory_space=pl.ANY),
        ],
        out_specs=pl.BlockSpec((tm, SLAB, LANES), lambda i, te, fi, sl, nx, nu: (i, 0, 0)),
        scratch_shapes=[
            pltpu.VMEM((2, D_MODEL, 2 * D_EXPERT), F32),
            pltpu.VMEM((2, D_EXPERT, D_MODEL), F32),
            pltpu.VMEM((D_MODEL, 2 * D_EXPERT), BF16),
            pltpu.VMEM((D_EXPERT, D_MODEL), BF16),
            pltpu.SemaphoreType.DMA((2, 2)),
        ],
    )
    return pl.pallas_call(
        functools.partial(_moe_kernel, layer),
        grid_spec=grid_spec,
        out_shape=jax.ShapeDtypeStruct((n_tiles * tm, SLAB, LANES), F32),
        compiler_params=pltpu.CompilerParams(dimension_semantics=("arbitrary",),
                                             vmem_limit_bytes=VMEM_LIMIT),
        name="moe_grouped_mlp",
    )(tile_expert, first, slot, nxt, n_used, xs3, b1p, b2, w1, w2)


def _combine_kernel(pos_ref, pos_next_ref, os_hbm, gate_ref, x1_ref, g_ref, b_ref, o_ref, buf_ref, sem):
    tc = x1_ref.shape[0]
    g = tc // SUBLANES
    i = pl.program_id(0)
    slot = i % 2

    def issue(p_ref, sl):
        def body(t, carry):
            for k in range(TOP_K):
                a = t * TOP_K + k
                pltpu.make_async_copy(os_hbm.at[p_ref[a]], buf_ref.at[sl, a], sem.at[sl]).start(priority=k % 2)
            return carry

        lax.fori_loop(0, tc, body, 0)

    @pl.when(i == 0)
    def _():
        issue(pos_ref, 0)

    @pl.when(i + 1 < pl.num_programs(0))
    def _():
        issue(pos_next_ref, 1 - slot)

    pltpu.make_async_copy(buf_ref.at[slot], buf_ref.at[slot], sem.at[slot]).wait()

    gate = gate_ref[...]
    accs = [jnp.zeros((tc, LANES), F32) for _ in range(SLAB)]
    for k in range(TOP_K):
        slabs = [buf_ref[slot, pl.ds(j * TOP_K + k, g, stride=SUBLANES * TOP_K)] for j in range(SUBLANES)]
        chunks = _swap_vreg_sublane(slabs)
        gk = gate[:, k:k + 1]
        for s in range(SLAB):
            accs[s] = accs[s] + gk * chunks[s].reshape(tc, LANES)
    f = jnp.concatenate(accs, axis=1)
    o_ref[...] = _layer_norm(DEEPNORM_ALPHA * x1_ref[...] + f, g_ref[...], b_ref[...])


def _combine(pos_flat, out_sorted3, gate, x1, ln_g, ln_b):
    n = x1.shape[0]
    tc = 256
    steps = n // tc
    vec = lambda v: v.astype(F32).reshape(1, D_MODEL)
    return pl.pallas_call(
        _combine_kernel,
        grid=(steps,),
        in_specs=[
            pl.BlockSpec((tc * TOP_K,), lambda i: (i,), memory_space=pltpu.SMEM),
            pl.BlockSpec((tc * TOP_K,), lambda i: (jnp.minimum(i + 1, steps - 1),), memory_space=pltpu.SMEM),
            pl.BlockSpec(memory_space=pl.ANY),
            pl.BlockSpec((tc, TOP_K), lambda i: (i, 0)),
            pl.BlockSpec((tc, D_MODEL), lambda i: (i, 0)),
            pl.BlockSpec((1, D_MODEL), lambda i: (0, 0)),
            pl.BlockSpec((1, D_MODEL), lambda i: (0, 0)),
        ],
        out_specs=pl.BlockSpec((tc, D_MODEL), lambda i: (i, 0)),
        out_shape=jax.ShapeDtypeStruct((n, D_MODEL), F32),
        scratch_shapes=[
            pltpu.VMEM((2, tc * TOP_K, SLAB, LANES), F32),
            pltpu.SemaphoreType.DMA((2,)),
        ],
        compiler_params=pltpu.CompilerParams(dimension_semantics=("arbitrary",),
                                             vmem_limit_bytes=VMEM_LIMIT),
        name="moe_combine_ln",
    )(pos_flat, pos_flat, out_sorted3, gate, x1, vec(ln_g), vec(ln_b))


def _group_metadata(counts, n_tiles):
    tm = MOE_TM
    experts = jnp.arange(N_EXPERTS, dtype=jnp.int32)
    cnt = counts[0, :N_EXPERTS]
    tiles_per = (cnt + tm - 1) // tm
    tile_end = jnp.cumsum(tiles_per)
    starts = ((tile_end - tiles_per) * tm).astype(jnp.int32)
    n_used = tile_end[-1:].astype(jnp.int32)
    active = tiles_per > 0
    te_last = jnp.max(jnp.where(active, experts, 0))
    tidx = jnp.arange(n_tiles, dtype=jnp.int32)
    te = jnp.sum((tidx[:, None] >= tile_end[None, :]).astype(jnp.int32), axis=1)
    is_used = tidx < n_used[0]
    tile_expert = jnp.where(is_used, jnp.minimum(te, N_EXPERTS - 1), te_last).astype(jnp.int32)
    prev = jnp.concatenate([tile_expert[:1] - 1, tile_expert[:-1]])
    first = (is_used & (tile_expert != prev)).astype(jnp.int32)
    slot_e = ((jnp.cumsum(active.astype(jnp.int32)) - 1) % 2).astype(jnp.int32)
    later = jnp.where(active, experts, N_EXPERTS)
    suffix_min = lax.cummin(later[::-1])[::-1]
    nxt_e = jnp.concatenate([suffix_min[1:], jnp.full((1,), N_EXPERTS, jnp.int32)])
    nxt_e = jnp.where(nxt_e >= N_EXPERTS, -1, nxt_e).astype(jnp.int32)
    return starts, cnt.astype(jnp.int32), (tile_expert, first, slot_e[tile_expert], nxt_e[tile_expert], n_used)


def _moe_layer(layer, routed, w1, b1p, w2, b2, ln_g, ln_b):
    x1, x1_slab, ti, gate, rank, counts = routed
    n = x1.shape[0]
    n_tiles = n * TOP_K // MOE_TM + N_EXPERTS
    rows = n_tiles * MOE_TM
    starts, cnt, meta = _group_metadata(counts, n_tiles)
    pos, xs = _dispatch(starts, cnt, ti.reshape(-1), rank.reshape(-1),
                        x1_slab.reshape(n, SLAB, LANES), rows)
    out_sorted = _moe_mlp(layer, meta, xs, w1, b1p, w2, b2, n_tiles)
    return _combine(pos, out_sorted, gate, x1, ln_g, ln_b)


def kernel(x, positions, mix_w_in, mix_b_in, dn_conv_w, dn_a_log, dn_dt_bias, dn_norm_w, swa_sinks,
           mix_w_out, mix_b_out, pool_w, pool_b, pool_scale, ln1_g, ln1_b, router_w, router_b,
           moe_w1, moe_b1, moe_w2, moe_b2, ln2_g, ln2_b):
    batch, t, d = x.shape
    n = batch * t
    x2d = x.reshape(n, d)
    pos2d = positions.reshape(n, 1).astype(jnp.int32)
    pw = 2 * LANES
    b1p = moe_b1.astype(F32).reshape(DEPTH, N_EXPERTS, 2 * D_EXPERT // pw, LANES, 2)
    b1p = jnp.swapaxes(b1p, 3, 4).reshape(DEPTH, N_EXPERTS, 1, 2 * D_EXPERT)
    b2r = moe_b2.astype(F32).reshape(DEPTH, N_EXPERTS, 1, D_MODEL)
    for layer in range(DEPTH):
        i = layer // 2
        if layer % 2 == 0:
            q, kv, dq, dk, dv, dz, ab = _inproj(x2d, pos2d, mix_w_in[i], mix_b_in[i])
            a_out = _swa(q, kv, swa_sinks[i], batch, t)
            dn_out = _deltanet(dq, dk, dv, dz, ab, dn_conv_w[i], dn_a_log[i], dn_dt_bias[i],
                               dn_norm_w[i], batch, t)
            routed = _post(a_out, dn_out, x2d, mix_w_out[i], mix_b_out[i], ln1_g[layer], ln1_b[layer],
                           router_w[layer], router_b[layer])
        else:
            routed = _pool(x2d, pool_w[i], pool_b[i], pool_scale[i], ln1_g[layer], ln1_b[layer],
                           router_w[layer], router_b[layer], batch, t)
        x2d = _moe_layer(layer, routed, moe_w1, b1p, moe_w2, b2r, ln2_g[layer], ln2_b[layer])
    return x2d.reshape(batch, t, d)
```

```python
import functools
import math

import numpy as np
import jax
import jax.numpy as jnp
from jax import lax
from jax.experimental import pallas as pl
from jax.experimental.pallas import tpu as pltpu

D_MODEL = 1024
DEPTH = 2
SWA_Q_HEADS = 8
SWA_KV_HEADS = 2
SWA_HEAD_DIM = 64
SWA_BLOCK = 128
ROPE_THETA = 500000.0
ROPE_DIM = SWA_HEAD_DIM // 4
DN_HEADS = 4
DN_HEAD_DIM = 128
DN_CONV = 4
POOL_WINDOWS = (2, 4, 8, 16)
POOL_GROUP = D_MODEL // 4
N_EXPERTS = 32
TOP_K = 4
D_EXPERT = D_MODEL
SWIGLU_LIMIT = 7.0
SWIGLU_ALPHA = 1.702
LN_EPS = 1e-5
RMS_EPS = 1e-6
DEEPNORM_ALPHA = (2 * DEPTH) ** 0.25
SWA_Q_WIDTH = SWA_Q_HEADS * SWA_HEAD_DIM
SWA_KV_WIDTH = SWA_KV_HEADS * SWA_HEAD_DIM
DN_WIDTH = DN_HEADS * DN_HEAD_DIM

LANES = 128
SUBLANES = 8
SLAB = D_MODEL // LANES
IN_PAD_WIDTH = 3328
VMEM_LIMIT = 56 * 1024 * 1024

F32 = jnp.float32
BF16 = jnp.bfloat16
HIGHEST = lax.Precision.HIGHEST

MOE_TM = 256
MOE_TILES_PER_STEP = 2
DN_CHUNK = 128
POOL_HALO = SUBLANES * len(POOL_WINDOWS)


def _sigmoid(x):
    return 0.5 + 0.5 * jnp.tanh(0.5 * x)


def _layer_norm(z, g, b):
    mu = jnp.mean(z, axis=-1, keepdims=True)
    zc = z - mu
    var = jnp.mean(zc * zc, axis=-1, keepdims=True)
    return zc * lax.rsqrt(var + LN_EPS) * g + b


def _dot(a, b):
    return jnp.dot(a, b, preferred_element_type=F32)


def _dot_nt(a, b):
    return lax.dot_general(a, b, (((1,), (1,)), ((), ())), preferred_element_type=F32)


def _route_tail(x1, rw_ref, rb_ref, carry_ref, x1_ref, slab_ref, ti_ref, gate_ref, rank_ref, cnt_ref):
    tm = x1.shape[0]
    x1_ref[...] = x1
    for s in range(SLAB):
        slab_ref[pl.ds(s, tm, stride=SLAB), :] = x1[:, s * LANES:(s + 1) * LANES]

    xh = x1.astype(BF16)
    xl = (x1 - xh.astype(F32)).astype(BF16)
    logits = _dot(xh, rw_ref[0]) + (_dot(xh, rw_ref[1]) + _dot(xl, rw_ref[0])) + rb_ref[...]
    lane = lax.broadcasted_iota(jnp.int32, (tm, LANES), 1)
    lane_f = lane.astype(F32)
    l = jnp.where(lane < N_EXPERTS, logits, -jnp.inf)
    vals, hits = [], []
    ti_out = jnp.zeros((tm, LANES), jnp.int32)
    for k in range(TOP_K):
        m = jnp.max(l, axis=-1, keepdims=True)
        idx = jnp.min(jnp.where(l == m, lane_f, float(LANES)), axis=-1, keepdims=True)
        hit = lane_f == idx
        l = jnp.where(hit, -jnp.inf, l)
        vals.append(m)
        hits.append(hit)
        ti_out = jnp.where(lane == k, idx.astype(jnp.int32), ti_out)
    exps = [jnp.exp(v - vals[0]) for v in vals]
    den = exps[0] + exps[1] + exps[2] + exps[3]
    gate_out = jnp.zeros((tm, LANES), F32)
    for k in range(TOP_K):
        gate_out = jnp.where(lane == k, exps[k] / den, gate_out)

    sel = jnp.zeros((tm, LANES), F32)
    for k in range(TOP_K):
        sel = sel + hits[k].astype(F32)
    ri = lax.broadcasted_iota(jnp.int32, (tm, tm), 0)
    ci = lax.broadcasted_iota(jnp.int32, (tm, tm), 1)
    tri = (ri > ci).astype(BF16)
    carry = carry_ref[...]
    prefix = _dot(tri, sel.astype(BF16)) + carry
    rank_out = jnp.zeros((tm, LANES), jnp.int32)
    for k in range(TOP_K):
        r = jnp.sum(jnp.where(hits[k], prefix, 0.0), axis=-1, keepdims=True)
        rank_out = jnp.where(lane == k, r.astype(jnp.int32), rank_out)
    new_carry = carry + jnp.sum(sel, axis=0, keepdims=True)
    carry_ref[...] = new_carry
    cnt_ref[...] = new_carry.astype(jnp.int32)
    ti_ref[...] = ti_out[:, :TOP_K]
    gate_ref[...] = gate_out[:, :TOP_K]
    rank_ref[...] = rank_out[:, :TOP_K]


def _route_out_shapes(n):
    return (
        jax.ShapeDtypeStruct((n, D_MODEL), F32),
        jax.ShapeDtypeStruct((n * SLAB, LANES), F32),
        jax.ShapeDtypeStruct((n, TOP_K), jnp.int32),
        jax.ShapeDtypeStruct((n, TOP_K), F32),
        jax.ShapeDtypeStruct((n, TOP_K), jnp.int32),
        jax.ShapeDtypeStruct((1, LANES), jnp.int32),
    )


def _route_out_specs(tm, row_map):
    return (
        pl.BlockSpec((tm, D_MODEL), lambda *a: (row_map(*a), 0)),
        pl.BlockSpec((tm * SLAB, LANES), lambda *a: (row_map(*a), 0)),
        pl.BlockSpec((tm, TOP_K), lambda *a: (row_map(*a), 0)),
        pl.BlockSpec((tm, TOP_K), lambda *a: (row_map(*a), 0)),
        pl.BlockSpec((tm, TOP_K), lambda *a: (row_map(*a), 0)),
        pl.BlockSpec((1, LANES), lambda *a: (0, 0)),
    )


def _inproj_kernel(x_ref, pos_ref, invf_ref, w_ref, b_ref,
                   q_ref, kv_ref, dq_ref, dk_ref, dv_ref, dz_ref, ab_ref):
    x = x_ref[...].astype(BF16)
    proj = _dot(x, w_ref[...]) + b_ref[...]
    tm = x.shape[0]
    ang = pos_ref[...].astype(F32) * invf_ref[...]
    cos = jnp.cos(ang)
    sin = jnp.sin(ang)
    d = lax.broadcasted_iota(jnp.int32, (tm, LANES), 1) % SWA_HEAD_DIM
    half = ROPE_DIM // 2
    c_tab = jnp.where(d < ROPE_DIM, cos, 1.0)
    s_lo = jnp.where(d < half, -sin, 0.0)
    s_hi = jnp.where((d >= half) & (d < ROPE_DIM), sin, 0.0)

    def rot(xc):
        return (xc * c_tab + pltpu.roll(xc, LANES - half, 1) * s_lo
                + pltpu.roll(xc, half, 1) * s_hi)

    for c in range(SWA_Q_WIDTH // LANES):
        q_ref[:, c * LANES:(c + 1) * LANES] = rot(proj[:, c * LANES:(c + 1) * LANES])
    o = SWA_Q_WIDTH
    kv_ref[:, :LANES] = rot(proj[:, o:o + LANES])
    kv_ref[:, LANES:] = proj[:, o + LANES:o + 2 * LANES]
    o += 2 * SWA_KV_WIDTH
    dq_ref[...] = proj[:, o:o + DN_WIDTH]
    dk_ref[...] = proj[:, o + DN_WIDTH:o + 2 * DN_WIDTH]
    dv_ref[...] = proj[:, o + 2 * DN_WIDTH:o + 3 * DN_WIDTH]
    o += 3 * DN_WIDTH
    dz_ref[...] = proj[:, o:o + DN_WIDTH]
    o += DN_WIDTH
    ab_ref[...] = proj[:, o:o + LANES]


def _inproj(x2d, pos2d, w_in, b_in):
    n = x2d.shape[0]
    tm = 256
    in_width = w_in.shape[1]
    w = jnp.pad(w_in, ((0, 0), (0, IN_PAD_WIDTH - in_width))).astype(BF16)
    b = jnp.pad(b_in, (0, IN_PAD_WIDTH - in_width)).reshape(1, IN_PAD_WIDTH)
    half = ROPE_DIM // 2
    lane_d = np.arange(LANES) % SWA_HEAD_DIM
    invf = (ROPE_THETA ** (-(lane_d % half).astype(np.float64) / half)).astype(np.float32)
    invf = jnp.asarray(invf.reshape(1, LANES))
    row = lambda i: (i, 0)
    const = lambda i: (0, 0)
    widths = (SWA_Q_WIDTH, 2 * SWA_KV_WIDTH, DN_WIDTH, DN_WIDTH, DN_WIDTH, DN_WIDTH, LANES)
    return pl.pallas_call(
        _inproj_kernel,
        grid=(n // tm,),
        in_specs=[
            pl.BlockSpec((tm, D_MODEL), row),
            pl.BlockSpec((tm, 1), row),
            pl.BlockSpec((1, LANES), const),
            pl.BlockSpec((D_MODEL, IN_PAD_WIDTH), const),
            pl.BlockSpec((1, IN_PAD_WIDTH), const),
        ],
        out_specs=tuple(pl.BlockSpec((tm, wd), row) for wd in widths),
        out_shape=tuple(jax.ShapeDtypeStruct((n, wd), F32) for wd in widths),
        compiler_params=pltpu.CompilerParams(dimension_semantics=("arbitrary",),
                                             vmem_limit_bytes=VMEM_LIMIT),
        name="inproj_rotary",
    )(x2d, pos2d, invf, w, b)


def _swa_kernel(sink_ref, q_ref, kv_ref, o_ref):
    t = q_ref.shape[0]
    blk = SWA_BLOCK
    lane = lax.broadcasted_iota(jnp.int32, (2 * blk, LANES), 1)
    qi = lax.broadcasted_iota(jnp.int32, (blk, 2 * blk), 0)
    kj = lax.broadcasted_iota(jnp.int32, (blk, 2 * blk), 1)
    rel = qi + blk - kj
    in_window = (rel >= 0) & (rel < blk)
    is_cur = kj >= blk
    scale = SWA_HEAD_DIM ** -0.5

    group = SWA_Q_HEADS // SWA_KV_HEADS
    n_chunks = SWA_Q_WIDTH // LANES
    lane_q = lax.broadcasted_iota(jnp.int32, (blk, LANES), 1)
    blocks_per_iter = 2

    def body(it, carry):
        items = []
        for sub in range(blocks_per_iter):
            n = it * blocks_per_iter + sub
            r0 = pl.multiple_of(n * blk, blk)
            p0 = pl.multiple_of(jnp.maximum(n - 1, 0) * blk, blk)
            kvc = kv_ref[pl.ds(r0, blk), :]
            kvp = kv_ref[pl.ds(p0, blk), :]
            kband = jnp.concatenate([kvp[:, :LANES], kvc[:, :LANES]], axis=0)
            vband = jnp.concatenate([kvp[:, LANES:], kvc[:, LANES:]], axis=0)
            valid = in_window & (is_cur | (n > 0))
            k_rhs, v_rhs = [], []
            for h in range(SWA_KV_HEADS):
                in_head = (lane >= h * SWA_HEAD_DIM) & (lane < (h + 1) * SWA_HEAD_DIM)
                km = jnp.where(in_head, kband, 0.0)
                vm = jnp.where(in_head, vband, 0.0)
                kr = pltpu.roll(km, SWA_HEAD_DIM, 1)
                vr = pltpu.roll(vm, SWA_HEAD_DIM, 1)
                lo_k, hi_k = (km, kr) if h == 0 else (kr, km)
                lo_v, hi_v = (vm, vr) if h == 0 else (vr, vm)
                k_rhs.append(jnp.concatenate([lo_k, hi_k], axis=0).astype(BF16))
                v_rhs.append(jnp.concatenate([lo_v, hi_v], axis=0).astype(BF16))
            for c in range(n_chunks):
                items.append((r0, c, valid, k_rhs[(2 * c) // group], v_rhs[(2 * c) // group]))

        scores = [_dot_nt((q_ref[pl.ds(r0, blk), c * LANES:(c + 1) * LANES] * scale).astype(BF16), kr_)
                  for r0, c, _, kr_, _ in items]
        masked = [[jnp.where(valid, s[:, j * 2 * blk:(j + 1) * 2 * blk], -jnp.inf) for j in range(2)]
                  for s, (_, _, valid, _, _) in zip(scores, items)]
        maxes = [[jnp.maximum(jnp.max(s[j], axis=-1, keepdims=True), sink_ref[2 * c + j]) for j in range(2)]
                 for s, (_, c, _, _, _) in zip(masked, items)]
        probs = [[jnp.exp(s[j] - m[j]) for j in range(2)] for s, m in zip(masked, maxes)]
        dens = [[jnp.sum(p[j], axis=-1, keepdims=True) + jnp.exp(sink_ref[2 * c + j] - m[j]) for j in range(2)]
                for p, m, (_, c, _, _, _) in zip(probs, maxes, items)]
        for p, dn, (r0, c, _, _, vr_) in zip(probs, dens, items):
            o = _dot(jnp.concatenate([p[0].astype(BF16), p[1].astype(BF16)], axis=1), vr_)
            inv = jnp.where(lane_q < SWA_HEAD_DIM, 1.0 / dn[0], 1.0 / dn[1])
            o_ref[pl.ds(r0, blk), c * LANES:(c + 1) * LANES] = o * inv
        return carry

    lax.fori_loop(0, t // (blk * blocks_per_iter), body, 0)


def _swa(q, kv, sinks, batch, t):
    return pl.pallas_call(
        _swa_kernel,
        grid=(batch,),
        in_specs=[
            pl.BlockSpec(memory_space=pltpu.SMEM),
            pl.BlockSpec((t, SWA_Q_WIDTH), lambda b: (b, 0)),
            pl.BlockSpec((t, 2 * SWA_KV_WIDTH), lambda b: (b, 0)),
        ],
        out_specs=pl.BlockSpec((t, SWA_Q_WIDTH), lambda b: (b, 0)),
        out_shape=jax.ShapeDtypeStruct((batch * t, SWA_Q_WIDTH), F32),
        compiler_params=pltpu.CompilerParams(dimension_semantics=("arbitrary",),
                                             vmem_limit_bytes=VMEM_LIMIT),
        name="swa_attention",
    )(sinks.astype(F32), q, kv)


def _dn_kernel(convw_ref, hp_ref, normw_ref, sel_ref, ones_ref, dq_ref, dk_ref, dv_ref, dz_ref, ab_ref, o_ref,
               state_ref, halo_ref, pad_ref):
    tc = dq_ref.shape[0]
    c = DN_CHUNK
    halo = SUBLANES

    @pl.when(pl.program_id(1) == 0)
    def _():
        state_ref[...] = jnp.zeros_like(state_ref)
        halo_ref[...] = jnp.zeros_like(halo_ref)

    def conv_silu(x_ref, idx):
        pad_ref[0:halo, :] = halo_ref[idx]
        pad_ref[halo:, :] = x_ref[...]
        halo_ref[idx] = x_ref[tc - halo:tc, :]
        acc = jnp.zeros((tc, DN_WIDTH), F32)
        for j in range(DN_CONV):
            wj = convw_ref[j:j + 1, idx * DN_WIDTH:(idx + 1) * DN_WIDTH]
            acc = acc + wj * pad_ref[pl.ds(halo - DN_CONV + 1 + j, tc), :]
        return acc * _sigmoid(acc)

    q_all = conv_silu(dq_ref, 0)
    k_all = conv_silu(dk_ref, 1)
    v_all = conv_silu(dv_ref, 2)

    ab = ab_ref[...]
    neg_a = hp_ref[0:1, :]
    dt_b = hp_ref[1:2, :]
    sp_arg = ab + dt_b
    softplus = jnp.maximum(sp_arg, 0.0) + jnp.log(1.0 + jnp.exp(-jnp.abs(sp_arg)))
    g_all = neg_a * softplus
    beta_all = _sigmoid(ab)

    d = DN_HEAD_DIM
    ri = lax.broadcasted_iota(jnp.int32, (c, 2 * c), 0)
    ci = lax.broadcasted_iota(jnp.int32, (c, 2 * c), 1) % c
    lower_incl = ri >= ci
    lower_strict = ri > ci
    eye2 = (ri == ci).astype(F32)
    r1 = lax.broadcasted_iota(jnp.int32, (c, c), 0)
    c1 = lax.broadcasted_iota(jnp.int32, (c, c), 1)
    ltri = (r1 >= c1).astype(F32)
    ones_bd = ones_ref[...]

    def lane_bcast(x, first_lane):
        sel = sel_ref[first_lane // 2]
        hi = x.astype(BF16)
        lo = (x - hi.astype(F32)).astype(BF16)
        return _dot(hi, sel) + _dot(lo, sel)

    def blockdiag(x2):
        n = x2.shape[1] // 2
        z = jnp.zeros((x2.shape[0], n), BF16)
        return jnp.concatenate([jnp.concatenate([x2[:, :n], z], axis=1),
                                jnp.concatenate([z, x2[:, n:]], axis=1)], axis=0)

    n_chunks = tc // c
    items = [(ch, pr) for ch in range(n_chunks) for pr in range(DN_HEADS // 2)]

    gcs, gcts = [], []
    for ch in range(n_chunks):
        gc = jnp.dot(ltri, g_all[ch * c:(ch + 1) * c, :], precision=HIGHEST, preferred_element_type=F32)
        gcs.append(gc)
        gcts.append(gc.T)
    pre = []
    for ch, pr in items:
        rows = slice(ch * c, (ch + 1) * c)
        cols2 = slice(pr * 2 * d, (pr + 1) * 2 * d)
        gct = gcts[ch]
        g2 = lane_bcast(gcs[ch], 2 * pr)
        b2 = lane_bcast(beta_all[rows, :], DN_HEADS + 2 * pr)
        grow2 = jnp.concatenate([gct[2 * pr:2 * pr + 1, :], gct[2 * pr + 1:2 * pr + 2, :]], axis=1)
        gl2 = g2[c - 1:c, :]
        q2 = q_all[rows, cols2]
        k2 = k_all[rows, cols2]
        ss = _dot(jnp.concatenate([q2 * q2, k2 * k2], axis=0).astype(BF16), ones_bd)
        qn2 = q2 * lax.rsqrt(ss[:c] + RMS_EPS) * (d ** -0.5)
        kn2 = k2 * lax.rsqrt(ss[c:] + RMS_EPS)
        eg2 = jnp.exp(g2)
        kb2 = kn2 * b2
        decay2 = jnp.exp(jnp.where(lower_incl, g2 - grow2, -jnp.inf))
        lhs = jnp.concatenate([kb2, qn2], axis=0).astype(BF16)
        kn_t2 = jnp.concatenate([kn2[:, :d].T, kn2[:, d:].T], axis=1).astype(BF16)
        kk_qk = _dot(lhs, blockdiag(kn_t2))
        vb2 = (v_all[rows, cols2] * b2).astype(BF16)
        kbeg2 = (kb2 * eg2).astype(BF16)
        zero2 = jnp.zeros((c, 2 * d), BF16)
        rhs_bd = jnp.concatenate(
            [jnp.concatenate([vb2[:, :d], kbeg2[:, :d], zero2], axis=1),
             jnp.concatenate([zero2, vb2[:, d:], kbeg2[:, d:]], axis=1)], axis=0)
        kdec2 = kn2 * jnp.exp(gl2 - g2)
        pre.append(dict(
            a2=jnp.where(lower_strict, kk_qk[:c] * decay2, 0.0),
            qk2=(kk_qk[c:] * decay2).astype(BF16),
            rhs_bd=rhs_bd,
            qdec2=qn2 * eg2,
            kdec_t2=jnp.concatenate([kdec2[:, :d].T, kdec2[:, d:].T], axis=1).astype(BF16),
            glast2=jnp.exp(gl2),
        ))

    tinv = [eye2 - p["a2"] for p in pre]
    apow = [p["a2"] for p in pre]
    for _ in range(int(math.log2(c)) - 1):
        a16 = [a.astype(BF16) for a in apow]
        apow = [_dot(a, blockdiag(a)) for a in a16]
        tinv = [t + _dot(t.astype(BF16), blockdiag(a.astype(BF16))) for t, a in zip(tinv, apow)]

    for p, t in zip(pre, tinv):
        sol16 = _dot(t.astype(BF16), p["rhs_bd"]).astype(BF16)
        sol_bd = jnp.concatenate(
            [jnp.concatenate([sol16[:, :2 * d], jnp.zeros((c, 2 * d), BF16)], axis=1),
             jnp.concatenate([jnp.zeros((c, 2 * d), BF16), sol16[:, 2 * d:]], axis=1)], axis=0)
        ks = _dot(p["kdec_t2"], sol_bd)
        qs = _dot(p["qk2"], sol_bd)
        p["n2"] = jnp.concatenate([ks[:, :d], ks[:, 2 * d:3 * d]], axis=1)
        p["k2"] = jnp.concatenate([ks[:, d:2 * d], ks[:, 3 * d:]], axis=1).astype(BF16)
        p["o2"] = jnp.concatenate([qs[:, :d], qs[:, 2 * d:3 * d]], axis=1)
        p["q2"] = (p["qdec2"] - jnp.concatenate([qs[:, d:2 * d], qs[:, 3 * d:]], axis=1)).astype(BF16)

    normw2 = jnp.concatenate([normw_ref[...], normw_ref[...]], axis=1)
    for (ch, pr), p in zip(items, pre):
        rows = slice(ch * c, (ch + 1) * c)
        s2 = state_ref[pr]
        s_bd = blockdiag(s2.astype(BF16))
        o2 = _dot(p["q2"], s_bd) + p["o2"]
        state_ref[pr] = s2 * p["glast2"] - _dot(p["k2"], s_bd) + p["n2"]
        cols2 = slice(pr * 2 * d, (pr + 1) * 2 * d)
        ms = _dot((o2 * o2).astype(BF16), ones_bd) * (1.0 / d)
        z2 = dz_ref[rows, cols2]
        o_ref[rows, cols2] = o2 * lax.rsqrt(ms + RMS_EPS) * normw2 * (z2 * _sigmoid(z2))


def _deltanet(dq, dk, dv, dz, ab, conv_w, a_log, dt_bias, norm_w, batch, t):
    tc = 512
    hp = jnp.zeros((SUBLANES, LANES), F32)
    hp = hp.at[0, :DN_HEADS].set(-jnp.exp(a_log.astype(F32)))
    hp = hp.at[1, :DN_HEADS].set(dt_bias.astype(F32))
    steps = t // tc
    row = lambda b, s: (b * steps + s, 0)
    const = lambda b, s: (0, 0)
    d = DN_HEAD_DIM
    lane_head = np.arange(2 * d) // d
    sel = np.stack([(np.arange(LANES)[:, None] == 2 * idx + lane_head[None, :]) for idx in range(DN_HEADS)])
    ones_bd = lane_head[:, None] == lane_head[None, :]
    sel = jnp.asarray(sel.astype(np.float32), dtype=BF16)
    ones_bd = jnp.asarray(ones_bd.astype(np.float32), dtype=BF16)
    return pl.pallas_call(
        _dn_kernel,
        grid=(batch, steps),
        in_specs=[
            pl.BlockSpec((DN_CONV, 3 * DN_WIDTH), const),
            pl.BlockSpec((SUBLANES, LANES), const),
            pl.BlockSpec((1, DN_HEAD_DIM), const),
            pl.BlockSpec((DN_HEADS, LANES, 2 * DN_HEAD_DIM), lambda b, s: (0, 0, 0)),
            pl.BlockSpec((2 * DN_HEAD_DIM, 2 * DN_HEAD_DIM), const),
            pl.BlockSpec((tc, DN_WIDTH), row),
            pl.BlockSpec((tc, DN_WIDTH), row),
            pl.BlockSpec((tc, DN_WIDTH), row),
            pl.BlockSpec((tc, DN_WIDTH), row),
            pl.BlockSpec((tc, LANES), row),
        ],
        out_specs=pl.BlockSpec((tc, DN_WIDTH), row),
        out_shape=jax.ShapeDtypeStruct((batch * t, DN_WIDTH), F32),
        scratch_shapes=[
            pltpu.VMEM((DN_HEADS // 2, DN_HEAD_DIM, 2 * DN_HEAD_DIM), F32),
            pltpu.VMEM((3, SUBLANES, DN_WIDTH), F32),
            pltpu.VMEM((tc + SUBLANES, DN_WIDTH), F32),
        ],
        compiler_params=pltpu.CompilerParams(dimension_semantics=("arbitrary", "arbitrary"),
                                             vmem_limit_bytes=VMEM_LIMIT),
        name="gated_deltanet",
    )(conv_w.astype(F32), hp, norm_w.astype(F32).reshape(1, DN_HEAD_DIM), sel, ones_bd, dq, dk, dv, dz, ab)


def _post_kernel(a_ref, dn_ref, x_ref, wo_ref, bo_ref, g_ref, b_ref, rw_ref, rb_ref,
                 x1_ref, slab_ref, ti_ref, gate_ref, rank_ref, cnt_ref, carry_ref):
    @pl.when(pl.program_id(0) == 0)
    def _():
        carry_ref[...] = jnp.zeros_like(carry_ref)

    mix = jnp.concatenate([a_ref[...], dn_ref[...]], axis=1).astype(BF16)
    h = _dot(mix, wo_ref[...]) + bo_ref[...]
    x1 = _layer_norm(DEEPNORM_ALPHA * x_ref[...] + h, g_ref[...], b_ref[...])
    _route_tail(x1, rw_ref, rb_ref, carry_ref, x1_ref, slab_ref, ti_ref, gate_ref, rank_ref, cnt_ref)


def _router_params(router_w, router_b):
    rw = jnp.pad(router_w.astype(F32), ((0, 0), (0, LANES - N_EXPERTS)))
    rw_hi = rw.astype(BF16)
    rw_lo = (rw - rw_hi.astype(F32)).astype(BF16)
    rb = jnp.pad(router_b.astype(F32), (0, LANES - N_EXPERTS)).reshape(1, LANES)
    return jnp.stack([rw_hi, rw_lo]), rb


def _post(a_out, dn_out, x2d, w_out, b_out, ln_g, ln_b, router_w, router_b):
    n = x2d.shape[0]
    tm = 256
    rw, rb = _router_params(router_w, router_b)
    row = lambda i: (i, 0)
    const = lambda i: (0, 0)
    vec = lambda v: v.astype(F32).reshape(1, D_MODEL)
    return pl.pallas_call(
        _post_kernel,
        grid=(n // tm,),
        in_specs=[
            pl.BlockSpec((tm, SWA_Q_WIDTH), row),
            pl.BlockSpec((tm, DN_WIDTH), row),
            pl.BlockSpec((tm, D_MODEL), row),
            pl.BlockSpec((SWA_Q_WIDTH + DN_WIDTH, D_MODEL), const),
            pl.BlockSpec((1, D_MODEL), const),
            pl.BlockSpec((1, D_MODEL), const),
            pl.BlockSpec((1, D_MODEL), const),
            pl.BlockSpec((2, D_MODEL, LANES), lambda i: (0, 0, 0)),
            pl.BlockSpec((1, LANES), const),
        ],
        out_specs=_route_out_specs(tm, lambda i: i),
        out_shape=_route_out_shapes(n),
        scratch_shapes=[pltpu.VMEM((1, LANES), F32)],
        compiler_params=pltpu.CompilerParams(dimension_semantics=("arbitrary",),
                                             vmem_limit_bytes=VMEM_LIMIT),
        name="outproj_ln_router",
    )(a_out, dn_out, x2d, w_out.astype(BF16), vec(b_out), vec(ln_g), vec(ln_b), rw, rb)


def _pool_kernel(x_ref, pw_ref, pb_ref, ps_ref, g_ref, b_ref, rw_ref, rb_ref,
                 x1_ref, slab_ref, ti_ref, gate_ref, rank_ref, cnt_ref,
                 carry_ref, halo_ref, pad_ref):
    tm = x_ref.shape[0]
    halo = POOL_HALO
    b = pl.program_id(0)
    s = pl.program_id(1)

    @pl.when((b == 0) & (s == 0))
    def _():
        carry_ref[...] = jnp.zeros_like(carry_ref)

    @pl.when(s == 0)
    def _():
        halo_ref[...] = jnp.zeros_like(halo_ref)

    x = x_ref[...]
    pad_ref[0:halo, :] = halo_ref[...]
    pad_ref[halo:, :] = x
    halo_ref[...] = x[tm - halo:tm, :]
    tpos = s * tm + lax.broadcasted_iota(jnp.int32, (tm, 1), 0)
    outs = []
    for level, win in enumerate(POOL_WINDOWS, start=1):
        lo = SUBLANES * level
        rows = tm + halo - lo
        cols = slice((level - 1) * POOL_GROUP, D_MODEL)
        shift = win // 2
        pad_ref[pl.ds(lo, rows), cols] = (pad_ref[pl.ds(lo, rows), cols]
                                          + pad_ref[pl.ds(lo - shift, rows), cols])
        gcols = slice((level - 1) * POOL_GROUP, level * POOL_GROUP)
        cnt = jnp.minimum(tpos + 1, win).astype(F32)
        pooled = pad_ref[pl.ds(halo, tm), gcols] / cnt - x[:, gcols]
        y = _dot(pooled.astype(BF16), pw_ref[level - 1]) + pb_ref[level - 1:level, :]
        outs.append(y)
    h = jnp.concatenate(outs, axis=1) * ps_ref[...]
    x1 = _layer_norm(DEEPNORM_ALPHA * x + h, g_ref[...], b_ref[...])
    _route_tail(x1, rw_ref, rb_ref, carry_ref, x1_ref, slab_ref, ti_ref, gate_ref, rank_ref, cnt_ref)


def _pool(x2d, pool_w, pool_b, pool_scale, ln_g, ln_b, router_w, router_b, batch, t):
    n = x2d.shape[0]
    tm = 256
    steps = t // tm
    rw, rb = _router_params(router_w, router_b)
    row = lambda b, s: (b * steps + s, 0)
    const2 = lambda b, s: (0, 0)
    vec = lambda v: v.astype(F32).reshape(1, D_MODEL)
    ng = len(POOL_WINDOWS)
    return pl.pallas_call(
        _pool_kernel,
        grid=(batch, steps),
        in_specs=[
            pl.BlockSpec((tm, D_MODEL), row),
            pl.BlockSpec((ng, POOL_GROUP, POOL_GROUP), lambda b, s: (0, 0, 0)),
            pl.BlockSpec((ng, POOL_GROUP), const2),
            pl.BlockSpec((1, D_MODEL), const2),
            pl.BlockSpec((1, D_MODEL), const2),
            pl.BlockSpec((1, D_MODEL), const2),
            pl.BlockSpec((2, D_MODEL, LANES), lambda b, s: (0, 0, 0)),
            pl.BlockSpec((1, LANES), const2),
        ],
        out_specs=_route_out_specs(tm, lambda b, s: b * steps + s),
        out_shape=_route_out_shapes(n),
        scratch_shapes=[
            pltpu.VMEM((1, LANES), F32),
            pltpu.VMEM((POOL_HALO, D_MODEL), F32),
            pltpu.VMEM((tm + POOL_HALO, D_MODEL), F32),
        ],
        compiler_params=pltpu.CompilerParams(dimension_semantics=("arbitrary", "arbitrary"),
                                             vmem_limit_bytes=VMEM_LIMIT),
        name="pool_ln_router",
    )(x2d, pool_w.astype(BF16), pool_b.astype(F32), vec(pool_scale), vec(ln_g), vec(ln_b), rw, rb)


def _swap_vreg_sublane(vs, order=(4, 2, 1)):
    sub = lax.broadcasted_iota(jnp.int32, vs[0].shape, 1)
    vs = list(vs)
    for d in order:
        keep = (sub & d) == 0
        nxt = list(vs)
        for j in range(SUBLANES):
            if j & d:
                continue
            a, b = vs[j], vs[j + d]
            nxt[j] = jnp.where(keep, a, pltpu.roll(b, d, 1))
            nxt[j + d] = jnp.where(keep, pltpu.roll(a, SUBLANES - d, 1), b)
        vs = nxt
    return vs


def _dispatch_kernel(starts_ref, cnt_ref, pos_ref, xv_ref, xs_hbm, zero_ref, sem, zsem):
    td = xv_ref.shape[0]

    @pl.when(pl.program_id(0) == 0)
    def _():
        zero_ref[...] = jnp.zeros_like(zero_ref)

        def per_expert(e, carry):
            c0 = cnt_ref[e]
            c1 = jnp.bitwise_and(c0 + (MOE_TM - 1), -MOE_TM)
            base = starts_ref[e]

            def start(r, c):
                pltpu.make_async_copy(zero_ref.at[0], xs_hbm.at[base + r], zsem).start()
                return c

            def wait(r, c):
                pltpu.make_async_copy(zero_ref.at[0], xs_hbm.at[base + r], zsem).wait()
                return c

            lax.fori_loop(c0, c1, start, 0)
            lax.fori_loop(c0, c1, wait, 0)
            return carry

        lax.fori_loop(0, N_EXPERTS, per_expert, 0)

        last = N_EXPERTS - 1
        used_rows = starts_ref[last] + jnp.bitwise_and(cnt_ref[last] + (MOE_TM - 1), -MOE_TM)
        first_free = used_rows // MOE_TM
        n_tiles = xs_hbm.shape[0] // MOE_TM

        def tail_copy(tile):
            dst = xs_hbm.at[pl.ds(pl.multiple_of(tile * MOE_TM, MOE_TM), MOE_TM)]
            return pltpu.make_async_copy(zero_ref, dst, zsem)

        def tail_start(tile, c):
            tail_copy(tile).start()
            return c

        def tail_wait(tile, c):
            tail_copy(tile).wait()
            return c

        lax.fori_loop(first_free, n_tiles, tail_start, 0)
        lax.fori_loop(first_free, n_tiles, tail_wait, 0)

    def body(t, carry):
        for k in range(TOP_K):
            pltpu.make_async_copy(xv_ref.at[t], xs_hbm.at[pos_ref[t * TOP_K + k]], sem).start(priority=k % 2)
        return carry

    lax.fori_loop(0, td, body, 0)
    pltpu.make_async_copy(xs_hbm.at[pl.ds(0, td * TOP_K)], xs_hbm.at[pl.ds(0, td * TOP_K)], sem).wait()


def _dispatch(starts, counts, pos_flat, x_slab3, rows):
    n = x_slab3.shape[0]
    td = 512
    grid_spec = pltpu.PrefetchScalarGridSpec(
        num_scalar_prefetch=2,
        grid=(n // td,),
        in_specs=[
            pl.BlockSpec((td * TOP_K,), lambda i, st, ct: (i,), memory_space=pltpu.SMEM),
            pl.BlockSpec((td, SLAB, LANES), lambda i, st, ct: (i, 0, 0)),
        ],
        out_specs=pl.BlockSpec(memory_space=pl.ANY),
        scratch_shapes=[
            pltpu.VMEM((MOE_TM, SLAB, LANES), F32),
            pltpu.SemaphoreType.DMA(()),
            pltpu.SemaphoreType.DMA(()),
        ],
    )
    return pl.pallas_call(
        _dispatch_kernel,
        grid_spec=grid_spec,
        out_shape=jax.ShapeDtypeStruct((rows, SLAB, LANES), F32),
        compiler_params=pltpu.CompilerParams(dimension_semantics=("arbitrary",),
                                             has_side_effects=True),
        name="moe_dispatch",
    )(starts, counts, pos_flat, x_slab3)


def _moe_kernel(layer, te_ref, first_ref, slot_ref, nxt_ref, nu_ref,
                x_ref, b1a_ref, b2a_ref, b1b_ref, b2b_ref, w1_hbm, w2_hbm, o_ref,
                w1s_ref, w2s_ref, w1p_ref, w2b_ref, sem):
    tm = MOE_TM
    g = tm // SUBLANES
    pw = 2 * LANES

    def weight_copies(expert, slot):
        return (pltpu.make_async_copy(w1_hbm.at[layer, expert], w1s_ref.at[slot], sem.at[0, slot]),
                pltpu.make_async_copy(w2_hbm.at[layer, expert], w2s_ref.at[slot], sem.at[1, slot]))

    @pl.when(pl.program_id(0) == 0)
    def _():
        for c in weight_copies(te_ref[0], 0):
            c.start()

    def tile(sub, b1_ref, b2_ref):
        i = pl.program_id(0) * MOE_TILES_PER_STEP + sub
        row0 = sub * tm
        used = i < nu_ref[0]
        e = te_ref[i]

        @pl.when(used & (first_ref[i] == 1))
        def _():
            slot = slot_ref[i]
            nxt = nxt_ref[i]

            @pl.when(nxt >= 0)
            def _():
                for c in weight_copies(nxt, 1 - slot):
                    c.start()

            for c in weight_copies(e, slot):
                c.wait()
            r = lax.broadcasted_iota(jnp.int32, (pw, pw), 0)
            cidx = lax.broadcasted_iota(jnp.int32, (pw, pw), 1)
            src = jnp.where(cidx < LANES, 2 * cidx, 2 * (cidx - LANES) + 1)
            perm = (r == src).astype(BF16)
            for blk in range(2 * D_EXPERT // pw):
                wb = w1s_ref[slot, :, blk * pw:(blk + 1) * pw].astype(BF16)
                w1p_ref[:, blk * pw:(blk + 1) * pw] = _dot(wb, perm).astype(BF16)
            w2b_ref[...] = w2s_ref[slot].astype(BF16)

        @pl.when(used)
        def _():
            slabs = [x_ref[pl.ds(row0 + j, g, stride=SUBLANES)] for j in range(SUBLANES)]
            chunks = _swap_vreg_sublane(slabs)
            x = jnp.concatenate([c.reshape(tm, LANES) for c in chunks], axis=1).astype(BF16)
            acts = []
            for blk in range(2 * D_EXPERT // pw):
                cols = slice(blk * pw, (blk + 1) * pw)
                h = _dot(x, w1p_ref[:, cols]) + b1_ref[0, 0, :, cols]
                hg = jnp.minimum(h[:, :LANES], SWIGLU_LIMIT)
                hl = jnp.clip(h[:, LANES:], -SWIGLU_LIMIT, SWIGLU_LIMIT)
                acts.append((hg * _sigmoid(SWIGLU_ALPHA * hg) * (hl + 1.0)).astype(BF16))
            act = jnp.concatenate(acts, axis=1)
            ys = []
            for nb in range(D_MODEL // pw):
                cols = slice(nb * pw, (nb + 1) * pw)
                y = _dot(act, w2b_ref[:, cols]) + b2_ref[0, 0, :, cols]
                ys += [y[:, s * LANES:(s + 1) * LANES].reshape(g, SUBLANES, LANES) for s in range(pw // LANES)]
            out_slabs = _swap_vreg_sublane(ys, order=(1, 2, 4))
            for j in range(SUBLANES):
                o_ref[pl.ds(row0 + j, g, stride=SUBLANES)] = out_slabs[j]

        @pl.when(jnp.logical_not(used))
        def _():
            o_ref[pl.ds(row0, tm)] = jnp.zeros((tm, SLAB, LANES), F32)

    tile(0, b1a_ref, b2a_ref)
    tile(1, b1b_ref, b2b_ref)


def _moe_mlp(layer, meta, xs3, w1, b1p, w2, b2, n_tiles):
    tps = MOE_TILES_PER_STEP
    tm = MOE_TM * tps
    tile_expert, first, slot, nxt, n_used = meta

    def x_map(i, te, fi, sl, nx, nu):
        return (jnp.minimum(i, (nu[0] - 1) // tps), 0, 0)

    def bias_map(sub):
        return lambda i, te, fi, sl, nx, nu: (layer, te[i * tps + sub], 0, 0)

    grid_spec = pltpu.PrefetchScalarGridSpec(
        num_scalar_prefetch=5,
        grid=(n_tiles // tps,),
        in_specs=[
            pl.BlockSpec((tm, SLAB, LANES), x_map),
            pl.BlockSpec((1, 1, 1, 2 * D_EXPERT), bias_map(0)),
            pl.BlockSpec((1, 1, 1, D_MODEL), bias_map(0)),
            pl.BlockSpec((1, 1, 1, 2 * D_EXPERT), bias_map(1)),
            pl.BlockSpec((1, 1, 1, D_MODEL), bias_map(1)),
            pl.BlockSpec(memory_space=pl.ANY),
            pl.BlockSpec(memory_space=pl.ANY),
        ],
        out_specs=pl.BlockSpec((tm, SLAB, LANES), lambda i, te, fi, sl, nx, nu: (i, 0, 0)),
        scratch_shapes=[
            pltpu.VMEM((2, D_MODEL, 2 * D_EXPERT), F32),
            pltpu.VMEM((2, D_EXPERT, D_MODEL), F32),
            pltpu.VMEM((D_MODEL, 2 * D_EXPERT), BF16),
            pltpu.VMEM((D_EXPERT, D_MODEL), BF16),
            pltpu.SemaphoreType.DMA((2, 2)),
        ],
    )
    return pl.pallas_call(
        functools.partial(_moe_kernel, layer),
        grid_spec=grid_spec,
        out_shape=jax.ShapeDtypeStruct((n_tiles * MOE_TM, SLAB, LANES), F32),
        compiler_params=pltpu.CompilerParams(dimension_semantics=("arbitrary",),
                                             vmem_limit_bytes=VMEM_LIMIT),
        name="moe_grouped_mlp",
    )(tile_expert, first, slot, nxt, n_used, xs3, b1p, b2, b1p, b2, w1, w2)


def _combine_kernel(pos_ref, pos_next_ref, gate_ref, os_hbm, x1_ref, g_ref, b_ref, o_ref,
                    buf_ref, mix_ref, sem):
    tc = x1_ref.shape[0]
    g = tc // SUBLANES
    i = pl.program_id(0)
    slot = i % 2

    def issue(p_ref, sl):
        def body(t, carry):
            for k in range(TOP_K):
                a = t * TOP_K + k
                pltpu.make_async_copy(os_hbm.at[p_ref[a]], buf_ref.at[sl, a], sem.at[sl]).start(priority=k % 2)
            return carry

        lax.fori_loop(0, tc, body, 0)

    @pl.when(i == 0)
    def _():
        issue(pos_ref, 0)

    @pl.when(i + 1 < pl.num_programs(0))
    def _():
        issue(pos_next_ref, 1 - slot)

    pltpu.make_async_copy(buf_ref.at[slot], buf_ref.at[slot], sem.at[slot]).wait()

    def mix_group(grp, carry):
        for j in range(SUBLANES):
            t = grp * SUBLANES + j
            a0 = t * TOP_K
            acc = gate_ref[a0] * buf_ref[slot, a0]
            for k in range(1, TOP_K):
                acc = acc + gate_ref[a0 + k] * buf_ref[slot, a0 + k]
            mix_ref[t] = acc
        return carry

    lax.fori_loop(0, g, mix_group, 0)
    slabs = [mix_ref[pl.ds(j, g, stride=SUBLANES)] for j in range(SUBLANES)]
    chunks = _swap_vreg_sublane(slabs)
    f = jnp.concatenate([c.reshape(tc, LANES) for c in chunks], axis=1)
    o_ref[...] = _layer_norm(DEEPNORM_ALPHA * x1_ref[...] + f, g_ref[...], b_ref[...])


def _combine(pos_flat, out_sorted3, gate, x1, ln_g, ln_b):
    n = x1.shape[0]
    tc = 256
    steps = n // tc
    vec = lambda v: v.astype(F32).reshape(1, D_MODEL)
    return pl.pallas_call(
        _combine_kernel,
        grid=(steps,),
        in_specs=[
            pl.BlockSpec((tc * TOP_K,), lambda i: (i,), memory_space=pltpu.SMEM),
            pl.BlockSpec((tc * TOP_K,), lambda i: (jnp.minimum(i + 1, steps - 1),), memory_space=pltpu.SMEM),
            pl.BlockSpec((tc * TOP_K,), lambda i: (i,), memory_space=pltpu.SMEM),
            pl.BlockSpec(memory_space=pl.ANY),
            pl.BlockSpec((tc, D_MODEL), lambda i: (i, 0)),
            pl.BlockSpec((1, D_MODEL), lambda i: (0, 0)),
            pl.BlockSpec((1, D_MODEL), lambda i: (0, 0)),
        ],
        out_specs=pl.BlockSpec((tc, D_MODEL), lambda i: (i, 0)),
        out_shape=jax.ShapeDtypeStruct((n, D_MODEL), F32),
        scratch_shapes=[
            pltpu.VMEM((2, tc * TOP_K, SLAB, LANES), F32),
            pltpu.VMEM((tc, SLAB, LANES), F32),
            pltpu.SemaphoreType.DMA((2,)),
        ],
        compiler_params=pltpu.CompilerParams(dimension_semantics=("arbitrary",),
                                             vmem_limit_bytes=VMEM_LIMIT),
        name="moe_combine_ln",
    )(pos_flat, pos_flat, gate.reshape(-1), out_sorted3, x1, vec(ln_g), vec(ln_b))


def _group_metadata(counts, n_tiles):
    tm = MOE_TM
    experts = jnp.arange(N_EXPERTS, dtype=jnp.int32)
    cnt = counts[0, :N_EXPERTS]
    tiles_per = (cnt + tm - 1) // tm
    tile_end = jnp.cumsum(tiles_per)
    starts = ((tile_end - tiles_per) * tm).astype(jnp.int32)
    n_used = tile_end[-1:].astype(jnp.int32)
    active = tiles_per > 0
    te_last = jnp.max(jnp.where(active, experts, 0))
    tidx = jnp.arange(n_tiles, dtype=jnp.int32)
    te = jnp.sum((tidx[:, None] >= tile_end[None, :]).astype(jnp.int32), axis=1)
    is_used = tidx < n_used[0]
    tile_expert = jnp.where(is_used, jnp.minimum(te, N_EXPERTS - 1), te_last).astype(jnp.int32)
    prev = jnp.concatenate([tile_expert[:1] - 1, tile_expert[:-1]])
    first = (is_used & (tile_expert != prev)).astype(jnp.int32)
    slot_e = ((jnp.cumsum(active.astype(jnp.int32)) - 1) % 2).astype(jnp.int32)
    later = jnp.where(active, experts, N_EXPERTS)
    suffix_min = lax.cummin(later[::-1])[::-1]
    nxt_e = jnp.concatenate([suffix_min[1:], jnp.full((1,), N_EXPERTS, jnp.int32)])
    nxt_e = jnp.where(nxt_e >= N_EXPERTS, -1, nxt_e).astype(jnp.int32)
    return starts, cnt.astype(jnp.int32), (tile_expert, first, slot_e[tile_expert], nxt_e[tile_expert], n_used)


def _moe_layer(layer, routed, w1, b1p, w2, b2, ln_g, ln_b):
    x1, x1_slab, ti, gate, rank, counts = routed
    n = x1.shape[0]
    n_tiles = n * TOP_K // MOE_TM + N_EXPERTS
    rows = n_tiles * MOE_TM
    starts, cnt, meta = _group_metadata(counts, n_tiles)
    assert n_tiles % MOE_TILES_PER_STEP == 0
    pos = (jnp.take(starts, ti.reshape(-1)) + rank.reshape(-1)).astype(jnp.int32)
    xs = _dispatch(starts, cnt, pos, x1_slab.reshape(n, SLAB, LANES), rows)
    out_sorted = _moe_mlp(layer, meta, xs, w1, b1p, w2, b2, n_tiles)
    return _combine(pos, out_sorted, gate, x1, ln_g, ln_b)


def kernel(x, positions, mix_w_in, mix_b_in, dn_conv_w, dn_a_log, dn_dt_bias, dn_norm_w, swa_sinks,
           mix_w_out, mix_b_out, pool_w, pool_b, pool_scale, ln1_g, ln1_b, router_w, router_b,
           moe_w1, moe_b1, moe_w2, moe_b2, ln2_g, ln2_b):
    batch, t, d = x.shape
    n = batch * t
    x2d = x.reshape(n, d)
    pos2d = positions.reshape(n, 1).astype(jnp.int32)
    pw = 2 * LANES
    b1p = moe_b1.astype(F32).reshape(DEPTH, N_EXPERTS, 2 * D_EXPERT // pw, LANES, 2)
    b1p = jnp.swapaxes(b1p, 3, 4).reshape(DEPTH, N_EXPERTS, 1, 2 * D_EXPERT)
    b2r = moe_b2.astype(F32).reshape(DEPTH, N_EXPERTS, 1, D_MODEL)
    for layer in range(DEPTH):
        i = layer // 2
        if layer % 2 == 0:
            q, kv, dq, dk, dv, dz, ab = _inproj(x2d, pos2d, mix_w_in[i], mix_b_in[i])
            a_out = _swa(q, kv, swa_sinks[i], batch, t)
            dn_out = _deltanet(dq, dk, dv, dz, ab, dn_conv_w[i], dn_a_log[i], dn_dt_bias[i],
                               dn_norm_w[i], batch, t)
            routed = _post(a_out, dn_out, x2d, mix_w_out[i], mix_b_out[i], ln1_g[layer], ln1_b[layer],
                           router_w[layer], router_b[layer])
        else:
            routed = _pool(x2d, pool_w[i], pool_b[i], pool_scale[i], ln1_g[layer], ln1_b[layer],
                           router_w[layer], router_b[layer], batch, t)
        x2d = _moe_layer(layer, routed, moe_w1, b1p, moe_w2, b2r, ln2_g[layer], ln2_b[layer])
    return x2d.reshape(batch, t, d)
```

```python
import functools
import math

import numpy as np
import jax
import jax.numpy as jnp
from jax import lax
from jax.experimental import pallas as pl
from jax.experimental.pallas import tpu as pltpu

D_MODEL = 1024
DEPTH = 2
SWA_Q_HEADS = 8
SWA_KV_HEADS = 2
SWA_HEAD_DIM = 64
SWA_BLOCK = 128
ROPE_THETA = 500000.0
ROPE_DIM = SWA_HEAD_DIM // 4
DN_HEADS = 4
DN_HEAD_DIM = 128
DN_CONV = 4
POOL_WINDOWS = (2, 4, 8, 16)
POOL_GROUP = D_MODEL // 4
N_EXPERTS = 32
TOP_K = 4
D_EXPERT = D_MODEL
SWIGLU_LIMIT = 7.0
SWIGLU_ALPHA = 1.702
LN_EPS = 1e-5
RMS_EPS = 1e-6
DEEPNORM_ALPHA = (2 * DEPTH) ** 0.25
SWA_Q_WIDTH = SWA_Q_HEADS * SWA_HEAD_DIM
SWA_KV_WIDTH = SWA_KV_HEADS * SWA_HEAD_DIM
DN_WIDTH = DN_HEADS * DN_HEAD_DIM

LANES = 128
SUBLANES = 8
SLAB = D_MODEL // LANES
IN_PAD_WIDTH = 3328
VMEM_LIMIT = 56 * 1024 * 1024

F32 = jnp.float32
BF16 = jnp.bfloat16
HIGHEST = lax.Precision.HIGHEST

MOE_TM = 256
MOE_TILES_PER_STEP = 2
DN_CHUNK = 128
POOL_HALO = SUBLANES * len(POOL_WINDOWS)


def _sigmoid(x):
    return 0.5 + 0.5 * jnp.tanh(0.5 * x)


def _layer_norm(z, g, b):
    mu = jnp.mean(z, axis=-1, keepdims=True)
    zc = z - mu
    var = jnp.mean(zc * zc, axis=-1, keepdims=True)
    return zc * lax.rsqrt(var + LN_EPS) * g + b


def _dot(a, b):
    return jnp.dot(a, b, preferred_element_type=F32)


def _dot_nt(a, b):
    return lax.dot_general(a, b, (((1,), (1,)), ((), ())), preferred_element_type=F32)


def _route_tail(x1, rw_ref, rb_ref, carry_ref, x1_ref, slab_ref, ti_ref, gate_ref, rank_ref, cnt_ref):
    tm = x1.shape[0]
    x1_ref[...] = x1
    for s in range(SLAB):
        slab_ref[pl.ds(s, tm, stride=SLAB), :] = x1[:, s * LANES:(s + 1) * LANES]

    xh = x1.astype(BF16)
    xl = (x1 - xh.astype(F32)).astype(BF16)
    logits = _dot(xh, rw_ref[0]) + (_dot(xh, rw_ref[1]) + _dot(xl, rw_ref[0])) + rb_ref[...]
    lane = lax.broadcasted_iota(jnp.int32, (tm, LANES), 1)
    lane_f = lane.astype(F32)
    l = jnp.where(lane < N_EXPERTS, logits, -jnp.inf)
    vals, hits = [], []
    ti_out = jnp.zeros((tm, LANES), jnp.int32)
    for k in range(TOP_K):
        m = jnp.max(l, axis=-1, keepdims=True)
        idx = jnp.min(jnp.where(l == m, lane_f, float(LANES)), axis=-1, keepdims=True)
        hit = lane_f == idx
        l = jnp.where(hit, -jnp.inf, l)
        vals.append(m)
        hits.append(hit)
        ti_out = jnp.where(lane == k, idx.astype(jnp.int32), ti_out)
    exps = [jnp.exp(v - vals[0]) for v in vals]
    den = exps[0] + exps[1] + exps[2] + exps[3]
    gate_out = jnp.zeros((tm, LANES), F32)
    for k in range(TOP_K):
        gate_out = jnp.where(lane == k, exps[k] / den, gate_out)

    sel = jnp.zeros((tm, LANES), F32)
    for k in range(TOP_K):
        sel = sel + hits[k].astype(F32)
    ri = lax.broadcasted_iota(jnp.int32, (tm, tm), 0)
    ci = lax.broadcasted_iota(jnp.int32, (tm, tm), 1)
    tri = (ri > ci).astype(BF16)
    carry = carry_ref[...]
    prefix = _dot(tri, sel.astype(BF16)) + carry
    rank_out = jnp.zeros((tm, LANES), jnp.int32)
    for k in range(TOP_K):
        r = jnp.sum(jnp.where(hits[k], prefix, 0.0), axis=-1, keepdims=True)
        rank_out = jnp.where(lane == k, r.astype(jnp.int32), rank_out)
    new_carry = carry + jnp.sum(sel, axis=0, keepdims=True)
    carry_ref[...] = new_carry
    cnt_ref[...] = new_carry.astype(jnp.int32)
    ti_ref[...] = ti_out[:, :TOP_K]
    gate_ref[...] = gate_out[:, :TOP_K]
    rank_ref[...] = rank_out[:, :TOP_K]


def _route_out_shapes(n):
    return (
        jax.ShapeDtypeStruct((n, D_MODEL), F32),
        jax.ShapeDtypeStruct((n * SLAB, LANES), F32),
        jax.ShapeDtypeStruct((n, TOP_K), jnp.int32),
        jax.ShapeDtypeStruct((n, TOP_K), F32),
        jax.ShapeDtypeStruct((n, TOP_K), jnp.int32),
        jax.ShapeDtypeStruct((1, LANES), jnp.int32),
    )


def _route_out_specs(tm, row_map):
    return (
        pl.BlockSpec((tm, D_MODEL), lambda *a: (row_map(*a), 0)),
        pl.BlockSpec((tm * SLAB, LANES), lambda *a: (row_map(*a), 0)),
        pl.BlockSpec((tm, TOP_K), lambda *a: (row_map(*a), 0)),
        pl.BlockSpec((tm, TOP_K), lambda *a: (row_map(*a), 0)),
        pl.BlockSpec((tm, TOP_K), lambda *a: (row_map(*a), 0)),
        pl.BlockSpec((1, LANES), lambda *a: (0, 0)),
    )


def _inproj_kernel(x_ref, pos_ref, invf_ref, w_ref, b_ref,
                   q_ref, kv_ref, dq_ref, dk_ref, dv_ref, dz_ref, ab_ref):
    x = x_ref[...].astype(BF16)
    proj = _dot(x, w_ref[...]) + b_ref[...]
    tm = x.shape[0]
    ang = pos_ref[...].astype(F32) * invf_ref[...]
    cos = jnp.cos(ang)
    sin = jnp.sin(ang)
    d = lax.broadcasted_iota(jnp.int32, (tm, LANES), 1) % SWA_HEAD_DIM
    half = ROPE_DIM // 2
    c_tab = jnp.where(d < ROPE_DIM, cos, 1.0)
    s_lo = jnp.where(d < half, -sin, 0.0)
    s_hi = jnp.where((d >= half) & (d < ROPE_DIM), sin, 0.0)

    def rot(xc):
        return (xc * c_tab + pltpu.roll(xc, LANES - half, 1) * s_lo
                + pltpu.roll(xc, half, 1) * s_hi)

    for c in range(SWA_Q_WIDTH // LANES):
        q_ref[:, c * LANES:(c + 1) * LANES] = rot(proj[:, c * LANES:(c + 1) * LANES])
    o = SWA_Q_WIDTH
    kv_ref[:, :LANES] = rot(proj[:, o:o + LANES])
    kv_ref[:, LANES:] = proj[:, o + LANES:o + 2 * LANES]
    o += 2 * SWA_KV_WIDTH
    dq_ref[...] = proj[:, o:o + DN_WIDTH]
    dk_ref[...] = proj[:, o + DN_WIDTH:o + 2 * DN_WIDTH]
    dv_ref[...] = proj[:, o + 2 * DN_WIDTH:o + 3 * DN_WIDTH]
    o += 3 * DN_WIDTH
    dz_ref[...] = proj[:, o:o + DN_WIDTH]
    o += DN_WIDTH
    ab_ref[...] = proj[:, o:o + LANES]


def _inproj(x2d, pos2d, w_in, b_in):
    n = x2d.shape[0]
    tm = 256
    in_width = w_in.shape[1]
    w = jnp.pad(w_in, ((0, 0), (0, IN_PAD_WIDTH - in_width))).astype(BF16)
    b = jnp.pad(b_in, (0, IN_PAD_WIDTH - in_width)).reshape(1, IN_PAD_WIDTH)
    half = ROPE_DIM // 2
    lane_d = np.arange(LANES) % SWA_HEAD_DIM
    invf = (ROPE_THETA ** (-(lane_d % half).astype(np.float64) / half)).astype(np.float32)
    invf = jnp.asarray(invf.reshape(1, LANES))
    row = lambda i: (i, 0)
    const = lambda i: (0, 0)
    widths = (SWA_Q_WIDTH, 2 * SWA_KV_WIDTH, DN_WIDTH, DN_WIDTH, DN_WIDTH, DN_WIDTH, LANES)
    return pl.pallas_call(
        _inproj_kernel,
        grid=(n // tm,),
        in_specs=[
            pl.BlockSpec((tm, D_MODEL), row),
            pl.BlockSpec((tm, 1), row),
            pl.BlockSpec((1, LANES), const),
            pl.BlockSpec((D_MODEL, IN_PAD_WIDTH), const),
            pl.BlockSpec((1, IN_PAD_WIDTH), const),
        ],
        out_specs=tuple(pl.BlockSpec((tm, wd), row) for wd in widths),
        out_shape=tuple(jax.ShapeDtypeStruct((n, wd), F32) for wd in widths),
        compiler_params=pltpu.CompilerParams(dimension_semantics=("arbitrary",),
                                             vmem_limit_bytes=VMEM_LIMIT),
        name="inproj_rotary",
    )(x2d, pos2d, invf, w, b)


def _swa_kernel(sink_ref, q_ref, kv_ref, o_ref):
    t = q_ref.shape[0]
    blk = SWA_BLOCK
    lane = lax.broadcasted_iota(jnp.int32, (2 * blk, LANES), 1)
    qi = lax.broadcasted_iota(jnp.int32, (blk, 2 * blk), 0)
    kj = lax.broadcasted_iota(jnp.int32, (blk, 2 * blk), 1)
    rel = qi + blk - kj
    in_window = (rel >= 0) & (rel < blk)
    is_cur = kj >= blk
    scale = SWA_HEAD_DIM ** -0.5

    group = SWA_Q_HEADS // SWA_KV_HEADS
    n_chunks = SWA_Q_WIDTH // LANES
    lane_q = lax.broadcasted_iota(jnp.int32, (blk, LANES), 1)
    blocks_per_iter = 2

    def body(it, carry):
        items = []
        for sub in range(blocks_per_iter):
            n = it * blocks_per_iter + sub
            r0 = pl.multiple_of(n * blk, blk)
            p0 = pl.multiple_of(jnp.maximum(n - 1, 0) * blk, blk)
            kvc = kv_ref[pl.ds(r0, blk), :]
            kvp = kv_ref[pl.ds(p0, blk), :]
            kband = jnp.concatenate([kvp[:, :LANES], kvc[:, :LANES]], axis=0)
            vband = jnp.concatenate([kvp[:, LANES:], kvc[:, LANES:]], axis=0)
            valid = in_window & (is_cur | (n > 0))
            k_rhs, v_rhs = [], []
            for h in range(SWA_KV_HEADS):
                in_head = (lane >= h * SWA_HEAD_DIM) & (lane < (h + 1) * SWA_HEAD_DIM)
                km = jnp.where(in_head, kband, 0.0)
                vm = jnp.where(in_head, vband, 0.0)
                kr = pltpu.roll(km, SWA_HEAD_DIM, 1)
                vr = pltpu.roll(vm, SWA_HEAD_DIM, 1)
                lo_k, hi_k = (km, kr) if h == 0 else (kr, km)
                lo_v, hi_v = (vm, vr) if h == 0 else (vr, vm)
                k_rhs.append(jnp.concatenate([lo_k, hi_k], axis=0).astype(BF16))
                v_rhs.append(jnp.concatenate([lo_v, hi_v], axis=0).astype(BF16))
            for c in range(n_chunks):
                items.append((r0, c, valid, k_rhs[(2 * c) // group], v_rhs[(2 * c) // group]))

        scores = [_dot_nt((q_ref[pl.ds(r0, blk), c * LANES:(c + 1) * LANES] * scale).astype(BF16), kr_)
                  for r0, c, _, kr_, _ in items]
        masked = [[jnp.where(valid, s[:, j * 2 * blk:(j + 1) * 2 * blk], -jnp.inf) for j in range(2)]
                  for s, (_, _, valid, _, _) in zip(scores, items)]
        maxes = [[jnp.maximum(jnp.max(s[j], axis=-1, keepdims=True), sink_ref[2 * c + j]) for j in range(2)]
                 for s, (_, c, _, _, _) in zip(masked, items)]
        probs = [[jnp.exp(s[j] - m[j]) for j in range(2)] for s, m in zip(masked, maxes)]
        dens = [[jnp.sum(p[j], axis=-1, keepdims=True) + jnp.exp(sink_ref[2 * c + j] - m[j]) for j in range(2)]
                for p, m, (_, c, _, _, _) in zip(probs, maxes, items)]
        for p, dn, (r0, c, _, _, vr_) in zip(probs, dens, items):
            o = _dot(jnp.concatenate([p[0].astype(BF16), p[1].astype(BF16)], axis=1), vr_)
            inv = jnp.where(lane_q < SWA_HEAD_DIM, 1.0 / dn[0], 1.0 / dn[1])
            o_ref[pl.ds(r0, blk), c * LANES:(c + 1) * LANES] = o * inv
        return carry

    lax.fori_loop(0, t // (blk * blocks_per_iter), body, 0)


def _swa(q, kv, sinks, batch, t):
    return pl.pallas_call(
        _swa_kernel,
        grid=(batch,),
        in_specs=[
            pl.BlockSpec(memory_space=pltpu.SMEM),
            pl.BlockSpec((t, SWA_Q_WIDTH), lambda b: (b, 0)),
            pl.BlockSpec((t, 2 * SWA_KV_WIDTH), lambda b: (b, 0)),
        ],
        out_specs=pl.BlockSpec((t, SWA_Q_WIDTH), lambda b: (b, 0)),
        out_shape=jax.ShapeDtypeStruct((batch * t, SWA_Q_WIDTH), F32),
        compiler_params=pltpu.CompilerParams(dimension_semantics=("arbitrary",),
                                             vmem_limit_bytes=VMEM_LIMIT),
        name="swa_attention",
    )(sinks.astype(F32), q, kv)


def _dn_kernel(convw_ref, hp_ref, normw_ref, sel_ref, ones_ref, dq_ref, dk_ref, dv_ref, dz_ref, ab_ref, o_ref,
               state_ref, halo_ref, pad_ref):
    tc = dq_ref.shape[0]
    c = DN_CHUNK
    halo = SUBLANES

    @pl.when(pl.program_id(1) == 0)
    def _():
        state_ref[...] = jnp.zeros_like(state_ref)
        halo_ref[...] = jnp.zeros_like(halo_ref)

    def conv_silu(x_ref, idx):
        pad_ref[0:halo, :] = halo_ref[idx]
        pad_ref[halo:, :] = x_ref[...]
        halo_ref[idx] = x_ref[tc - halo:tc, :]
        acc = jnp.zeros((tc, DN_WIDTH), F32)
        for j in range(DN_CONV):
            wj = convw_ref[j:j + 1, idx * DN_WIDTH:(idx + 1) * DN_WIDTH]
            acc = acc + wj * pad_ref[pl.ds(halo - DN_CONV + 1 + j, tc), :]
        return acc * _sigmoid(acc)

    q_all = conv_silu(dq_ref, 0)
    k_all = conv_silu(dk_ref, 1)
    v_all = conv_silu(dv_ref, 2)

    ab = ab_ref[...]
    neg_a = hp_ref[0:1, :]
    dt_b = hp_ref[1:2, :]
    sp_arg = ab + dt_b
    softplus = jnp.maximum(sp_arg, 0.0) + jnp.log(1.0 + jnp.exp(-jnp.abs(sp_arg)))
    g_all = neg_a * softplus
    beta_all = _sigmoid(ab)

    d = DN_HEAD_DIM
    ri = lax.broadcasted_iota(jnp.int32, (c, 2 * c), 0)
    ci = lax.broadcasted_iota(jnp.int32, (c, 2 * c), 1) % c
    lower_incl = ri >= ci
    lower_strict = ri > ci
    eye2 = (ri == ci).astype(F32)
    r1 = lax.broadcasted_iota(jnp.int32, (c, c), 0)
    c1 = lax.broadcasted_iota(jnp.int32, (c, c), 1)
    ltri = (r1 >= c1).astype(F32)
    ones_bd = ones_ref[...]

    def lane_bcast(x, first_lane):
        sel = sel_ref[first_lane // 2]
        hi = x.astype(BF16)
        lo = (x - hi.astype(F32)).astype(BF16)
        return _dot(hi, sel) + _dot(lo, sel)

    def blockdiag(x2):
        n = x2.shape[1] // 2
        z = jnp.zeros((x2.shape[0], n), BF16)
        return jnp.concatenate([jnp.concatenate([x2[:, :n], z], axis=1),
                                jnp.concatenate([z, x2[:, n:]], axis=1)], axis=0)

    n_chunks = tc // c
    items = [(ch, pr) for ch in range(n_chunks) for pr in range(DN_HEADS // 2)]

    gcs, gcts = [], []
    for ch in range(n_chunks):
        gc = jnp.dot(ltri, g_all[ch * c:(ch + 1) * c, :], precision=HIGHEST, preferred_element_type=F32)
        gcs.append(gc)
        gcts.append(gc.T)
    pre = []
    for ch, pr in items:
        rows = slice(ch * c, (ch + 1) * c)
        cols2 = slice(pr * 2 * d, (pr + 1) * 2 * d)
        gct = gcts[ch]
        g2 = lane_bcast(gcs[ch], 2 * pr)
        b2 = lane_bcast(beta_all[rows, :], DN_HEADS + 2 * pr)
        grow2 = jnp.concatenate([gct[2 * pr:2 * pr + 1, :], gct[2 * pr + 1:2 * pr + 2, :]], axis=1)
        gl2 = g2[c - 1:c, :]
        q2 = q_all[rows, cols2]
        k2 = k_all[rows, cols2]
        ss = _dot(jnp.concatenate([q2 * q2, k2 * k2], axis=0).astype(BF16), ones_bd)
        qn2 = q2 * lax.rsqrt(ss[:c] + RMS_EPS) * (d ** -0.5)
        kn2 = k2 * lax.rsqrt(ss[c:] + RMS_EPS)
        eg2 = jnp.exp(g2)
        kb2 = kn2 * b2
        decay2 = jnp.exp(jnp.where(lower_incl, g2 - grow2, -jnp.inf))
        lhs = jnp.concatenate([kb2, qn2], axis=0).astype(BF16)
        kn_t2 = jnp.concatenate([kn2[:, :d].T, kn2[:, d:].T], axis=1).astype(BF16)
        kk_qk = _dot(lhs, blockdiag(kn_t2))
        vb2 = (v_all[rows, cols2] * b2).astype(BF16)
        kbeg2 = (kb2 * eg2).astype(BF16)
        zero2 = jnp.zeros((c, 2 * d), BF16)
        rhs_bd = jnp.concatenate(
            [jnp.concatenate([vb2[:, :d], kbeg2[:, :d], zero2], axis=1),
             jnp.concatenate([zero2, vb2[:, d:], kbeg2[:, d:]], axis=1)], axis=0)
        kdec2 = kn2 * jnp.exp(gl2 - g2)
        pre.append(dict(
            a2=jnp.where(lower_strict, kk_qk[:c] * decay2, 0.0),
            qk2=(kk_qk[c:] * decay2).astype(BF16),
            rhs_bd=rhs_bd,
            qdec2=qn2 * eg2,
            kdec_t2=jnp.concatenate([kdec2[:, :d].T, kdec2[:, d:].T], axis=1).astype(BF16),
            glast2=jnp.exp(gl2),
        ))

    tinv = [eye2 - p["a2"] for p in pre]
    apow = [p["a2"] for p in pre]
    for _ in range(int(math.log2(c)) - 1):
        a16 = [a.astype(BF16) for a in apow]
        apow = [_dot(a, blockdiag(a)) for a in a16]
        tinv = [t + _dot(t.astype(BF16), blockdiag(a.astype(BF16))) for t, a in zip(tinv, apow)]

    for p, t in zip(pre, tinv):
        sol16 = _dot(t.astype(BF16), p["rhs_bd"]).astype(BF16)
        sol_bd = jnp.concatenate(
            [jnp.concatenate([sol16[:, :2 * d], jnp.zeros((c, 2 * d), BF16)], axis=1),
             jnp.concatenate([jnp.zeros((c, 2 * d), BF16), sol16[:, 2 * d:]], axis=1)], axis=0)
        ks = _dot(p["kdec_t2"], sol_bd)
        qs = _dot(p["qk2"], sol_bd)
        p["n2"] = jnp.concatenate([ks[:, :d], ks[:, 2 * d:3 * d]], axis=1)
        p["k2"] = jnp.concatenate([ks[:, d:2 * d], ks[:, 3 * d:]], axis=1).astype(BF16)
        p["o2"] = jnp.concatenate([qs[:, :d], qs[:, 2 * d:3 * d]], axis=1)
        p["q2"] = (p["qdec2"] - jnp.concatenate([qs[:, d:2 * d], qs[:, 3 * d:]], axis=1)).astype(BF16)

    normw2 = jnp.concatenate([normw_ref[...], normw_ref[...]], axis=1)
    for (ch, pr), p in zip(items, pre):
        rows = slice(ch * c, (ch + 1) * c)
        s2 = state_ref[pr]
        s_bd = blockdiag(s2.astype(BF16))
        o2 = _dot(p["q2"], s_bd) + p["o2"]
        state_ref[pr] = s2 * p["glast2"] - _dot(p["k2"], s_bd) + p["n2"]
        cols2 = slice(pr * 2 * d, (pr + 1) * 2 * d)
        ms = _dot((o2 * o2).astype(BF16), ones_bd) * (1.0 / d)
        z2 = dz_ref[rows, cols2]
        o_ref[rows, cols2] = o2 * lax.rsqrt(ms + RMS_EPS) * normw2 * (z2 * _sigmoid(z2))


def _deltanet(dq, dk, dv, dz, ab, conv_w, a_log, dt_bias, norm_w, batch, t):
    tc = 512
    hp = jnp.zeros((SUBLANES, LANES), F32)
    hp = hp.at[0, :DN_HEADS].set(-jnp.exp(a_log.astype(F32)))
    hp = hp.at[1, :DN_HEADS].set(dt_bias.astype(F32))
    steps = t // tc
    row = lambda b, s: (b * steps + s, 0)
    const = lambda b, s: (0, 0)
    d = DN_HEAD_DIM
    lane_head = np.arange(2 * d) // d
    sel = np.stack([(np.arange(LANES)[:, None] == 2 * idx + lane_head[None, :]) for idx in range(DN_HEADS)])
    ones_bd = lane_head[:, None] == lane_head[None, :]
    sel = jnp.asarray(sel.astype(np.float32), dtype=BF16)
    ones_bd = jnp.asarray(ones_bd.astype(np.float32), dtype=BF16)
    return pl.pallas_call(
        _dn_kernel,
        grid=(batch, steps),
        in_specs=[
            pl.BlockSpec((DN_CONV, 3 * DN_WIDTH), const),
            pl.BlockSpec((SUBLANES, LANES), const),
            pl.BlockSpec((1, DN_HEAD_DIM), const),
            pl.BlockSpec((DN_HEADS, LANES, 2 * DN_HEAD_DIM), lambda b, s: (0, 0, 0)),
            pl.BlockSpec((2 * DN_HEAD_DIM, 2 * DN_HEAD_DIM), const),
            pl.BlockSpec((tc, DN_WIDTH), row),
            pl.BlockSpec((tc, DN_WIDTH), row),
            pl.BlockSpec((tc, DN_WIDTH), row),
            pl.BlockSpec((tc, DN_WIDTH), row),
            pl.BlockSpec((tc, LANES), row),
        ],
        out_specs=pl.BlockSpec((tc, DN_WIDTH), row),
        out_shape=jax.ShapeDtypeStruct((batch * t, DN_WIDTH), F32),
        scratch_shapes=[
            pltpu.VMEM((DN_HEADS // 2, DN_HEAD_DIM, 2 * DN_HEAD_DIM), F32),
            pltpu.VMEM((3, SUBLANES, DN_WIDTH), F32),
            pltpu.VMEM((tc + SUBLANES, DN_WIDTH), F32),
        ],
        compiler_params=pltpu.CompilerParams(dimension_semantics=("arbitrary", "arbitrary"),
                                             vmem_limit_bytes=VMEM_LIMIT),
        name="gated_deltanet",
    )(conv_w.astype(F32), hp, norm_w.astype(F32).reshape(1, DN_HEAD_DIM), sel, ones_bd, dq, dk, dv, dz, ab)


def _post_kernel(a_ref, dn_ref, x_ref, wo_ref, bo_ref, g_ref, b_ref, rw_ref, rb_ref,
                 x1_ref, slab_ref, ti_ref, gate_ref, rank_ref, cnt_ref, carry_ref):
    @pl.when(pl.program_id(0) == 0)
    def _():
        carry_ref[...] = jnp.zeros_like(carry_ref)

    mix = jnp.concatenate([a_ref[...], dn_ref[...]], axis=1).astype(BF16)
    h = _dot(mix, wo_ref[...]) + bo_ref[...]
    x1 = _layer_norm(DEEPNORM_ALPHA * x_ref[...] + h, g_ref[...], b_ref[...])
    _route_tail(x1, rw_ref, rb_ref, carry_ref, x1_ref, slab_ref, ti_ref, gate_ref, rank_ref, cnt_ref)


def _router_params(router_w, router_b):
    rw = jnp.pad(router_w.astype(F32), ((0, 0), (0, LANES - N_EXPERTS)))
    rw_hi = rw.astype(BF16)
    rw_lo = (rw - rw_hi.astype(F32)).astype(BF16)
    rb = jnp.pad(router_b.astype(F32), (0, LANES - N_EXPERTS)).reshape(1, LANES)
    return jnp.stack([rw_hi, rw_lo]), rb


def _post(a_out, dn_out, x2d, w_out, b_out, ln_g, ln_b, router_w, router_b):
    n = x2d.shape[0]
    tm = 256
    rw, rb = _router_params(router_w, router_b)
    row = lambda i: (i, 0)
    const = lambda i: (0, 0)
    vec = lambda v: v.astype(F32).reshape(1, D_MODEL)
    return pl.pallas_call(
        _post_kernel,
        grid=(n // tm,),
        in_specs=[
            pl.BlockSpec((tm, SWA_Q_WIDTH), row),
            pl.BlockSpec((tm, DN_WIDTH), row),
            pl.BlockSpec((tm, D_MODEL), row),
            pl.BlockSpec((SWA_Q_WIDTH + DN_WIDTH, D_MODEL), const),
            pl.BlockSpec((1, D_MODEL), const),
            pl.BlockSpec((1, D_MODEL), const),
            pl.BlockSpec((1, D_MODEL), const),
            pl.BlockSpec((2, D_MODEL, LANES), lambda i: (0, 0, 0)),
            pl.BlockSpec((1, LANES), const),
        ],
        out_specs=_route_out_specs(tm, lambda i: i),
        out_shape=_route_out_shapes(n),
        scratch_shapes=[pltpu.VMEM((1, LANES), F32)],
        compiler_params=pltpu.CompilerParams(dimension_semantics=("arbitrary",),
                                             vmem_limit_bytes=VMEM_LIMIT),
        name="outproj_ln_router",
    )(a_out, dn_out, x2d, w_out.astype(BF16), vec(b_out), vec(ln_g), vec(ln_b), rw, rb)


def _pool_kernel(x_ref, pw_ref, pb_ref, ps_ref, g_ref, b_ref, rw_ref, rb_ref,
                 x1_ref, slab_ref, ti_ref, gate_ref, rank_ref, cnt_ref,
                 carry_ref, halo_ref, pad_ref):
    tm = x_ref.shape[0]
    halo = POOL_HALO
    b = pl.program_id(0)
    s = pl.program_id(1)

    @pl.when((b == 0) & (s == 0))
    def _():
        carry_ref[...] = jnp.zeros_like(carry_ref)

    @pl.when(s == 0)
    def _():
        halo_ref[...] = jnp.zeros_like(halo_ref)

    x = x_ref[...]
    pad_ref[0:halo, :] = halo_ref[...]
    pad_ref[halo:, :] = x
    halo_ref[...] = x[tm - halo:tm, :]
    tpos = s * tm + lax.broadcasted_iota(jnp.int32, (tm, 1), 0)
    outs = []
    for level, win in enumerate(POOL_WINDOWS, start=1):
        lo = SUBLANES * level
        rows = tm + halo - lo
        cols = slice((level - 1) * POOL_GROUP, D_MODEL)
        shift = win // 2
        pad_ref[pl.ds(lo, rows), cols] = (pad_ref[pl.ds(lo, rows), cols]
                                          + pad_ref[pl.ds(lo - shift, rows), cols])
        gcols = slice((level - 1) * POOL_GROUP, level * POOL_GROUP)
        cnt = jnp.minimum(tpos + 1, win).astype(F32)
        pooled = pad_ref[pl.ds(halo, tm), gcols] / cnt - x[:, gcols]
        y = _dot(pooled.astype(BF16), pw_ref[level - 1]) + pb_ref[level - 1:level, :]
        outs.append(y)
    h = jnp.concatenate(outs, axis=1) * ps_ref[...]
    x1 = _layer_norm(DEEPNORM_ALPHA * x + h, g_ref[...], b_ref[...])
    _route_tail(x1, rw_ref, rb_ref, carry_ref, x1_ref, slab_ref, ti_ref, gate_ref, rank_ref, cnt_ref)


def _pool(x2d, pool_w, pool_b, pool_scale, ln_g, ln_b, router_w, router_b, batch, t):
    n = x2d.shape[0]
    tm = 256
    steps = t // tm
    rw, rb = _router_params(router_w, router_b)
    row = lambda b, s: (b * steps + s, 0)
    const2 = lambda b, s: (0, 0)
    vec = lambda v: v.astype(F32).reshape(1, D_MODEL)
    ng = len(POOL_WINDOWS)
    return pl.pallas_call(
        _pool_kernel,
        grid=(batch, steps),
        in_specs=[
            pl.BlockSpec((tm, D_MODEL), row),
            pl.BlockSpec((ng, POOL_GROUP, POOL_GROUP), lambda b, s: (0, 0, 0)),
            pl.BlockSpec((ng, POOL_GROUP), const2),
            pl.BlockSpec((1, D_MODEL), const2),
            pl.BlockSpec((1, D_MODEL), const2),
            pl.BlockSpec((1, D_MODEL), const2),
            pl.BlockSpec((2, D_MODEL, LANES), lambda b, s: (0, 0, 0)),
            pl.BlockSpec((1, LANES), const2),
        ],
        out_specs=_route_out_specs(tm, lambda b, s: b * steps + s),
        out_shape=_route_out_shapes(n),
        scratch_shapes=[
            pltpu.VMEM((1, LANES), F32),
            pltpu.VMEM((POOL_HALO, D_MODEL), F32),
            pltpu.VMEM((tm + POOL_HALO, D_MODEL), F32),
        ],
        compiler_params=pltpu.CompilerParams(dimension_semantics=("arbitrary", "arbitrary"),
                                             vmem_limit_bytes=VMEM_LIMIT),
        name="pool_ln_router",
    )(x2d, pool_w.astype(BF16), pool_b.astype(F32), vec(pool_scale), vec(ln_g), vec(ln_b), rw, rb)


def _swap_vreg_sublane(vs, order=(4, 2, 1)):
    sub = lax.broadcasted_iota(jnp.int32, vs[0].shape, 1)
    vs = list(vs)
    for d in order:
        keep = (sub & d) == 0
        nxt = list(vs)
        for j in range(SUBLANES):
            if j & d:
                continue
            a, b = vs[j], vs[j + d]
            nxt[j] = jnp.where(keep, a, pltpu.roll(b, d, 1))
            nxt[j + d] = jnp.where(keep, pltpu.roll(a, SUBLANES - d, 1), b)
        vs = nxt
    return vs


def _inv_kernel(starts_ref, cnt_ref, pos_ref, inv_ref):
    nb = pos_ref.shape[0]
    unroll = 8
    i = pl.program_id(0)

    @pl.when(i == 0)
    def _():
        def fill(lo, hi):
            def body(r, c):
                inv_ref[r] = -1
                return c
            lax.fori_loop(lo, hi, body, 0)

        def per_expert(e, carry):
            c0 = cnt_ref[e]
            fill(starts_ref[e] + c0, starts_ref[e] + jnp.bitwise_and(c0 + (MOE_TM - 1), -MOE_TM))
            return carry

        lax.fori_loop(0, N_EXPERTS, per_expert, 0)
        last = N_EXPERTS - 1
        fill(starts_ref[last] + jnp.bitwise_and(cnt_ref[last] + (MOE_TM - 1), -MOE_TM), inv_ref.shape[0])

    def body(j, carry):
        for u in range(unroll):
            a = j * unroll + u
            inv_ref[pos_ref[a]] = i * nb + a
        return carry

    lax.fori_loop(0, nb // unroll, body, 0)


def _build_inv(starts, counts, pos_flat, rows):
    na = pos_flat.shape[0]
    nb = min(8192, na)
    grid_spec = pltpu.PrefetchScalarGridSpec(
        num_scalar_prefetch=2,
        grid=(na // nb,),
        in_specs=[pl.BlockSpec((nb,), lambda i, st, ct: (i,), memory_space=pltpu.SMEM)],
        out_specs=pl.BlockSpec((rows,), lambda i, st, ct: (0,), memory_space=pltpu.SMEM),
    )
    return pl.pallas_call(
        _inv_kernel,
        grid_spec=grid_spec,
        out_shape=jax.ShapeDtypeStruct((rows,), jnp.int32),
        compiler_params=pltpu.CompilerParams(dimension_semantics=("arbitrary",)),
        name="moe_inverse_perm",
    )(starts, counts, pos_flat)


def _moe_kernel(layer, te_ref, first_ref, slot_ref, nxt_ref, nu_ref,
                src_cur_ref, src_next_ref, dst_a_ref, dst_b_ref, b1a_ref, b2a_ref, b1b_ref, b2b_ref,
                x_hbm, w1_hbm, w2_hbm, y_hbm,
                w1s_ref, w2s_ref, w1p_ref, w2b_ref, xbuf0_ref, xbuf1_ref, ybuf0_ref, ybuf1_ref,
                sem, gsem, ssem):
    tm = MOE_TM
    g = tm // SUBLANES
    pw = 2 * LANES
    step = pl.program_id(0)
    xbuf = (xbuf0_ref, xbuf1_ref)
    ybuf = (ybuf0_ref, ybuf1_ref)

    def weight_copies(expert, slot):
        return (pltpu.make_async_copy(w1_hbm.at[layer, expert], w1s_ref.at[slot], sem.at[0, slot]),
                pltpu.make_async_copy(w2_hbm.at[layer, expert], w2s_ref.at[slot], sem.at[1, slot]))

    def gather_copy(src_ref, off, r, slot):
        return pltpu.make_async_copy(x_hbm.at[src_ref[off + r]], xbuf[slot].at[r], gsem.at[slot])

    def scatter_copy(dst_ref, off, r, slot):
        return pltpu.make_async_copy(ybuf[slot].at[r], y_hbm.at[dst_ref[off + r]], ssem.at[slot])

    def wait_gather(slot):
        pltpu.make_async_copy(xbuf[slot], xbuf[slot], gsem.at[slot]).wait()

    def wait_scatter(slot):
        pltpu.make_async_copy(ybuf[slot], ybuf[slot], ssem.at[slot]).wait()

    @pl.when(step == 0)
    def _():
        for c in weight_copies(te_ref[0], 0):
            c.start()
        ybuf1_ref[...] = jnp.zeros((tm, SLAB, LANES), F32)

        def first_gather(r, c):
            gather_copy(src_cur_ref, 0, r, 0).start()
            return c

        lax.fori_loop(0, tm, first_gather, 0)

    def tile(sub, b1_ref, b2_ref):
        i = step * MOE_TILES_PER_STEP + sub
        other = 1 - sub
        used = i < nu_ref[0]
        e = te_ref[i]
        nxt_src, nxt_off = (src_cur_ref, tm) if sub == 0 else (src_next_ref, 0)
        prv_dst, prv_off = (dst_a_ref, tm) if sub == 0 else (dst_b_ref, 0)

        @pl.when(used & (first_ref[i] == 1))
        def _():
            slot = slot_ref[i]
            nxt = nxt_ref[i]

            @pl.when(nxt >= 0)
            def _():
                for c in weight_copies(nxt, 1 - slot):
                    c.start()

            for c in weight_copies(e, slot):
                c.wait()
            r = lax.broadcasted_iota(jnp.int32, (pw, pw), 0)
            cidx = lax.broadcasted_iota(jnp.int32, (pw, pw), 1)
            src = jnp.where(cidx < LANES, 2 * cidx, 2 * (cidx - LANES) + 1)
            perm = (r == src).astype(BF16)
            for blk in range(2 * D_EXPERT // pw):
                wb = w1s_ref[slot, :, blk * pw:(blk + 1) * pw].astype(BF16)
                w1p_ref[:, blk * pw:(blk + 1) * pw] = _dot(wb, perm).astype(BF16)
            w2b_ref[...] = w2s_ref[slot].astype(BF16)

        @pl.when(used)
        def _():
            wait_gather(sub)

            @pl.when(i >= 1)
            def _():
                wait_scatter(sub)

            for r in range(tm):
                gather_copy(nxt_src, nxt_off, r, other).start(priority=r % 2)
                scatter_copy(prv_dst, prv_off, r, other).start(priority=(r + 1) % 2)
            slabs = [xbuf[sub][pl.ds(j, g, stride=SUBLANES)] for j in range(SUBLANES)]
            chunks = _swap_vreg_sublane(slabs)
            x = jnp.concatenate([c.reshape(tm, LANES) for c in chunks], axis=1).astype(BF16)
            acts = []
            for blk in range(2 * D_EXPERT // pw):
                cols = slice(blk * pw, (blk + 1) * pw)
                h = _dot(x, w1p_ref[:, cols]) + b1_ref[0, 0, :, cols]
                hg = jnp.minimum(h[:, :LANES], SWIGLU_LIMIT)
                hl = jnp.clip(h[:, LANES:], -SWIGLU_LIMIT, SWIGLU_LIMIT)
                acts.append((hg * _sigmoid(SWIGLU_ALPHA * hg) * (hl + 1.0)).astype(BF16))
            act = jnp.concatenate(acts, axis=1)
            ys = []
            for nb in range(D_MODEL // pw):
                cols = slice(nb * pw, (nb + 1) * pw)
                y = _dot(act, w2b_ref[:, cols]) + b2_ref[0, 0, :, cols]
                ys += [y[:, s * LANES:(s + 1) * LANES].reshape(g, SUBLANES, LANES) for s in range(pw // LANES)]
            out_slabs = _swap_vreg_sublane(ys, order=(1, 2, 4))
            for j in range(SUBLANES):
                ybuf[sub][pl.ds(j, g, stride=SUBLANES)] = out_slabs[j]

        @pl.when(i == nu_ref[0])
        def _():
            wait_gather(sub)
            wait_scatter(sub)

            def last_scatter(r, c):
                scatter_copy(prv_dst, prv_off, r, other).start()
                return c

            lax.fori_loop(0, tm, last_scatter, 0)
            wait_scatter(other)

    tile(0, b1a_ref, b2a_ref)
    tile(1, b1b_ref, b2b_ref)


def _moe_mlp(layer, meta, inv, x_slab3, w1, b1p, w2, b2):
    tps = MOE_TILES_PER_STEP
    tm = MOE_TM * tps
    tile_expert, first, slot, nxt, n_used = meta
    n_assign = x_slab3.shape[0] * TOP_K
    steps = inv.shape[0] // tm
    spare = n_assign + jnp.bitwise_and(jnp.arange(inv.shape[0], dtype=jnp.int32), MOE_TM - 1)
    src = jnp.right_shift(jnp.maximum(inv, 0), 2)
    dst = jnp.concatenate([spare[:tm], jnp.where(inv < 0, spare, inv)])

    def bias_map(sub):
        return lambda i, te, fi, sl, nx, nu: (layer, te[i * tps + sub], 0, 0)

    def idx_spec(shift):
        return pl.BlockSpec((tm,), lambda i, te, fi, sl, nx, nu: (jnp.minimum(i + shift, steps - 1 + shift),),
                            memory_space=pltpu.SMEM)

    grid_spec = pltpu.PrefetchScalarGridSpec(
        num_scalar_prefetch=5,
        grid=(steps,),
        in_specs=[
            idx_spec(0), pl.BlockSpec((tm,), lambda i, te, fi, sl, nx, nu: (jnp.minimum(i + 1, steps - 1),),
                                      memory_space=pltpu.SMEM),
            idx_spec(0), idx_spec(1),
            pl.BlockSpec((1, 1, 1, 2 * D_EXPERT), bias_map(0)),
            pl.BlockSpec((1, 1, 1, D_MODEL), bias_map(0)),
            pl.BlockSpec((1, 1, 1, 2 * D_EXPERT), bias_map(1)),
            pl.BlockSpec((1, 1, 1, D_MODEL), bias_map(1)),
            pl.BlockSpec(memory_space=pl.ANY),
            pl.BlockSpec(memory_space=pl.ANY),
            pl.BlockSpec(memory_space=pl.ANY),
        ],
        out_specs=pl.BlockSpec(memory_space=pl.ANY),
        scratch_shapes=[
            pltpu.VMEM((2, D_MODEL, 2 * D_EXPERT), F32),
            pltpu.VMEM((2, D_EXPERT, D_MODEL), F32),
            pltpu.VMEM((D_MODEL, 2 * D_EXPERT), BF16),
            pltpu.VMEM((D_EXPERT, D_MODEL), BF16),
            pltpu.VMEM((MOE_TM, SLAB, LANES), F32),
            pltpu.VMEM((MOE_TM, SLAB, LANES), F32),
            pltpu.VMEM((MOE_TM, SLAB, LANES), F32),
            pltpu.VMEM((MOE_TM, SLAB, LANES), F32),
            pltpu.SemaphoreType.DMA((2, 2)),
            pltpu.SemaphoreType.DMA((2,)),
            pltpu.SemaphoreType.DMA((2,)),
        ],
    )
    return pl.pallas_call(
        functools.partial(_moe_kernel, layer),
        grid_spec=grid_spec,
        out_shape=jax.ShapeDtypeStruct((n_assign + MOE_TM, SLAB, LANES), F32),
        compiler_params=pltpu.CompilerParams(dimension_semantics=("arbitrary",),
                                             vmem_limit_bytes=VMEM_LIMIT, has_side_effects=True),
        name="moe_grouped_mlp",
    )(tile_expert, first, slot, nxt, n_used, src, src, dst, dst, b1p, b2, b1p, b2, x_slab3, w1, w2)


def _combine_kernel(gate_ref, y_ref, x1_ref, g_ref, b_ref, o_ref, mix_ref):
    tc = x1_ref.shape[0]
    g = tc // SUBLANES

    def mix_group(grp, carry):
        for j in range(SUBLANES):
            t = grp * SUBLANES + j
            a0 = t * TOP_K
            acc = gate_ref[a0] * y_ref[a0]
            for k in range(1, TOP_K):
                acc = acc + gate_ref[a0 + k] * y_ref[a0 + k]
            mix_ref[t] = acc
        return carry

    lax.fori_loop(0, g, mix_group, 0)
    slabs = [mix_ref[pl.ds(j, g, stride=SUBLANES)] for j in range(SUBLANES)]
    chunks = _swap_vreg_sublane(slabs)
    f = jnp.concatenate([c.reshape(tc, LANES) for c in chunks], axis=1)
    o_ref[...] = _layer_norm(DEEPNORM_ALPHA * x1_ref[...] + f, g_ref[...], b_ref[...])


def _combine(y_tok, gate, x1, ln_g, ln_b):
    n = x1.shape[0]
    tc = 256
    vec = lambda v: v.astype(F32).reshape(1, D_MODEL)
    return pl.pallas_call(
        _combine_kernel,
        grid=(n // tc,),
        in_specs=[
            pl.BlockSpec((tc * TOP_K,), lambda i: (i,), memory_space=pltpu.SMEM),
            pl.BlockSpec((tc * TOP_K, SLAB, LANES), lambda i: (i, 0, 0)),
            pl.BlockSpec((tc, D_MODEL), lambda i: (i, 0)),
            pl.BlockSpec((1, D_MODEL), lambda i: (0, 0)),
            pl.BlockSpec((1, D_MODEL), lambda i: (0, 0)),
        ],
        out_specs=pl.BlockSpec((tc, D_MODEL), lambda i: (i, 0)),
        out_shape=jax.ShapeDtypeStruct((n, D_MODEL), F32),
        scratch_shapes=[pltpu.VMEM((tc, SLAB, LANES), F32)],
        compiler_params=pltpu.CompilerParams(dimension_semantics=("arbitrary",),
                                             vmem_limit_bytes=VMEM_LIMIT),
        name="moe_combine_ln",
    )(gate.reshape(-1), y_tok, x1, vec(ln_g), vec(ln_b))


def _group_metadata(counts, n_tiles):
    tm = MOE_TM
    experts = jnp.arange(N_EXPERTS, dtype=jnp.int32)
    cnt = counts[0, :N_EXPERTS]
    tiles_per = (cnt + tm - 1) // tm
    tile_end = jnp.cumsum(tiles_per)
    starts = ((tile_end - tiles_per) * tm).astype(jnp.int32)
    n_used = tile_end[-1:].astype(jnp.int32)
    active = tiles_per > 0
    te_last = jnp.max(jnp.where(active, experts, 0))
    tidx = jnp.arange(n_tiles, dtype=jnp.int32)
    te = jnp.sum((tidx[:, None] >= tile_end[None, :]).astype(jnp.int32), axis=1)
    is_used = tidx < n_used[0]
    tile_expert = jnp.where(is_used, jnp.minimum(te, N_EXPERTS - 1), te_last).astype(jnp.int32)
    prev = jnp.concatenate([tile_expert[:1] - 1, tile_expert[:-1]])
    first = (is_used & (tile_expert != prev)).astype(jnp.int32)
    slot_e = ((jnp.cumsum(active.astype(jnp.int32)) - 1) % 2).astype(jnp.int32)
    later = jnp.where(active, experts, N_EXPERTS)
    suffix_min = lax.cummin(later[::-1])[::-1]
    nxt_e = jnp.concatenate([suffix_min[1:], jnp.full((1,), N_EXPERTS, jnp.int32)])
    nxt_e = jnp.where(nxt_e >= N_EXPERTS, -1, nxt_e).astype(jnp.int32)
    return starts, cnt.astype(jnp.int32), (tile_expert, first, slot_e[tile_expert], nxt_e[tile_expert], n_used)


def _moe_layer(layer, routed, w1, b1p, w2, b2, ln_g, ln_b):
    x1, x1_slab, ti, gate, rank, counts = routed
    n = x1.shape[0]
    n_tiles = n * TOP_K // MOE_TM + N_EXPERTS + MOE_TILES_PER_STEP
    assert n_tiles % MOE_TILES_PER_STEP == 0
    starts, cnt, meta = _group_metadata(counts, n_tiles)
    pos = (jnp.take(starts, ti.reshape(-1)) + rank.reshape(-1)).astype(jnp.int32)
    inv = _build_inv(starts, cnt, pos, n_tiles * MOE_TM)
    y_tok = _moe_mlp(layer, meta, inv, x1_slab.reshape(n, SLAB, LANES), w1, b1p, w2, b2)
    return _combine(y_tok, gate, x1, ln_g, ln_b)


def kernel(x, positions, mix_w_in, mix_b_in, dn_conv_w, dn_a_log, dn_dt_bias, dn_norm_w, swa_sinks,
           mix_w_out, mix_b_out, pool_w, pool_b, pool_scale, ln1_g, ln1_b, router_w, router_b,
           moe_w1, moe_b1, moe_w2, moe_b2, ln2_g, ln2_b):
    batch, t, d = x.shape
    n = batch * t
    x2d = x.reshape(n, d)
    pos2d = positions.reshape(n, 1).astype(jnp.int32)
    pw = 2 * LANES
    b1p = moe_b1.astype(F32).reshape(DEPTH, N_EXPERTS, 2 * D_EXPERT // pw, LANES, 2)
    b1p = jnp.swapaxes(b1p, 3, 4).reshape(DEPTH, N_EXPERTS, 1, 2 * D_EXPERT)
    b2r = moe_b2.astype(F32).reshape(DEPTH, N_EXPERTS, 1, D_MODEL)
    for layer in range(DEPTH):
        i = layer // 2
        if layer % 2 == 0:
            q, kv, dq, dk, dv, dz, ab = _inproj(x2d, pos2d, mix_w_in[i], mix_b_in[i])
            a_out = _swa(q, kv, swa_sinks[i], batch, t)
            dn_out = _deltanet(dq, dk, dv, dz, ab, dn_conv_w[i], dn_a_log[i], dn_dt_bias[i],
                               dn_norm_w[i], batch, t)
            routed = _post(a_out, dn_out, x2d, mix_w_out[i], mix_b_out[i], ln1_g[layer], ln1_b[layer],
                           router_w[layer], router_b[layer])
        else:
            routed = _pool(x2d, pool_w[i], pool_b[i], pool_scale[i], ln1_g[layer], ln1_b[layer],
                           router_w[layer], router_b[layer], batch, t)
        x2d = _moe_layer(layer, routed, moe_w1, b1p, moe_w2, b2r, ln2_g[layer], ln2_b[layer])
    return x2d.reshape(batch, t, d)
```

```python
import functools
import math

import numpy as np
import jax
import jax.numpy as jnp
from jax import lax
from jax.experimental import pallas as pl
from jax.experimental.pallas import tpu as pltpu

D_MODEL = 1024
DEPTH = 2
SWA_Q_HEADS = 8
SWA_KV_HEADS = 2
SWA_HEAD_DIM = 64
SWA_BLOCK = 128
ROPE_THETA = 500000.0
ROPE_DIM = SWA_HEAD_DIM // 4
DN_HEADS = 4
DN_HEAD_DIM = 128
DN_CONV = 4
POOL_WINDOWS = (2, 4, 8, 16)
POOL_GROUP = D_MODEL // 4
N_EXPERTS = 32
TOP_K = 4
D_EXPERT = D_MODEL
SWIGLU_LIMIT = 7.0
SWIGLU_ALPHA = 1.702
LN_EPS = 1e-5
RMS_EPS = 1e-6
DEEPNORM_ALPHA = (2 * DEPTH) ** 0.25
SWA_Q_WIDTH = SWA_Q_HEADS * SWA_HEAD_DIM
SWA_KV_WIDTH = SWA_KV_HEADS * SWA_HEAD_DIM
DN_WIDTH = DN_HEADS * DN_HEAD_DIM

LANES = 128
SUBLANES = 8
SLAB = D_MODEL // LANES
IN_PAD_WIDTH = 3328
VMEM_LIMIT = 56 * 1024 * 1024

F32 = jnp.float32
BF16 = jnp.bfloat16
HIGHEST = lax.Precision.HIGHEST

MOE_TM = 256
MOE_TILES_PER_STEP = 2
DN_CHUNK = 128
POOL_HALO = SUBLANES * len(POOL_WINDOWS)


def _sigmoid(x):
    return 0.5 + 0.5 * jnp.tanh(0.5 * x)


def _layer_norm(z, g, b):
    mu = jnp.mean(z, axis=-1, keepdims=True)
    zc = z - mu
    var = jnp.mean(zc * zc, axis=-1, keepdims=True)
    return zc * lax.rsqrt(var + LN_EPS) * g + b


def _dot(a, b):
    return jnp.dot(a, b, preferred_element_type=F32)


def _dot_nt(a, b):
    return lax.dot_general(a, b, (((1,), (1,)), ((), ())), preferred_element_type=F32)


def _route_tail(x1, rw_ref, rb_ref, carry_ref, x1_ref, slab_ref, ti_ref, gate_ref, rank_ref, cnt_ref):
    tm = x1.shape[0]
    x1_ref[...] = x1
    for s in range(SLAB):
        slab_ref[pl.ds(s, tm, stride=SLAB), :] = x1[:, s * LANES:(s + 1) * LANES]

    xh = x1.astype(BF16)
    xl = (x1 - xh.astype(F32)).astype(BF16)
    logits = _dot(xh, rw_ref[0]) + (_dot(xh, rw_ref[1]) + _dot(xl, rw_ref[0])) + rb_ref[...]
    lane = lax.broadcasted_iota(jnp.int32, (tm, LANES), 1)
    lane_f = lane.astype(F32)
    l = jnp.where(lane < N_EXPERTS, logits, -jnp.inf)
    vals, hits = [], []
    ti_out = jnp.zeros((tm, LANES), jnp.int32)
    for k in range(TOP_K):
        m = jnp.max(l, axis=-1, keepdims=True)
        idx = jnp.min(jnp.where(l == m, lane_f, float(LANES)), axis=-1, keepdims=True)
        hit = lane_f == idx
        l = jnp.where(hit, -jnp.inf, l)
        vals.append(m)
        hits.append(hit)
        ti_out = jnp.where(lane == k, idx.astype(jnp.int32), ti_out)
    exps = [jnp.exp(v - vals[0]) for v in vals]
    den = exps[0] + exps[1] + exps[2] + exps[3]
    gate_out = jnp.zeros((tm, LANES), F32)
    for k in range(TOP_K):
        gate_out = jnp.where(lane == k, exps[k] / den, gate_out)

    sel = jnp.zeros((tm, LANES), F32)
    for k in range(TOP_K):
        sel = sel + hits[k].astype(F32)
    ri = lax.broadcasted_iota(jnp.int32, (tm, tm), 0)
    ci = lax.broadcasted_iota(jnp.int32, (tm, tm), 1)
    tri = (ri > ci).astype(BF16)
    carry = carry_ref[...]
    prefix = _dot(tri, sel.astype(BF16)) + carry
    rank_out = jnp.zeros((tm, LANES), jnp.int32)
    for k in range(TOP_K):
        r = jnp.sum(jnp.where(hits[k], prefix, 0.0), axis=-1, keepdims=True)
        rank_out = jnp.where(lane == k, r.astype(jnp.int32), rank_out)
    new_carry = carry + jnp.sum(sel, axis=0, keepdims=True)
    carry_ref[...] = new_carry
    cnt_ref[...] = new_carry.astype(jnp.int32)
    ti_ref[...] = ti_out[:, :TOP_K]
    gate_ref[...] = gate_out[:, :TOP_K]
    rank_ref[...] = rank_out[:, :TOP_K]


def _route_out_shapes(n):
    return (
        jax.ShapeDtypeStruct((n, D_MODEL), F32),
        jax.ShapeDtypeStruct((n * SLAB, LANES), F32),
        jax.ShapeDtypeStruct((n, TOP_K), jnp.int32),
        jax.ShapeDtypeStruct((n, TOP_K), F32),
        jax.ShapeDtypeStruct((n, TOP_K), jnp.int32),
        jax.ShapeDtypeStruct((1, LANES), jnp.int32),
    )


def _route_out_specs(tm, row_map):
    return (
        pl.BlockSpec((tm, D_MODEL), lambda *a: (row_map(*a), 0)),
        pl.BlockSpec((tm * SLAB, LANES), lambda *a: (row_map(*a), 0)),
        pl.BlockSpec((tm, TOP_K), lambda *a: (row_map(*a), 0)),
        pl.BlockSpec((tm, TOP_K), lambda *a: (row_map(*a), 0)),
        pl.BlockSpec((tm, TOP_K), lambda *a: (row_map(*a), 0)),
        pl.BlockSpec((1, LANES), lambda *a: (0, 0)),
    )


def _inproj_kernel(x_ref, pos_ref, invf_ref, w_ref, b_ref,
                   q_ref, kv_ref, dq_ref, dk_ref, dv_ref, dz_ref, ab_ref):
    x = x_ref[...].astype(BF16)
    proj = _dot(x, w_ref[...]) + b_ref[...]
    tm = x.shape[0]
    ang = pos_ref[...].astype(F32) * invf_ref[...]
    cos = jnp.cos(ang)
    sin = jnp.sin(ang)
    d = lax.broadcasted_iota(jnp.int32, (tm, LANES), 1) % SWA_HEAD_DIM
    half = ROPE_DIM // 2
    c_tab = jnp.where(d < ROPE_DIM, cos, 1.0)
    s_lo = jnp.where(d < half, -sin, 0.0)
    s_hi = jnp.where((d >= half) & (d < ROPE_DIM), sin, 0.0)

    def rot(xc):
        return (xc * c_tab + pltpu.roll(xc, LANES - half, 1) * s_lo
                + pltpu.roll(xc, half, 1) * s_hi)

    for c in range(SWA_Q_WIDTH // LANES):
        q_ref[:, c * LANES:(c + 1) * LANES] = rot(proj[:, c * LANES:(c + 1) * LANES])
    o = SWA_Q_WIDTH
    kv_ref[:, :LANES] = rot(proj[:, o:o + LANES])
    kv_ref[:, LANES:] = proj[:, o + LANES:o + 2 * LANES]
    o += 2 * SWA_KV_WIDTH
    dq_ref[...] = proj[:, o:o + DN_WIDTH]
    dk_ref[...] = proj[:, o + DN_WIDTH:o + 2 * DN_WIDTH]
    dv_ref[...] = proj[:, o + 2 * DN_WIDTH:o + 3 * DN_WIDTH]
    o += 3 * DN_WIDTH
    dz_ref[...] = proj[:, o:o + DN_WIDTH]
    o += DN_WIDTH
    ab_ref[...] = proj[:, o:o + LANES]


def _inproj(x2d, pos2d, w_in, b_in):
    n = x2d.shape[0]
    tm = 256
    in_width = w_in.shape[1]
    w = jnp.pad(w_in, ((0, 0), (0, IN_PAD_WIDTH - in_width))).astype(BF16)
    b = jnp.pad(b_in, (0, IN_PAD_WIDTH - in_width)).reshape(1, IN_PAD_WIDTH)
    half = ROPE_DIM // 2
    lane_d = np.arange(LANES) % SWA_HEAD_DIM
    invf = (ROPE_THETA ** (-(lane_d % half).astype(np.float64) / half)).astype(np.float32)
    invf = jnp.asarray(invf.reshape(1, LANES))
    row = lambda i: (i, 0)
    const = lambda i: (0, 0)
    widths = (SWA_Q_WIDTH, 2 * SWA_KV_WIDTH, DN_WIDTH, DN_WIDTH, DN_WIDTH, DN_WIDTH, LANES)
    return pl.pallas_call(
        _inproj_kernel,
        grid=(n // tm,),
        in_specs=[
            pl.BlockSpec((tm, D_MODEL), row),
            pl.BlockSpec((tm, 1), row),
            pl.BlockSpec((1, LANES), const),
            pl.BlockSpec((D_MODEL, IN_PAD_WIDTH), const),
            pl.BlockSpec((1, IN_PAD_WIDTH), const),
        ],
        out_specs=tuple(pl.BlockSpec((tm, wd), row) for wd in widths),
        out_shape=tuple(jax.ShapeDtypeStruct((n, wd), F32) for wd in widths),
        compiler_params=pltpu.CompilerParams(dimension_semantics=("arbitrary",),
                                             vmem_limit_bytes=VMEM_LIMIT),
        name="inproj_rotary",
    )(x2d, pos2d, invf, w, b)


def _swa_kernel(sink_ref, q_ref, kv_ref, o_ref):
    t = q_ref.shape[0]
    blk = SWA_BLOCK
    lane = lax.broadcasted_iota(jnp.int32, (2 * blk, LANES), 1)
    qi = lax.broadcasted_iota(jnp.int32, (blk, 2 * blk), 0)
    kj = lax.broadcasted_iota(jnp.int32, (blk, 2 * blk), 1)
    rel = qi + blk - kj
    in_window = (rel >= 0) & (rel < blk)
    is_cur = kj >= blk
    scale = SWA_HEAD_DIM ** -0.5

    group = SWA_Q_HEADS // SWA_KV_HEADS
    n_chunks = SWA_Q_WIDTH // LANES
    lane_q = lax.broadcasted_iota(jnp.int32, (blk, LANES), 1)
    blocks_per_iter = 2

    def body(it, carry):
        items = []
        for sub in range(blocks_per_iter):
            n = it * blocks_per_iter + sub
            r0 = pl.multiple_of(n * blk, blk)
            p0 = pl.multiple_of(jnp.maximum(n - 1, 0) * blk, blk)
            kvc = kv_ref[pl.ds(r0, blk), :]
            kvp = kv_ref[pl.ds(p0, blk), :]
            kband = jnp.concatenate([kvp[:, :LANES], kvc[:, :LANES]], axis=0)
            vband = jnp.concatenate([kvp[:, LANES:], kvc[:, LANES:]], axis=0)
            valid = in_window & (is_cur | (n > 0))
            k_rhs, v_rhs = [], []
            for h in range(SWA_KV_HEADS):
                in_head = (lane >= h * SWA_HEAD_DIM) & (lane < (h + 1) * SWA_HEAD_DIM)
                km = jnp.where(in_head, kband, 0.0)
                vm = jnp.where(in_head, vband, 0.0)
                kr = pltpu.roll(km, SWA_HEAD_DIM, 1)
                vr = pltpu.roll(vm, SWA_HEAD_DIM, 1)
                lo_k, hi_k = (km, kr) if h == 0 else (kr, km)
                lo_v, hi_v = (vm, vr) if h == 0 else (vr, vm)
                k_rhs.append(jnp.concatenate([lo_k, hi_k], axis=0).astype(BF16))
                v_rhs.append(jnp.concatenate([lo_v, hi_v], axis=0).astype(BF16))
            for c in range(n_chunks):
                items.append((r0, c, valid, k_rhs[(2 * c) // group], v_rhs[(2 * c) // group]))

        scores = [_dot_nt((q_ref[pl.ds(r0, blk), c * LANES:(c + 1) * LANES] * scale).astype(BF16), kr_)
                  for r0, c, _, kr_, _ in items]
        masked = [[jnp.where(valid, s[:, j * 2 * blk:(j + 1) * 2 * blk], -jnp.inf) for j in range(2)]
                  for s, (_, _, valid, _, _) in zip(scores, items)]
        maxes = [[jnp.maximum(jnp.max(s[j], axis=-1, keepdims=True), sink_ref[2 * c + j]) for j in range(2)]
                 for s, (_, c, _, _, _) in zip(masked, items)]
        probs = [[jnp.exp(s[j] - m[j]) for j in range(2)] for s, m in zip(masked, maxes)]
        dens = [[jnp.sum(p[j], axis=-1, keepdims=True) + jnp.exp(sink_ref[2 * c + j] - m[j]) for j in range(2)]
                for p, m, (_, c, _, _, _) in zip(probs, maxes, items)]
        for p, dn, (r0, c, _, _, vr_) in zip(probs, dens, items):
            o = _dot(jnp.concatenate([p[0].astype(BF16), p[1].astype(BF16)], axis=1), vr_)
            inv = jnp.where(lane_q < SWA_HEAD_DIM, 1.0 / dn[0], 1.0 / dn[1])
            o_ref[pl.ds(r0, blk), c * LANES:(c + 1) * LANES] = o * inv
        return carry

    lax.fori_loop(0, t // (blk * blocks_per_iter), body, 0)


def _swa(q, kv, sinks, batch, t):
    return pl.pallas_call(
        _swa_kernel,
        grid=(batch,),
        in_specs=[
            pl.BlockSpec(memory_space=pltpu.SMEM),
            pl.BlockSpec((t, SWA_Q_WIDTH), lambda b: (b, 0)),
            pl.BlockSpec((t, 2 * SWA_KV_WIDTH), lambda b: (b, 0)),
        ],
        out_specs=pl.BlockSpec((t, SWA_Q_WIDTH), lambda b: (b, 0)),
        out_shape=jax.ShapeDtypeStruct((batch * t, SWA_Q_WIDTH), F32),
        compiler_params=pltpu.CompilerParams(dimension_semantics=("arbitrary",),
                                             vmem_limit_bytes=VMEM_LIMIT),
        name="swa_attention",
    )(sinks.astype(F32), q, kv)


def _dn_kernel(convw_ref, hp_ref, normw_ref, sel_ref, ones_ref, dq_ref, dk_ref, dv_ref, dz_ref, ab_ref, o_ref,
               state_ref, halo_ref, pad_ref):
    tc = dq_ref.shape[0]
    c = DN_CHUNK
    halo = SUBLANES

    @pl.when(pl.program_id(1) == 0)
    def _():
        state_ref[...] = jnp.zeros_like(state_ref)
        halo_ref[...] = jnp.zeros_like(halo_ref)

    def conv_silu(x_ref, idx):
        pad_ref[0:halo, :] = halo_ref[idx]
        pad_ref[halo:, :] = x_ref[...]
        halo_ref[idx] = x_ref[tc - halo:tc, :]
        acc = jnp.zeros((tc, DN_WIDTH), F32)
        for j in range(DN_CONV):
            wj = convw_ref[j:j + 1, idx * DN_WIDTH:(idx + 1) * DN_WIDTH]
            acc = acc + wj * pad_ref[pl.ds(halo - DN_CONV + 1 + j, tc), :]
        return acc * _sigmoid(acc)

    q_all = conv_silu(dq_ref, 0)
    k_all = conv_silu(dk_ref, 1)
    v_all = conv_silu(dv_ref, 2)

    ab = ab_ref[...]
    neg_a = hp_ref[0:1, :]
    dt_b = hp_ref[1:2, :]
    sp_arg = ab + dt_b
    softplus = jnp.maximum(sp_arg, 0.0) + jnp.log(1.0 + jnp.exp(-jnp.abs(sp_arg)))
    g_all = neg_a * softplus
    beta_all = _sigmoid(ab)

    d = DN_HEAD_DIM
    ri = lax.broadcasted_iota(jnp.int32, (c, 2 * c), 0)
    ci = lax.broadcasted_iota(jnp.int32, (c, 2 * c), 1) % c
    lower_incl = ri >= ci
    lower_strict = ri > ci
    eye2 = (ri == ci).astype(F32)
    r1 = lax.broadcasted_iota(jnp.int32, (c, c), 0)
    c1 = lax.broadcasted_iota(jnp.int32, (c, c), 1)
    ltri = (r1 >= c1).astype(F32)
    ones_bd = ones_ref[...]

    def lane_bcast(x, first_lane):
        sel = sel_ref[first_lane // 2]
        hi = x.astype(BF16)
        lo = (x - hi.astype(F32)).astype(BF16)
        return _dot(hi, sel) + _dot(lo, sel)

    def blockdiag(x2):
        n = x2.shape[1] // 2
        z = jnp.zeros((x2.shape[0], n), BF16)
        return jnp.concatenate([jnp.concatenate([x2[:, :n], z], axis=1),
                                jnp.concatenate([z, x2[:, n:]], axis=1)], axis=0)

    n_chunks = tc // c
    items = [(ch, pr) for ch in range(n_chunks) for pr in range(DN_HEADS // 2)]

    gcs, gcts = [], []
    for ch in range(n_chunks):
        gc = jnp.dot(ltri, g_all[ch * c:(ch + 1) * c, :], precision=HIGHEST, preferred_element_type=F32)
        gcs.append(gc)
        gcts.append(gc.T)
    pre = []
    for ch, pr in items:
        rows = slice(ch * c, (ch + 1) * c)
        cols2 = slice(pr * 2 * d, (pr + 1) * 2 * d)
        gct = gcts[ch]
        g2 = lane_bcast(gcs[ch], 2 * pr)
        b2 = lane_bcast(beta_all[rows, :], DN_HEADS + 2 * pr)
        grow2 = jnp.concatenate([gct[2 * pr:2 * pr + 1, :], gct[2 * pr + 1:2 * pr + 2, :]], axis=1)
        gl2 = g2[c - 1:c, :]
        q2 = q_all[rows, cols2]
        k2 = k_all[rows, cols2]
        ss = _dot(jnp.concatenate([q2 * q2, k2 * k2], axis=0).astype(BF16), ones_bd)
        qn2 = q2 * lax.rsqrt(ss[:c] + RMS_EPS) * (d ** -0.5)
        kn2 = k2 * lax.rsqrt(ss[c:] + RMS_EPS)
        eg2 = jnp.exp(g2)
        kb2 = kn2 * b2
        decay2 = jnp.exp(jnp.where(lower_incl, g2 - grow2, -jnp.inf))
        lhs = jnp.concatenate([kb2, qn2], axis=0).astype(BF16)
        kn_t2 = jnp.concatenate([kn2[:, :d].T, kn2[:, d:].T], axis=1).astype(BF16)
        kk_qk = _dot(lhs, blockdiag(kn_t2))
        vb2 = (v_all[rows, cols2] * b2).astype(BF16)
        kbeg2 = (kb2 * eg2).astype(BF16)
        zero2 = jnp.zeros((c, 2 * d), BF16)
        rhs_bd = jnp.concatenate(
            [jnp.concatenate([vb2[:, :d], kbeg2[:, :d], zero2], axis=1),
             jnp.concatenate([zero2, vb2[:, d:], kbeg2[:, d:]], axis=1)], axis=0)
        kdec2 = kn2 * jnp.exp(gl2 - g2)
        pre.append(dict(
            a2=jnp.where(lower_strict, kk_qk[:c] * decay2, 0.0),
            qk2=(kk_qk[c:] * decay2).astype(BF16),
            rhs_bd=rhs_bd,
            qdec2=qn2 * eg2,
            kdec_t2=jnp.concatenate([kdec2[:, :d].T, kdec2[:, d:].T], axis=1).astype(BF16),
            glast2=jnp.exp(gl2),
        ))

    tinv = [eye2 - p["a2"] for p in pre]
    apow = [p["a2"] for p in pre]
    for _ in range(int(math.log2(c)) - 1):
        a16 = [a.astype(BF16) for a in apow]
        apow = [_dot(a, blockdiag(a)) for a in a16]
        tinv = [t + _dot(t.astype(BF16), blockdiag(a.astype(BF16))) for t, a in zip(tinv, apow)]

    for p, t in zip(pre, tinv):
        sol16 = _dot(t.astype(BF16), p["rhs_bd"]).astype(BF16)
        sol_bd = jnp.concatenate(
            [jnp.concatenate([sol16[:, :2 * d], jnp.zeros((c, 2 * d), BF16)], axis=1),
             jnp.concatenate([jnp.zeros((c, 2 * d), BF16), sol16[:, 2 * d:]], axis=1)], axis=0)
        ks = _dot(p["kdec_t2"], sol_bd)
        qs = _dot(p["qk2"], sol_bd)
        p["n2"] = jnp.concatenate([ks[:, :d], ks[:, 2 * d:3 * d]], axis=1)
        p["k2"] = jnp.concatenate([ks[:, d:2 * d], ks[:, 3 * d:]], axis=1).astype(BF16)
        p["o2"] = jnp.concatenate([qs[:, :d], qs[:, 2 * d:3 * d]], axis=1)
        p["q2"] = (p["qdec2"] - jnp.concatenate([qs[:, d:2 * d], qs[:, 3 * d:]], axis=1)).astype(BF16)

    normw2 = jnp.concatenate([normw_ref[...], normw_ref[...]], axis=1)
    for (ch, pr), p in zip(items, pre):
        rows = slice(ch * c, (ch + 1) * c)
        s2 = state_ref[pr]
        s_bd = blockdiag(s2.astype(BF16))
        o2 = _dot(p["q2"], s_bd) + p["o2"]
        state_ref[pr] = s2 * p["glast2"] - _dot(p["k2"], s_bd) + p["n2"]
        cols2 = slice(pr * 2 * d, (pr + 1) * 2 * d)
        ms = _dot((o2 * o2).astype(BF16), ones_bd) * (1.0 / d)
        z2 = dz_ref[rows, cols2]
        o_ref[rows, cols2] = o2 * lax.rsqrt(ms + RMS_EPS) * normw2 * (z2 * _sigmoid(z2))


def _deltanet(dq, dk, dv, dz, ab, conv_w, a_log, dt_bias, norm_w, batch, t):
    tc = 512
    hp = jnp.zeros((SUBLANES, LANES), F32)
    hp = hp.at[0, :DN_HEADS].set(-jnp.exp(a_log.astype(F32)))
    hp = hp.at[1, :DN_HEADS].set(dt_bias.astype(F32))
    steps = t // tc
    row = lambda b, s: (b * steps + s, 0)
    const = lambda b, s: (0, 0)
    d = DN_HEAD_DIM
    lane_head = np.arange(2 * d) // d
    sel = np.stack([(np.arange(LANES)[:, None] == 2 * idx + lane_head[None, :]) for idx in range(DN_HEADS)])
    ones_bd = lane_head[:, None] == lane_head[None, :]
    sel = jnp.asarray(sel.astype(np.float32), dtype=BF16)
    ones_bd = jnp.asarray(ones_bd.astype(np.float32), dtype=BF16)
    return pl.pallas_call(
        _dn_kernel,
        grid=(batch, steps),
        in_specs=[
            pl.BlockSpec((DN_CONV, 3 * DN_WIDTH), const),
            pl.BlockSpec((SUBLANES, LANES), const),
            pl.BlockSpec((1, DN_HEAD_DIM), const),
            pl.BlockSpec((DN_HEADS, LANES, 2 * DN_HEAD_DIM), lambda b, s: (0, 0, 0)),
            pl.BlockSpec((2 * DN_HEAD_DIM, 2 * DN_HEAD_DIM), const),
            pl.BlockSpec((tc, DN_WIDTH), row),
            pl.BlockSpec((tc, DN_WIDTH), row),
            pl.BlockSpec((tc, DN_WIDTH), row),
            pl.BlockSpec((tc, DN_WIDTH), row),
            pl.BlockSpec((tc, LANES), row),
        ],
        out_specs=pl.BlockSpec((tc, DN_WIDTH), row),
        out_shape=jax.ShapeDtypeStruct((batch * t, DN_WIDTH), F32),
        scratch_shapes=[
            pltpu.VMEM((DN_HEADS // 2, DN_HEAD_DIM, 2 * DN_HEAD_DIM), F32),
            pltpu.VMEM((3, SUBLANES, DN_WIDTH), F32),
            pltpu.VMEM((tc + SUBLANES, DN_WIDTH), F32),
        ],
        compiler_params=pltpu.CompilerParams(dimension_semantics=("arbitrary", "arbitrary"),
                                             vmem_limit_bytes=VMEM_LIMIT),
        name="gated_deltanet",
    )(conv_w.astype(F32), hp, norm_w.astype(F32).reshape(1, DN_HEAD_DIM), sel, ones_bd, dq, dk, dv, dz, ab)


def _post_kernel(a_ref, dn_ref, x_ref, wo_ref, bo_ref, g_ref, b_ref, rw_ref, rb_ref,
                 x1_ref, slab_ref, ti_ref, gate_ref, rank_ref, cnt_ref, carry_ref):
    @pl.when(pl.program_id(0) == 0)
    def _():
        carry_ref[...] = jnp.zeros_like(carry_ref)

    mix = jnp.concatenate([a_ref[...], dn_ref[...]], axis=1).astype(BF16)
    h = _dot(mix, wo_ref[...]) + bo_ref[...]
    x1 = _layer_norm(DEEPNORM_ALPHA * x_ref[...] + h, g_ref[...], b_ref[...])
    _route_tail(x1, rw_ref, rb_ref, carry_ref, x1_ref, slab_ref, ti_ref, gate_ref, rank_ref, cnt_ref)


def _router_params(router_w, router_b):
    rw = jnp.pad(router_w.astype(F32), ((0, 0), (0, LANES - N_EXPERTS)))
    rw_hi = rw.astype(BF16)
    rw_lo = (rw - rw_hi.astype(F32)).astype(BF16)
    rb = jnp.pad(router_b.astype(F32), (0, LANES - N_EXPERTS)).reshape(1, LANES)
    return jnp.stack([rw_hi, rw_lo]), rb


def _post(a_out, dn_out, x2d, w_out, b_out, ln_g, ln_b, router_w, router_b):
    n = x2d.shape[0]
    tm = 256
    rw, rb = _router_params(router_w, router_b)
    row = lambda i: (i, 0)
    const = lambda i: (0, 0)
    vec = lambda v: v.astype(F32).reshape(1, D_MODEL)
    return pl.pallas_call(
        _post_kernel,
        grid=(n // tm,),
        in_specs=[
            pl.BlockSpec((tm, SWA_Q_WIDTH), row),
            pl.BlockSpec((tm, DN_WIDTH), row),
            pl.BlockSpec((tm, D_MODEL), row),
            pl.BlockSpec((SWA_Q_WIDTH + DN_WIDTH, D_MODEL), const),
            pl.BlockSpec((1, D_MODEL), const),
            pl.BlockSpec((1, D_MODEL), const),
            pl.BlockSpec((1, D_MODEL), const),
            pl.BlockSpec((2, D_MODEL, LANES), lambda i: (0, 0, 0)),
            pl.BlockSpec((1, LANES), const),
        ],
        out_specs=_route_out_specs(tm, lambda i: i),
        out_shape=_route_out_shapes(n),
        scratch_shapes=[pltpu.VMEM((1, LANES), F32)],
        compiler_params=pltpu.CompilerParams(dimension_semantics=("arbitrary",),
                                             vmem_limit_bytes=VMEM_LIMIT),
        name="outproj_ln_router",
    )(a_out, dn_out, x2d, w_out.astype(BF16), vec(b_out), vec(ln_g), vec(ln_b), rw, rb)


def _pool_kernel(x_ref, pw_ref, pb_ref, ps_ref, g_ref, b_ref, rw_ref, rb_ref,
                 x1_ref, slab_ref, ti_ref, gate_ref, rank_ref, cnt_ref,
                 carry_ref, halo_ref, pad_ref):
    tm = x_ref.shape[0]
    halo = POOL_HALO
    b = pl.program_id(0)
    s = pl.program_id(1)

    @pl.when((b == 0) & (s == 0))
    def _():
        carry_ref[...] = jnp.zeros_like(carry_ref)

    @pl.when(s == 0)
    def _():
        halo_ref[...] = jnp.zeros_like(halo_ref)

    x = x_ref[...]
    pad_ref[0:halo, :] = halo_ref[...]
    pad_ref[halo:, :] = x
    halo_ref[...] = x[tm - halo:tm, :]
    tpos = s * tm + lax.broadcasted_iota(jnp.int32, (tm, 1), 0)
    outs = []
    for level, win in enumerate(POOL_WINDOWS, start=1):
        lo = SUBLANES * level
        rows = tm + halo - lo
        cols = slice((level - 1) * POOL_GROUP, D_MODEL)
        shift = win // 2
        pad_ref[pl.ds(lo, rows), cols] = (pad_ref[pl.ds(lo, rows), cols]
                                          + pad_ref[pl.ds(lo - shift, rows), cols])
        gcols = slice((level - 1) * POOL_GROUP, level * POOL_GROUP)
        cnt = jnp.minimum(tpos + 1, win).astype(F32)
        pooled = pad_ref[pl.ds(halo, tm), gcols] / cnt - x[:, gcols]
        y = _dot(pooled.astype(BF16), pw_ref[level - 1]) + pb_ref[level - 1:level, :]
        outs.append(y)
    h = jnp.concatenate(outs, axis=1) * ps_ref[...]
    x1 = _layer_norm(DEEPNORM_ALPHA * x + h, g_ref[...], b_ref[...])
    _route_tail(x1, rw_ref, rb_ref, carry_ref, x1_ref, slab_ref, ti_ref, gate_ref, rank_ref, cnt_ref)


def _pool(x2d, pool_w, pool_b, pool_scale, ln_g, ln_b, router_w, router_b, batch, t):
    n = x2d.shape[0]
    tm = 256
    steps = t // tm
    rw, rb = _router_params(router_w, router_b)
    row = lambda b, s: (b * steps + s, 0)
    const2 = lambda b, s: (0, 0)
    vec = lambda v: v.astype(F32).reshape(1, D_MODEL)
    ng = len(POOL_WINDOWS)
    return pl.pallas_call(
        _pool_kernel,
        grid=(batch, steps),
        in_specs=[
            pl.BlockSpec((tm, D_MODEL), row),
            pl.BlockSpec((ng, POOL_GROUP, POOL_GROUP), lambda b, s: (0, 0, 0)),
            pl.BlockSpec((ng, POOL_GROUP), const2),
            pl.BlockSpec((1, D_MODEL), const2),
            pl.BlockSpec((1, D_MODEL), const2),
            pl.BlockSpec((1, D_MODEL), const2),
            pl.BlockSpec((2, D_MODEL, LANES), lambda b, s: (0, 0, 0)),
            pl.BlockSpec((1, LANES), const2),
        ],
        out_specs=_route_out_specs(tm, lambda b, s: b * steps + s),
        out_shape=_route_out_shapes(n),
        scratch_shapes=[
            pltpu.VMEM((1, LANES), F32),
            pltpu.VMEM((POOL_HALO, D_MODEL), F32),
            pltpu.VMEM((tm + POOL_HALO, D_MODEL), F32),
        ],
        compiler_params=pltpu.CompilerParams(dimension_semantics=("arbitrary", "arbitrary"),
                                             vmem_limit_bytes=VMEM_LIMIT),
        name="pool_ln_router",
    )(x2d, pool_w.astype(BF16), pool_b.astype(F32), vec(pool_scale), vec(ln_g), vec(ln_b), rw, rb)


def _swap_vreg_sublane(vs, order=(4, 2, 1)):
    sub = lax.broadcasted_iota(jnp.int32, vs[0].shape, 1)
    vs = list(vs)
    for d in order:
        keep = (sub & d) == 0
        nxt = list(vs)
        for j in range(SUBLANES):
            if j & d:
                continue
            a, b = vs[j], vs[j + d]
            nxt[j] = jnp.where(keep, a, pltpu.roll(b, d, 1))
            nxt[j + d] = jnp.where(keep, pltpu.roll(a, SUBLANES - d, 1), b)
        vs = nxt
    return vs


def _inv_kernel(starts_ref, cnt_ref, pos_ref, inv_ref):
    nb = pos_ref.shape[0]
    unroll = 8
    i = pl.program_id(0)

    @pl.when(i == 0)
    def _():
        def fill(lo, hi):
            def body(r, c):
                inv_ref[r] = -1
                return c
            lax.fori_loop(lo, hi, body, 0)

        def per_expert(e, carry):
            c0 = cnt_ref[e]
            fill(starts_ref[e] + c0, starts_ref[e] + jnp.bitwise_and(c0 + (MOE_TM - 1), -MOE_TM))
            return carry

        lax.fori_loop(0, N_EXPERTS, per_expert, 0)
        last = N_EXPERTS - 1
        fill(starts_ref[last] + jnp.bitwise_and(cnt_ref[last] + (MOE_TM - 1), -MOE_TM), inv_ref.shape[0])

    def body(j, carry):
        for u in range(unroll):
            a = j * unroll + u
            inv_ref[pos_ref[a]] = i * nb + a
        return carry

    lax.fori_loop(0, nb // unroll, body, 0)


def _build_inv(starts, counts, pos_flat, rows):
    na = pos_flat.shape[0]
    nb = min(8192, na)
    grid_spec = pltpu.PrefetchScalarGridSpec(
        num_scalar_prefetch=2,
        grid=(na // nb,),
        in_specs=[pl.BlockSpec((nb,), lambda i, st, ct: (i,), memory_space=pltpu.SMEM)],
        out_specs=pl.BlockSpec((rows,), lambda i, st, ct: (0,), memory_space=pltpu.SMEM),
    )
    return pl.pallas_call(
        _inv_kernel,
        grid_spec=grid_spec,
        out_shape=jax.ShapeDtypeStruct((rows,), jnp.int32),
        compiler_params=pltpu.CompilerParams(dimension_semantics=("arbitrary",)),
        name="moe_inverse_perm",
    )(starts, counts, pos_flat)


def _moe_kernel(layer, te_ref, first_ref, slot_ref, nxt_ref, nu_ref,
                src_cur_ref, src_next_ref, dst_a_ref, dst_b_ref, b1a_ref, b2a_ref, b1b_ref, b2b_ref,
                x_hbm, w1_hbm, w2_hbm, y_hbm,
                w1s_ref, w2s_ref, w1p_ref, w2b_ref, xbuf_ref, ybuf_ref, sem, gsem, ssem):
    tm = MOE_TM
    g = tm // SUBLANES
    pw = 2 * LANES
    step = pl.program_id(0)
    spare = 2 * tm

    def weight_copies(expert, slot):
        return (pltpu.make_async_copy(w1_hbm.at[layer, expert], w1s_ref.at[slot], sem.at[0, slot]),
                pltpu.make_async_copy(w2_hbm.at[layer, expert], w2s_ref.at[slot], sem.at[1, slot]))

    def gather_copy(src_ref, off, r, slot):
        return pltpu.make_async_copy(x_hbm.at[src_ref[off + r]], xbuf_ref.at[slot * tm + r], gsem.at[slot])

    def scatter_copy(dst_ref, off, r, slot):
        return pltpu.make_async_copy(ybuf_ref.at[slot * tm + r], y_hbm.at[dst_ref[off + r]], ssem.at[slot])

    def wait_gather(slot):
        view = xbuf_ref.at[pl.ds(slot * tm, tm)]
        pltpu.make_async_copy(view, view, gsem.at[slot]).wait()

    def wait_scatter(slot):
        view = ybuf_ref.at[pl.ds(slot * tm, tm)]
        pltpu.make_async_copy(view, view, ssem.at[slot]).wait()

    @pl.when(step == 0)
    def _():
        for c in weight_copies(te_ref[0], 0):
            c.start()
        ybuf_ref[pl.ds(tm, tm)] = jnp.zeros((tm, SLAB, LANES), F32)
        xbuf_ref[spare] = jnp.zeros((SLAB, LANES), F32)

        def first_gather(r, c):
            gather_copy(src_cur_ref, 0, r, 0).start()
            return c

        lax.fori_loop(0, tm, first_gather, 0)

    def tile(sub, b1_ref, b2_ref):
        i = step * MOE_TILES_PER_STEP + sub
        other = 1 - sub
        used = i < nu_ref[0]
        e = te_ref[i]
        nxt_src, nxt_off = (src_cur_ref, tm) if sub == 0 else (src_next_ref, 0)
        prv_dst, prv_off = (dst_a_ref, tm) if sub == 0 else (dst_b_ref, 0)

        @pl.when(used & (first_ref[i] == 1))
        def _():
            slot = slot_ref[i]
            nxt = nxt_ref[i]

            @pl.when(nxt >= 0)
            def _():
                for c in weight_copies(nxt, 1 - slot):
                    c.start()

            for c in weight_copies(e, slot):
                c.wait()
            r = lax.broadcasted_iota(jnp.int32, (pw, pw), 0)
            cidx = lax.broadcasted_iota(jnp.int32, (pw, pw), 1)
            src = jnp.where(cidx < LANES, 2 * cidx, 2 * (cidx - LANES) + 1)
            perm = (r == src).astype(BF16)
            for blk in range(2 * D_EXPERT // pw):
                wb = w1s_ref[slot, :, blk * pw:(blk + 1) * pw].astype(BF16)
                w1p_ref[:, blk * pw:(blk + 1) * pw] = _dot(wb, perm).astype(BF16)
            w2b_ref[...] = w2s_ref[slot].astype(BF16)

        @pl.when(used)
        def _():
            wait_gather(sub)

            @pl.when(i >= 1)
            def _():
                wait_scatter(sub)

            n_blocks = 2 * D_EXPERT // pw + D_MODEL // pw
            slices = SUBLANES
            rows_per = tm // slices
            n_groups = n_blocks * slices
            bounds = [(a * tm) // n_groups for a in range(n_groups + 1)]
            group = [0]

            def paced_zero():
                a = group[0]
                group[0] += 1
                d = None
                for r in range(bounds[a], bounds[a + 1]):
                    gather_copy(nxt_src, nxt_off, r, other).start(priority=0)
                    d = prv_dst[prv_off + r]
                    pltpu.make_async_copy(ybuf_ref.at[other * tm + r], y_hbm.at[d], ssem.at[other]).start(priority=1)
                idx = spare + jnp.right_shift(d, 31)
                ybuf_ref[idx] = jnp.zeros((SLAB, LANES), F32)
                z = xbuf_ref[idx] + ybuf_ref[idx]
                return jnp.concatenate([z[0:1, :], z[0:1, :]], axis=1)

            slabs = [xbuf_ref[pl.ds(sub * tm + j, g, stride=SUBLANES)] for j in range(SUBLANES)]
            chunks = _swap_vreg_sublane(slabs)
            x = jnp.concatenate([c.reshape(tm, LANES) for c in chunks], axis=1).astype(BF16)
            acts = []
            for blk in range(2 * D_EXPERT // pw):
                cols = slice(blk * pw, (blk + 1) * pw)
                hd = _dot(x, w1p_ref[:, cols])
                bias = b1_ref[0, 0, :, cols]
                parts = []
                for j in range(slices):
                    h = hd[j * rows_per:(j + 1) * rows_per, :] + (bias + paced_zero())
                    hg = jnp.minimum(h[:, :LANES], SWIGLU_LIMIT)
                    hl = jnp.clip(h[:, LANES:], -SWIGLU_LIMIT, SWIGLU_LIMIT)
                    parts.append((hg * _sigmoid(SWIGLU_ALPHA * hg) * (hl + 1.0)).astype(BF16))
                acts.append(jnp.concatenate(parts, axis=0))
            act = jnp.concatenate(acts, axis=1)
            ys = []
            for nb in range(D_MODEL // pw):
                cols = slice(nb * pw, (nb + 1) * pw)
                yd = _dot(act, w2b_ref[:, cols])
                bias = b2_ref[0, 0, :, cols]
                y = jnp.concatenate([yd[j * rows_per:(j + 1) * rows_per, :] + (bias + paced_zero())
                                     for j in range(slices)], axis=0)
                ys += [y[:, s * LANES:(s + 1) * LANES].reshape(g, SUBLANES, LANES) for s in range(pw // LANES)]
            out_slabs = _swap_vreg_sublane(ys, order=(1, 2, 4))
            for j in range(SUBLANES):
                ybuf_ref[pl.ds(sub * tm + j, g, stride=SUBLANES)] = out_slabs[j]

        @pl.when(i == nu_ref[0])
        def _():
            wait_gather(sub)
            wait_scatter(sub)

            def last_scatter(r, c):
                scatter_copy(prv_dst, prv_off, r, other).start()
                return c

            lax.fori_loop(0, tm, last_scatter, 0)
            wait_scatter(other)

    tile(0, b1a_ref, b2a_ref)
    tile(1, b1b_ref, b2b_ref)


def _moe_mlp(layer, meta, inv, x_slab3, w1, b1p, w2, b2):
    tps = MOE_TILES_PER_STEP
    tm = MOE_TM * tps
    tile_expert, first, slot, nxt, n_used = meta
    n_assign = x_slab3.shape[0] * TOP_K
    steps = inv.shape[0] // tm
    spare = n_assign + jnp.bitwise_and(jnp.arange(inv.shape[0], dtype=jnp.int32), MOE_TM - 1)
    src = jnp.right_shift(jnp.maximum(inv, 0), 2)
    dst = jnp.concatenate([spare[:tm], jnp.where(inv < 0, spare, inv)])

    def bias_map(sub):
        return lambda i, te, fi, sl, nx, nu: (layer, te[i * tps + sub], 0, 0)

    def idx_spec(shift):
        return pl.BlockSpec((tm,), lambda i, te, fi, sl, nx, nu: (jnp.minimum(i + shift, steps - 1 + shift),),
                            memory_space=pltpu.SMEM)

    grid_spec = pltpu.PrefetchScalarGridSpec(
        num_scalar_prefetch=5,
        grid=(steps,),
        in_specs=[
            idx_spec(0), pl.BlockSpec((tm,), lambda i, te, fi, sl, nx, nu: (jnp.minimum(i + 1, steps - 1),),
                                      memory_space=pltpu.SMEM),
            idx_spec(0), idx_spec(1),
            pl.BlockSpec((1, 1, 1, 2 * D_EXPERT), bias_map(0)),
            pl.BlockSpec((1, 1, 1, D_MODEL), bias_map(0)),
            pl.BlockSpec((1, 1, 1, 2 * D_EXPERT), bias_map(1)),
            pl.BlockSpec((1, 1, 1, D_MODEL), bias_map(1)),
            pl.BlockSpec(memory_space=pl.ANY),
            pl.BlockSpec(memory_space=pl.ANY),
            pl.BlockSpec(memory_space=pl.ANY),
        ],
        out_specs=pl.BlockSpec(memory_space=pl.ANY),
        scratch_shapes=[
            pltpu.VMEM((2, D_MODEL, 2 * D_EXPERT), F32),
            pltpu.VMEM((2, D_EXPERT, D_MODEL), F32),
            pltpu.VMEM((D_MODEL, 2 * D_EXPERT), BF16),
            pltpu.VMEM((D_EXPERT, D_MODEL), BF16),
            pltpu.VMEM((2 * MOE_TM + 1, SLAB, LANES), F32),
            pltpu.VMEM((2 * MOE_TM + 1, SLAB, LANES), F32),
            pltpu.SemaphoreType.DMA((2, 2)),
            pltpu.SemaphoreType.DMA((2,)),
            pltpu.SemaphoreType.DMA((2,)),
        ],
    )
    return pl.pallas_call(
        functools.partial(_moe_kernel, layer),
        grid_spec=grid_spec,
        out_shape=jax.ShapeDtypeStruct((n_assign + MOE_TM, SLAB, LANES), F32),
        compiler_params=pltpu.CompilerParams(dimension_semantics=("arbitrary",),
                                             vmem_limit_bytes=VMEM_LIMIT, has_side_effects=True),
        name="moe_grouped_mlp",
    )(tile_expert, first, slot, nxt, n_used, src, src, dst, dst, b1p, b2, b1p, b2, x_slab3, w1, w2)


def _combine_kernel(gate_ref, y_ref, x1_ref, g_ref, b_ref, o_ref, mix_ref):
    tc = x1_ref.shape[0]
    g = tc // SUBLANES

    def mix_group(grp, carry):
        for j in range(SUBLANES):
            t = grp * SUBLANES + j
            a0 = t * TOP_K
            acc = gate_ref[a0] * y_ref[a0]
            for k in range(1, TOP_K):
                acc = acc + gate_ref[a0 + k] * y_ref[a0 + k]
            mix_ref[t] = acc
        return carry

    lax.fori_loop(0, g, mix_group, 0)
    slabs = [mix_ref[pl.ds(j, g, stride=SUBLANES)] for j in range(SUBLANES)]
    chunks = _swap_vreg_sublane(slabs)
    f = jnp.concatenate([c.reshape(tc, LANES) for c in chunks], axis=1)
    o_ref[...] = _layer_norm(DEEPNORM_ALPHA * x1_ref[...] + f, g_ref[...], b_ref[...])


def _combine(y_tok, gate, x1, ln_g, ln_b):
    n = x1.shape[0]
    tc = 256
    vec = lambda v: v.astype(F32).reshape(1, D_MODEL)
    return pl.pallas_call(
        _combine_kernel,
        grid=(n // tc,),
        in_specs=[
            pl.BlockSpec((tc * TOP_K,), lambda i: (i,), memory_space=pltpu.SMEM),
            pl.BlockSpec((tc * TOP_K, SLAB, LANES), lambda i: (i, 0, 0)),
            pl.BlockSpec((tc, D_MODEL), lambda i: (i, 0)),
            pl.BlockSpec((1, D_MODEL), lambda i: (0, 0)),
            pl.BlockSpec((1, D_MODEL), lambda i: (0, 0)),
        ],
        out_specs=pl.BlockSpec((tc, D_MODEL), lambda i: (i, 0)),
        out_shape=jax.ShapeDtypeStruct((n, D_MODEL), F32),
        scratch_shapes=[pltpu.VMEM((tc, SLAB, LANES), F32)],
        compiler_params=pltpu.CompilerParams(dimension_semantics=("arbitrary",),
                                             vmem_limit_bytes=VMEM_LIMIT),
        name="moe_combine_ln",
    )(gate.reshape(-1), y_tok, x1, vec(ln_g), vec(ln_b))


def _group_metadata(counts, n_tiles):
    tm = MOE_TM
    experts = jnp.arange(N_EXPERTS, dtype=jnp.int32)
    cnt = counts[0, :N_EXPERTS]
    tiles_per = (cnt + tm - 1) // tm
    tile_end = jnp.cumsum(tiles_per)
    starts = ((tile_end - tiles_per) * tm).astype(jnp.int32)
    n_used = tile_end[-1:].astype(jnp.int32)
    active = tiles_per > 0
    te_last = jnp.max(jnp.where(active, experts, 0))
    tidx = jnp.arange(n_tiles, dtype=jnp.int32)
    te = jnp.sum((tidx[:, None] >= tile_end[None, :]).astype(jnp.int32), axis=1)
    is_used = tidx < n_used[0]
    tile_expert = jnp.where(is_used, jnp.minimum(te, N_EXPERTS - 1), te_last).astype(jnp.int32)
    prev = jnp.concatenate([tile_expert[:1] - 1, tile_expert[:-1]])
    first = (is_used & (tile_expert != prev)).astype(jnp.int32)
    slot_e = ((jnp.cumsum(active.astype(jnp.int32)) - 1) % 2).astype(jnp.int32)
    later = jnp.where(active, experts, N_EXPERTS)
    suffix_min = lax.cummin(later[::-1])[::-1]
    nxt_e = jnp.concatenate([suffix_min[1:], jnp.full((1,), N_EXPERTS, jnp.int32)])
    nxt_e = jnp.where(nxt_e >= N_EXPERTS, -1, nxt_e).astype(jnp.int32)
    return starts, cnt.astype(jnp.int32), (tile_expert, first, slot_e[tile_expert], nxt_e[tile_expert], n_used)


def _moe_layer(layer, routed, w1, b1p, w2, b2, ln_g, ln_b):
    x1, x1_slab, ti, gate, rank, counts = routed
    n = x1.shape[0]
    n_tiles = n * TOP_K // MOE_TM + N_EXPERTS + MOE_TILES_PER_STEP
    assert n_tiles % MOE_TILES_PER_STEP == 0
    starts, cnt, meta = _group_metadata(counts, n_tiles)
    pos = (jnp.take(starts, ti.reshape(-1)) + rank.reshape(-1)).astype(jnp.int32)
    inv = _build_inv(starts, cnt, pos, n_tiles * MOE_TM)
    y_tok = _moe_mlp(layer, meta, inv, x1_slab.reshape(n, SLAB, LANES), w1, b1p, w2, b2)
    return _combine(y_tok, gate, x1, ln_g, ln_b)


def kernel(x, positions, mix_w_in, mix_b_in, dn_conv_w, dn_a_log, dn_dt_bias, dn_norm_w, swa_sinks,
           mix_w_out, mix_b_out, pool_w, pool_b, pool_scale, ln1_g, ln1_b, router_w, router_b,
           moe_w1, moe_b1, moe_w2, moe_b2, ln2_g, ln2_b):
    batch, t, d = x.shape
    n = batch * t
    x2d = x.reshape(n, d)
    pos2d = positions.reshape(n, 1).astype(jnp.int32)
    pw = 2 * LANES
    b1p = moe_b1.astype(F32).reshape(DEPTH, N_EXPERTS, 2 * D_EXPERT // pw, LANES, 2)
    b1p = jnp.swapaxes(b1p, 3, 4).reshape(DEPTH, N_EXPERTS, 1, 2 * D_EXPERT)
    b2r = moe_b2.astype(F32).reshape(DEPTH, N_EXPERTS, 1, D_MODEL)
    for layer in range(DEPTH):
        i = layer // 2
        if layer % 2 == 0:
            q, kv, dq, dk, dv, dz, ab = _inproj(x2d, pos2d, mix_w_in[i], mix_b_in[i])
            a_out = _swa(q, kv, swa_sinks[i], batch, t)
            dn_out = _deltanet(dq, dk, dv, dz, ab, dn_conv_w[i], dn_a_log[i], dn_dt_bias[i],
                               dn_norm_w[i], batch, t)
            routed = _post(a_out, dn_out, x2d, mix_w_out[i], mix_b_out[i], ln1_g[layer], ln1_b[layer],
                           router_w[layer], router_b[layer])
        else:
            routed = _pool(x2d, pool_w[i], pool_b[i], pool_scale[i], ln1_g[layer], ln1_b[layer],
                           router_w[layer], router_b[layer], batch, t)
        x2d = _moe_layer(layer, routed, moe_w1, b1p, moe_w2, b2r, ln2_g[layer], ln2_b[layer])
    return x2d.reshape(batch, t, d)
```

```python
import functools
import math

import numpy as np
import jax
import jax.numpy as jnp
from jax import lax
from jax.experimental import pallas as pl
from jax.experimental.pallas import tpu as pltpu

D_MODEL = 1024
DEPTH = 2
SWA_Q_HEADS = 8
SWA_KV_HEADS = 2
SWA_HEAD_DIM = 64
SWA_BLOCK = 128
ROPE_THETA = 500000.0
ROPE_DIM = SWA_HEAD_DIM // 4
DN_HEADS = 4
DN_HEAD_DIM = 128
DN_CONV = 4
POOL_WINDOWS = (2, 4, 8, 16)
POOL_GROUP = D_MODEL // 4
N_EXPERTS = 32
TOP_K = 4
D_EXPERT = D_MODEL
SWIGLU_LIMIT = 7.0
SWIGLU_ALPHA = 1.702
LN_EPS = 1e-5
RMS_EPS = 1e-6
DEEPNORM_ALPHA = (2 * DEPTH) ** 0.25
SWA_Q_WIDTH = SWA_Q_HEADS * SWA_HEAD_DIM
SWA_KV_WIDTH = SWA_KV_HEADS * SWA_HEAD_DIM
DN_WIDTH = DN_HEADS * DN_HEAD_DIM

LANES = 128
SUBLANES = 8
SLAB = D_MODEL // LANES
IN_PAD_WIDTH = 3328
VMEM_LIMIT = 56 * 1024 * 1024

F32 = jnp.float32
BF16 = jnp.bfloat16
HIGHEST = lax.Precision.HIGHEST

MOE_TM = 256
MOE_TILES_PER_STEP = 2
DN_CHUNK = 128
POOL_HALO = SUBLANES * len(POOL_WINDOWS)


def _sigmoid(x):
    return 0.5 + 0.5 * jnp.tanh(0.5 * x)


def _layer_norm(z, g, b):
    mu = jnp.mean(z, axis=-1, keepdims=True)
    zc = z - mu
    var = jnp.mean(zc * zc, axis=-1, keepdims=True)
    return zc * lax.rsqrt(var + LN_EPS) * g + b


def _dot(a, b):
    return jnp.dot(a, b, preferred_element_type=F32)


def _dot_nt(a, b):
    return lax.dot_general(a, b, (((1,), (1,)), ((), ())), preferred_element_type=F32)


def _route_tail(x1, rw_ref, rb_ref, carry_ref, x1_ref, slab_ref, ti_ref, gate_ref, rank_ref, cnt_ref):
    tm = x1.shape[0]
    x1_ref[...] = x1
    for s in range(SLAB):
        slab_ref[pl.ds(s, tm, stride=SLAB), :] = x1[:, s * LANES:(s + 1) * LANES]

    xh = x1.astype(BF16)
    xl = (x1 - xh.astype(F32)).astype(BF16)
    logits = _dot(xh, rw_ref[0]) + (_dot(xh, rw_ref[1]) + _dot(xl, rw_ref[0])) + rb_ref[...]
    lane = lax.broadcasted_iota(jnp.int32, (tm, LANES), 1)
    lane_f = lane.astype(F32)
    l = jnp.where(lane < N_EXPERTS, logits, -jnp.inf)
    vals, hits = [], []
    ti_out = jnp.zeros((tm, LANES), jnp.int32)
    for k in range(TOP_K):
        m = jnp.max(l, axis=-1, keepdims=True)
        idx = jnp.min(jnp.where(l == m, lane_f, float(LANES)), axis=-1, keepdims=True)
        hit = lane_f == idx
        l = jnp.where(hit, -jnp.inf, l)
        vals.append(m)
        hits.append(hit)
        ti_out = jnp.where(lane == k, idx.astype(jnp.int32), ti_out)
    exps = [jnp.exp(v - vals[0]) for v in vals]
    den = exps[0] + exps[1] + exps[2] + exps[3]
    gate_out = jnp.zeros((tm, LANES), F32)
    for k in range(TOP_K):
        gate_out = jnp.where(lane == k, exps[k] / den, gate_out)

    sel = jnp.zeros((tm, LANES), F32)
    for k in range(TOP_K):
        sel = sel + hits[k].astype(F32)
    ri = lax.broadcasted_iota(jnp.int32, (tm, tm), 0)
    ci = lax.broadcasted_iota(jnp.int32, (tm, tm), 1)
    tri = (ri > ci).astype(BF16)
    carry = carry_ref[...]
    prefix = _dot(tri, sel.astype(BF16)) + carry
    rank_out = jnp.zeros((tm, LANES), jnp.int32)
    for k in range(TOP_K):
        r = jnp.sum(jnp.where(hits[k], prefix, 0.0), axis=-1, keepdims=True)
        rank_out = jnp.where(lane == k, r.astype(jnp.int32), rank_out)
    new_carry = carry + jnp.sum(sel, axis=0, keepdims=True)
    carry_ref[...] = new_carry
    cnt_ref[...] = new_carry.astype(jnp.int32)
    ti_ref[...] = ti_out[:, :TOP_K]
    gate_ref[...] = gate_out[:, :TOP_K]
    rank_ref[...] = rank_out[:, :TOP_K]


def _route_out_shapes(n):
    return (
        jax.ShapeDtypeStruct((n, D_MODEL), F32),
        jax.ShapeDtypeStruct((n * SLAB, LANES), F32),
        jax.ShapeDtypeStruct((n, TOP_K), jnp.int32),
        jax.ShapeDtypeStruct((n, TOP_K), F32),
        jax.ShapeDtypeStruct((n, TOP_K), jnp.int32),
        jax.ShapeDtypeStruct((1, LANES), jnp.int32),
    )


def _route_out_specs(tm, row_map):
    return (
        pl.BlockSpec((tm, D_MODEL), lambda *a: (row_map(*a), 0)),
        pl.BlockSpec((tm * SLAB, LANES), lambda *a: (row_map(*a), 0)),
        pl.BlockSpec((tm, TOP_K), lambda *a: (row_map(*a), 0)),
        pl.BlockSpec((tm, TOP_K), lambda *a: (row_map(*a), 0)),
        pl.BlockSpec((tm, TOP_K), lambda *a: (row_map(*a), 0)),
        pl.BlockSpec((1, LANES), lambda *a: (0, 0)),
    )


def _inproj_kernel(x_ref, pos_ref, invf_ref, w_ref, b_ref,
                   q_ref, kv_ref, dq_ref, dk_ref, dv_ref, dz_ref, ab_ref):
    x = x_ref[...].astype(BF16)
    proj = _dot(x, w_ref[...]) + b_ref[...]
    tm = x.shape[0]
    ang = pos_ref[...].astype(F32) * invf_ref[...]
    cos = jnp.cos(ang)
    sin = jnp.sin(ang)
    d = lax.broadcasted_iota(jnp.int32, (tm, LANES), 1) % SWA_HEAD_DIM
    half = ROPE_DIM // 2
    c_tab = jnp.where(d < ROPE_DIM, cos, 1.0)
    s_lo = jnp.where(d < half, -sin, 0.0)
    s_hi = jnp.where((d >= half) & (d < ROPE_DIM), sin, 0.0)

    def rot(xc):
        return (xc * c_tab + pltpu.roll(xc, LANES - half, 1) * s_lo
                + pltpu.roll(xc, half, 1) * s_hi)

    for c in range(SWA_Q_WIDTH // LANES):
        q_ref[:, c * LANES:(c + 1) * LANES] = rot(proj[:, c * LANES:(c + 1) * LANES])
    o = SWA_Q_WIDTH
    kv_ref[:, :LANES] = rot(proj[:, o:o + LANES])
    kv_ref[:, LANES:] = proj[:, o + LANES:o + 2 * LANES]
    o += 2 * SWA_KV_WIDTH
    dq_ref[...] = proj[:, o:o + DN_WIDTH]
    dk_ref[...] = proj[:, o + DN_WIDTH:o + 2 * DN_WIDTH]
    dv_ref[...] = proj[:, o + 2 * DN_WIDTH:o + 3 * DN_WIDTH]
    o += 3 * DN_WIDTH
    dz_ref[...] = proj[:, o:o + DN_WIDTH]
    o += DN_WIDTH
    ab_ref[...] = proj[:, o:o + LANES]


def _inproj(x2d, pos2d, w_in, b_in):
    n = x2d.shape[0]
    tm = 256
    in_width = w_in.shape[1]
    w = jnp.pad(w_in, ((0, 0), (0, IN_PAD_WIDTH - in_width))).astype(BF16)
    b = jnp.pad(b_in, (0, IN_PAD_WIDTH - in_width)).reshape(1, IN_PAD_WIDTH)
    half = ROPE_DIM // 2
    lane_d = np.arange(LANES) % SWA_HEAD_DIM
    invf = (ROPE_THETA ** (-(lane_d % half).astype(np.float64) / half)).astype(np.float32)
    invf = jnp.asarray(invf.reshape(1, LANES))
    row = lambda i: (i, 0)
    const = lambda i: (0, 0)
    widths = (SWA_Q_WIDTH, 2 * SWA_KV_WIDTH, DN_WIDTH, DN_WIDTH, DN_WIDTH, DN_WIDTH, LANES)
    return pl.pallas_call(
        _inproj_kernel,
        grid=(n // tm,),
        in_specs=[
            pl.BlockSpec((tm, D_MODEL), row),
            pl.BlockSpec((tm, 1), row),
            pl.BlockSpec((1, LANES), const),
            pl.BlockSpec((D_MODEL, IN_PAD_WIDTH), const),
            pl.BlockSpec((1, IN_PAD_WIDTH), const),
        ],
        out_specs=tuple(pl.BlockSpec((tm, wd), row) for wd in widths),
        out_shape=tuple(jax.ShapeDtypeStruct((n, wd), F32) for wd in widths),
        compiler_params=pltpu.CompilerParams(dimension_semantics=("arbitrary",),
                                             vmem_limit_bytes=VMEM_LIMIT),
        name="inproj_rotary",
    )(x2d, pos2d, invf, w, b)


def _swa_kernel(sink_ref, q_ref, kv_ref, o_ref):
    t = q_ref.shape[0]
    blk = SWA_BLOCK
    lane = lax.broadcasted_iota(jnp.int32, (2 * blk, LANES), 1)
    qi = lax.broadcasted_iota(jnp.int32, (blk, 2 * blk), 0)
    kj = lax.broadcasted_iota(jnp.int32, (blk, 2 * blk), 1)
    rel = qi + blk - kj
    in_window = (rel >= 0) & (rel < blk)
    is_cur = kj >= blk
    scale = SWA_HEAD_DIM ** -0.5

    group = SWA_Q_HEADS // SWA_KV_HEADS
    n_chunks = SWA_Q_WIDTH // LANES
    lane_q = lax.broadcasted_iota(jnp.int32, (blk, LANES), 1)
    blocks_per_iter = 2

    def body(it, carry):
        items = []
        for sub in range(blocks_per_iter):
            n = it * blocks_per_iter + sub
            r0 = pl.multiple_of(n * blk, blk)
            p0 = pl.multiple_of(jnp.maximum(n - 1, 0) * blk, blk)
            kvc = kv_ref[pl.ds(r0, blk), :]
            kvp = kv_ref[pl.ds(p0, blk), :]
            kband = jnp.concatenate([kvp[:, :LANES], kvc[:, :LANES]], axis=0)
            vband = jnp.concatenate([kvp[:, LANES:], kvc[:, LANES:]], axis=0)
            valid = in_window & (is_cur | (n > 0))
            k_rhs, v_rhs = [], []
            for h in range(SWA_KV_HEADS):
                in_head = (lane >= h * SWA_HEAD_DIM) & (lane < (h + 1) * SWA_HEAD_DIM)
                km = jnp.where(in_head, kband, 0.0)
                vm = jnp.where(in_head, vband, 0.0)
                kr = pltpu.roll(km, SWA_HEAD_DIM, 1)
                vr = pltpu.roll(vm, SWA_HEAD_DIM, 1)
                lo_k, hi_k = (km, kr) if h == 0 else (kr, km)
                lo_v, hi_v = (vm, vr) if h == 0 else (vr, vm)
                k_rhs.append(jnp.concatenate([lo_k, hi_k], axis=0).astype(BF16))
                v_rhs.append(jnp.concatenate([lo_v, hi_v], axis=0).astype(BF16))
            for c in range(n_chunks):
                items.append((r0, c, valid, k_rhs[(2 * c) // group], v_rhs[(2 * c) // group]))

        scores = [_dot_nt((q_ref[pl.ds(r0, blk), c * LANES:(c + 1) * LANES] * scale).astype(BF16), kr_)
                  for r0, c, _, kr_, _ in items]
        masked = [[jnp.where(valid, s[:, j * 2 * blk:(j + 1) * 2 * blk], -jnp.inf) for j in range(2)]
                  for s, (_, _, valid, _, _) in zip(scores, items)]
        maxes = [[jnp.maximum(jnp.max(s[j], axis=-1, keepdims=True), sink_ref[2 * c + j]) for j in range(2)]
                 for s, (_, c, _, _, _) in zip(masked, items)]
        probs = [[jnp.exp(s[j] - m[j]) for j in range(2)] for s, m in zip(masked, maxes)]
        dens = [[jnp.sum(p[j], axis=-1, keepdims=True) + jnp.exp(sink_ref[2 * c + j] - m[j]) for j in range(2)]
                for p, m, (_, c, _, _, _) in zip(probs, maxes, items)]
        for p, dn, (r0, c, _, _, vr_) in zip(probs, dens, items):
            o = _dot(jnp.concatenate([p[0].astype(BF16), p[1].astype(BF16)], axis=1), vr_)
            inv = jnp.where(lane_q < SWA_HEAD_DIM, 1.0 / dn[0], 1.0 / dn[1])
            o_ref[pl.ds(r0, blk), c * LANES:(c + 1) * LANES] = o * inv
        return carry

    lax.fori_loop(0, t // (blk * blocks_per_iter), body, 0)


def _swa(q, kv, sinks, batch, t):
    return pl.pallas_call(
        _swa_kernel,
        grid=(batch,),
        in_specs=[
            pl.BlockSpec(memory_space=pltpu.SMEM),
            pl.BlockSpec((t, SWA_Q_WIDTH), lambda b: (b, 0)),
            pl.BlockSpec((t, 2 * SWA_KV_WIDTH), lambda b: (b, 0)),
        ],
        out_specs=pl.BlockSpec((t, SWA_Q_WIDTH), lambda b: (b, 0)),
        out_shape=jax.ShapeDtypeStruct((batch * t, SWA_Q_WIDTH), F32),
        compiler_params=pltpu.CompilerParams(dimension_semantics=("arbitrary",),
                                             vmem_limit_bytes=VMEM_LIMIT),
        name="swa_attention",
    )(sinks.astype(F32), q, kv)


def _dn_kernel(convw_ref, hp_ref, normw_ref, sel_ref, ones_ref, dq_ref, dk_ref, dv_ref, dz_ref, ab_ref, o_ref,
               state_ref, halo_ref, pad_ref):
    tc = dq_ref.shape[0]
    c = DN_CHUNK
    halo = SUBLANES

    @pl.when(pl.program_id(1) == 0)
    def _():
        state_ref[...] = jnp.zeros_like(state_ref)
        halo_ref[...] = jnp.zeros_like(halo_ref)

    def conv_silu(x_ref, idx):
        pad_ref[0:halo, :] = halo_ref[idx]
        pad_ref[halo:, :] = x_ref[...]
        halo_ref[idx] = x_ref[tc - halo:tc, :]
        acc = jnp.zeros((tc, DN_WIDTH), F32)
        for j in range(DN_CONV):
            wj = convw_ref[j:j + 1, idx * DN_WIDTH:(idx + 1) * DN_WIDTH]
            acc = acc + wj * pad_ref[pl.ds(halo - DN_CONV + 1 + j, tc), :]
        return acc * _sigmoid(acc)

    q_all = conv_silu(dq_ref, 0)
    k_all = conv_silu(dk_ref, 1)
    v_all = conv_silu(dv_ref, 2)

    ab = ab_ref[...]
    neg_a = hp_ref[0:1, :]
    dt_b = hp_ref[1:2, :]
    sp_arg = ab + dt_b
    softplus = jnp.maximum(sp_arg, 0.0) + jnp.log(1.0 + jnp.exp(-jnp.abs(sp_arg)))
    g_all = neg_a * softplus
    beta_all = _sigmoid(ab)

    d = DN_HEAD_DIM
    ri = lax.broadcasted_iota(jnp.int32, (c, 2 * c), 0)
    ci = lax.broadcasted_iota(jnp.int32, (c, 2 * c), 1) % c
    lower_incl = ri >= ci
    lower_strict = ri > ci
    eye2 = (ri == ci).astype(F32)
    r1 = lax.broadcasted_iota(jnp.int32, (c, c), 0)
    c1 = lax.broadcasted_iota(jnp.int32, (c, c), 1)
    ltri = (r1 >= c1).astype(F32)
    ones_bd = ones_ref[...]

    def lane_bcast(x, first_lane):
        sel = sel_ref[first_lane // 2]
        hi = x.astype(BF16)
        lo = (x - hi.astype(F32)).astype(BF16)
        return _dot(hi, sel) + _dot(lo, sel)

    def blockdiag(x2):
        n = x2.shape[1] // 2
        z = jnp.zeros((x2.shape[0], n), BF16)
        return jnp.concatenate([jnp.concatenate([x2[:, :n], z], axis=1),
                                jnp.concatenate([z, x2[:, n:]], axis=1)], axis=0)

    n_chunks = tc // c
    items = [(ch, pr) for ch in range(n_chunks) for pr in range(DN_HEADS // 2)]

    gcs, gcts = [], []
    for ch in range(n_chunks):
        gc = jnp.dot(ltri, g_all[ch * c:(ch + 1) * c, :], precision=HIGHEST, preferred_element_type=F32)
        gcs.append(gc)
        gcts.append(gc.T)
    pre = []
    for ch, pr in items:
        rows = slice(ch * c, (ch + 1) * c)
        cols2 = slice(pr * 2 * d, (pr + 1) * 2 * d)
        gct = gcts[ch]
        g2 = lane_bcast(gcs[ch], 2 * pr)
        b2 = lane_bcast(beta_all[rows, :], DN_HEADS + 2 * pr)
        grow2 = jnp.concatenate([gct[2 * pr:2 * pr + 1, :], gct[2 * pr + 1:2 * pr + 2, :]], axis=1)
        gl2 = g2[c - 1:c, :]
        q2 = q_all[rows, cols2]
        k2 = k_all[rows, cols2]
        ss = _dot(jnp.concatenate([q2 * q2, k2 * k2], axis=0).astype(BF16), ones_bd)
        qn2 = q2 * lax.rsqrt(ss[:c] + RMS_EPS) * (d ** -0.5)
        kn2 = k2 * lax.rsqrt(ss[c:] + RMS_EPS)
        eg2 = jnp.exp(g2)
        kb2 = kn2 * b2
        decay2 = jnp.exp(jnp.where(lower_incl, g2 - grow2, -jnp.inf))
        lhs = jnp.concatenate([kb2, qn2], axis=0).astype(BF16)
        kn_t2 = jnp.concatenate([kn2[:, :d].T, kn2[:, d:].T], axis=1).astype(BF16)
        kk_qk = _dot(lhs, blockdiag(kn_t2))
        vb2 = (v_all[rows, cols2] * b2).astype(BF16)
        kbeg2 = (kb2 * eg2).astype(BF16)
        zero2 = jnp.zeros((c, 2 * d), BF16)
        rhs_bd = jnp.concatenate(
            [jnp.concatenate([vb2[:, :d], kbeg2[:, :d], zero2], axis=1),
             jnp.concatenate([zero2, vb2[:, d:], kbeg2[:, d:]], axis=1)], axis=0)
        kdec2 = kn2 * jnp.exp(gl2 - g2)
        pre.append(dict(
            a2=jnp.where(lower_strict, kk_qk[:c] * decay2, 0.0),
            qk2=(kk_qk[c:] * decay2).astype(BF16),
            rhs_bd=rhs_bd,
            qdec2=qn2 * eg2,
            kdec_t2=jnp.concatenate([kdec2[:, :d].T, kdec2[:, d:].T], axis=1).astype(BF16),
            glast2=jnp.exp(gl2),
        ))

    tinv = [eye2 - p["a2"] for p in pre]
    apow = [p["a2"] for p in pre]
    for _ in range(int(math.log2(c)) - 1):
        a16 = [a.astype(BF16) for a in apow]
        apow = [_dot(a, blockdiag(a)) for a in a16]
        tinv = [t + _dot(t.astype(BF16), blockdiag(a.astype(BF16))) for t, a in zip(tinv, apow)]

    for p, t in zip(pre, tinv):
        sol16 = _dot(t.astype(BF16), p["rhs_bd"]).astype(BF16)
        sol_bd = jnp.concatenate(
            [jnp.concatenate([sol16[:, :2 * d], jnp.zeros((c, 2 * d), BF16)], axis=1),
             jnp.concatenate([jnp.zeros((c, 2 * d), BF16), sol16[:, 2 * d:]], axis=1)], axis=0)
        ks = _dot(p["kdec_t2"], sol_bd)
        qs = _dot(p["qk2"], sol_bd)
        p["n2"] = jnp.concatenate([ks[:, :d], ks[:, 2 * d:3 * d]], axis=1)
        p["k2"] = jnp.concatenate([ks[:, d:2 * d], ks[:, 3 * d:]], axis=1).astype(BF16)
        p["o2"] = jnp.concatenate([qs[:, :d], qs[:, 2 * d:3 * d]], axis=1)
        p["q2"] = (p["qdec2"] - jnp.concatenate([qs[:, d:2 * d], qs[:, 3 * d:]], axis=1)).astype(BF16)

    normw2 = jnp.concatenate([normw_ref[...], normw_ref[...]], axis=1)
    for (ch, pr), p in zip(items, pre):
        rows = slice(ch * c, (ch + 1) * c)
        s2 = state_ref[pr]
        s_bd = blockdiag(s2.astype(BF16))
        o2 = _dot(p["q2"], s_bd) + p["o2"]
        state_ref[pr] = s2 * p["glast2"] - _dot(p["k2"], s_bd) + p["n2"]
        cols2 = slice(pr * 2 * d, (pr + 1) * 2 * d)
        ms = _dot((o2 * o2).astype(BF16), ones_bd) * (1.0 / d)
        z2 = dz_ref[rows, cols2]
        o_ref[rows, cols2] = o2 * lax.rsqrt(ms + RMS_EPS) * normw2 * (z2 * _sigmoid(z2))


def _deltanet(dq, dk, dv, dz, ab, conv_w, a_log, dt_bias, norm_w, batch, t):
    tc = 512
    hp = jnp.zeros((SUBLANES, LANES), F32)
    hp = hp.at[0, :DN_HEADS].set(-jnp.exp(a_log.astype(F32)))
    hp = hp.at[1, :DN_HEADS].set(dt_bias.astype(F32))
    steps = t // tc
    row = lambda b, s: (b * steps + s, 0)
    const = lambda b, s: (0, 0)
    d = DN_HEAD_DIM
    lane_head = np.arange(2 * d) // d
    sel = np.stack([(np.arange(LANES)[:, None] == 2 * idx + lane_head[None, :]) for idx in range(DN_HEADS)])
    ones_bd = lane_head[:, None] == lane_head[None, :]
    sel = jnp.asarray(sel.astype(np.float32), dtype=BF16)
    ones_bd = jnp.asarray(ones_bd.astype(np.float32), dtype=BF16)
    return pl.pallas_call(
        _dn_kernel,
        grid=(batch, steps),
        in_specs=[
            pl.BlockSpec((DN_CONV, 3 * DN_WIDTH), const),
            pl.BlockSpec((SUBLANES, LANES), const),
            pl.BlockSpec((1, DN_HEAD_DIM), const),
            pl.BlockSpec((DN_HEADS, LANES, 2 * DN_HEAD_DIM), lambda b, s: (0, 0, 0)),
            pl.BlockSpec((2 * DN_HEAD_DIM, 2 * DN_HEAD_DIM), const),
            pl.BlockSpec((tc, DN_WIDTH), row),
            pl.BlockSpec((tc, DN_WIDTH), row),
            pl.BlockSpec((tc, DN_WIDTH), row),
            pl.BlockSpec((tc, DN_WIDTH), row),
            pl.BlockSpec((tc, LANES), row),
        ],
        out_specs=pl.BlockSpec((tc, DN_WIDTH), row),
        out_shape=jax.ShapeDtypeStruct((batch * t, DN_WIDTH), F32),
        scratch_shapes=[
            pltpu.VMEM((DN_HEADS // 2, DN_HEAD_DIM, 2 * DN_HEAD_DIM), F32),
            pltpu.VMEM((3, SUBLANES, DN_WIDTH), F32),
            pltpu.VMEM((tc + SUBLANES, DN_WIDTH), F32),
        ],
        compiler_params=pltpu.CompilerParams(dimension_semantics=("arbitrary", "arbitrary"),
                                             vmem_limit_bytes=VMEM_LIMIT),
        name="gated_deltanet",
    )(conv_w.astype(F32), hp, norm_w.astype(F32).reshape(1, DN_HEAD_DIM), sel, ones_bd, dq, dk, dv, dz, ab)


def _post_kernel(a_ref, dn_ref, x_ref, wo_ref, bo_ref, g_ref, b_ref, rw_ref, rb_ref,
                 x1_ref, slab_ref, ti_ref, gate_ref, rank_ref, cnt_ref, carry_ref):
    @pl.when(pl.program_id(0) == 0)
    def _():
        carry_ref[...] = jnp.zeros_like(carry_ref)

    mix = jnp.concatenate([a_ref[...], dn_ref[...]], axis=1).astype(BF16)
    h = _dot(mix, wo_ref[...]) + bo_ref[...]
    x1 = _layer_norm(DEEPNORM_ALPHA * x_ref[...] + h, g_ref[...], b_ref[...])
    _route_tail(x1, rw_ref, rb_ref, carry_ref, x1_ref, slab_ref, ti_ref, gate_ref, rank_ref, cnt_ref)


def _router_params(router_w, router_b):
    rw = jnp.pad(router_w.astype(F32), ((0, 0), (0, LANES - N_EXPERTS)))
    rw_hi = rw.astype(BF16)
    rw_lo = (rw - rw_hi.astype(F32)).astype(BF16)
    rb = jnp.pad(router_b.astype(F32), (0, LANES - N_EXPERTS)).reshape(1, LANES)
    return jnp.stack([rw_hi, rw_lo]), rb


def _post(a_out, dn_out, x2d, w_out, b_out, ln_g, ln_b, router_w, router_b):
    n = x2d.shape[0]
    tm = 256
    rw, rb = _router_params(router_w, router_b)
    row = lambda i: (i, 0)
    const = lambda i: (0, 0)
    vec = lambda v: v.astype(F32).reshape(1, D_MODEL)
    return pl.pallas_call(
        _post_kernel,
        grid=(n // tm,),
        in_specs=[
            pl.BlockSpec((tm, SWA_Q_WIDTH), row),
            pl.BlockSpec((tm, DN_WIDTH), row),
            pl.BlockSpec((tm, D_MODEL), row),
            pl.BlockSpec((SWA_Q_WIDTH + DN_WIDTH, D_MODEL), const),
            pl.BlockSpec((1, D_MODEL), const),
            pl.BlockSpec((1, D_MODEL), const),
            pl.BlockSpec((1, D_MODEL), const),
            pl.BlockSpec((2, D_MODEL, LANES), lambda i: (0, 0, 0)),
            pl.BlockSpec((1, LANES), const),
        ],
        out_specs=_route_out_specs(tm, lambda i: i),
        out_shape=_route_out_shapes(n),
        scratch_shapes=[pltpu.VMEM((1, LANES), F32)],
        compiler_params=pltpu.CompilerParams(dimension_semantics=("arbitrary",),
                                             vmem_limit_bytes=VMEM_LIMIT),
        name="outproj_ln_router",
    )(a_out, dn_out, x2d, w_out.astype(BF16), vec(b_out), vec(ln_g), vec(ln_b), rw, rb)


def _pool_kernel(x_ref, pw_ref, pb_ref, ps_ref, g_ref, b_ref, rw_ref, rb_ref,
                 x1_ref, slab_ref, ti_ref, gate_ref, rank_ref, cnt_ref,
                 carry_ref, halo_ref, pad_ref):
    tm = x_ref.shape[0]
    halo = POOL_HALO
    b = pl.program_id(0)
    s = pl.program_id(1)

    @pl.when((b == 0) & (s == 0))
    def _():
        carry_ref[...] = jnp.zeros_like(carry_ref)

    @pl.when(s == 0)
    def _():
        halo_ref[...] = jnp.zeros_like(halo_ref)

    x = x_ref[...]
    pad_ref[0:halo, :] = halo_ref[...]
    pad_ref[halo:, :] = x
    halo_ref[...] = x[tm - halo:tm, :]
    tpos = s * tm + lax.broadcasted_iota(jnp.int32, (tm, 1), 0)
    outs = []
    for level, win in enumerate(POOL_WINDOWS, start=1):
        lo = SUBLANES * level
        rows = tm + halo - lo
        cols = slice((level - 1) * POOL_GROUP, D_MODEL)
        shift = win // 2
        pad_ref[pl.ds(lo, rows), cols] = (pad_ref[pl.ds(lo, rows), cols]
                                          + pad_ref[pl.ds(lo - shift, rows), cols])
        gcols = slice((level - 1) * POOL_GROUP, level * POOL_GROUP)
        cnt = jnp.minimum(tpos + 1, win).astype(F32)
        pooled = pad_ref[pl.ds(halo, tm), gcols] / cnt - x[:, gcols]
        y = _dot(pooled.astype(BF16), pw_ref[level - 1]) + pb_ref[level - 1:level, :]
        outs.append(y)
    h = jnp.concatenate(outs, axis=1) * ps_ref[...]
    x1 = _layer_norm(DEEPNORM_ALPHA * x + h, g_ref[...], b_ref[...])
    _route_tail(x1, rw_ref, rb_ref, carry_ref, x1_ref, slab_ref, ti_ref, gate_ref, rank_ref, cnt_ref)


def _pool(x2d, pool_w, pool_b, pool_scale, ln_g, ln_b, router_w, router_b, batch, t):
    n = x2d.shape[0]
    tm = 256
    steps = t // tm
    rw, rb = _router_params(router_w, router_b)
    row = lambda b, s: (b * steps + s, 0)
    const2 = lambda b, s: (0, 0)
    vec = lambda v: v.astype(F32).reshape(1, D_MODEL)
    ng = len(POOL_WINDOWS)
    return pl.pallas_call(
        _pool_kernel,
        grid=(batch, steps),
        in_specs=[
            pl.BlockSpec((tm, D_MODEL), row),
            pl.BlockSpec((ng, POOL_GROUP, POOL_GROUP), lambda b, s: (0, 0, 0)),
            pl.BlockSpec((ng, POOL_GROUP), const2),
            pl.BlockSpec((1, D_MODEL), const2),
            pl.BlockSpec((1, D_MODEL), const2),
            pl.BlockSpec((1, D_MODEL), const2),
            pl.BlockSpec((2, D_MODEL, LANES), lambda b, s: (0, 0, 0)),
            pl.BlockSpec((1, LANES), const2),
        ],
        out_specs=_route_out_specs(tm, lambda b, s: b * steps + s),
        out_shape=_route_out_shapes(n),
        scratch_shapes=[
            pltpu.VMEM((1, LANES), F32),
            pltpu.VMEM((POOL_HALO, D_MODEL), F32),
            pltpu.VMEM((tm + POOL_HALO, D_MODEL), F32),
        ],
        compiler_params=pltpu.CompilerParams(dimension_semantics=("arbitrary", "arbitrary"),
                                             vmem_limit_bytes=VMEM_LIMIT),
        name="pool_ln_router",
    )(x2d, pool_w.astype(BF16), pool_b.astype(F32), vec(pool_scale), vec(ln_g), vec(ln_b), rw, rb)


def _swap_vreg_sublane(vs, order=(4, 2, 1)):
    sub = lax.broadcasted_iota(jnp.int32, vs[0].shape, 1)
    vs = list(vs)
    for d in order:
        keep = (sub & d) == 0
        nxt = list(vs)
        for j in range(SUBLANES):
            if j & d:
                continue
            a, b = vs[j], vs[j + d]
            nxt[j] = jnp.where(keep, a, pltpu.roll(b, d, 1))
            nxt[j + d] = jnp.where(keep, pltpu.roll(a, SUBLANES - d, 1), b)
        vs = nxt
    return vs


def _dispatch_kernel(starts_ref, cnt_ref, pos_ref, xv_ref, xs_hbm, zero_ref, sem, zsem):
    td = xv_ref.shape[0]

    @pl.when(pl.program_id(0) == 0)
    def _():
        zero_ref[...] = jnp.zeros_like(zero_ref)

        def per_expert(e, carry):
            c0 = cnt_ref[e]
            c1 = jnp.bitwise_and(c0 + (MOE_TM - 1), -MOE_TM)
            base = starts_ref[e]

            def start(r, c):
                pltpu.make_async_copy(zero_ref.at[0], xs_hbm.at[base + r], zsem).start()
                return c

            def wait(r, c):
                pltpu.make_async_copy(zero_ref.at[0], xs_hbm.at[base + r], zsem).wait()
                return c

            lax.fori_loop(c0, c1, start, 0)
            lax.fori_loop(c0, c1, wait, 0)
            return carry

        lax.fori_loop(0, N_EXPERTS, per_expert, 0)

        last = N_EXPERTS - 1
        used_rows = starts_ref[last] + jnp.bitwise_and(cnt_ref[last] + (MOE_TM - 1), -MOE_TM)
        first_free = used_rows // MOE_TM
        n_tiles = xs_hbm.shape[0] // MOE_TM

        def tail_copy(tile):
            dst = xs_hbm.at[pl.ds(pl.multiple_of(tile * MOE_TM, MOE_TM), MOE_TM)]
            return pltpu.make_async_copy(zero_ref, dst, zsem)

        def tail_start(tile, c):
            tail_copy(tile).start()
            return c

        def tail_wait(tile, c):
            tail_copy(tile).wait()
            return c

        lax.fori_loop(first_free, n_tiles, tail_start, 0)
        lax.fori_loop(first_free, n_tiles, tail_wait, 0)

    def body(t, carry):
        for k in range(TOP_K):
            pltpu.make_async_copy(xv_ref.at[t], xs_hbm.at[pos_ref[t * TOP_K + k]], sem).start(priority=k % 2)
        return carry

    lax.fori_loop(0, td, body, 0)
    pltpu.make_async_copy(xs_hbm.at[pl.ds(0, td * TOP_K)], xs_hbm.at[pl.ds(0, td * TOP_K)], sem).wait()


def _dispatch(starts, counts, pos_flat, x_slab3, rows):
    n = x_slab3.shape[0]
    td = min(1024, n)
    grid_spec = pltpu.PrefetchScalarGridSpec(
        num_scalar_prefetch=2,
        grid=(n // td,),
        in_specs=[
            pl.BlockSpec((td * TOP_K,), lambda i, st, ct: (i,), memory_space=pltpu.SMEM),
            pl.BlockSpec((td, SLAB, LANES), lambda i, st, ct: (i, 0, 0)),
        ],
        out_specs=pl.BlockSpec(memory_space=pl.ANY),
        scratch_shapes=[
            pltpu.VMEM((MOE_TM, SLAB, LANES), F32),
            pltpu.SemaphoreType.DMA(()),
            pltpu.SemaphoreType.DMA(()),
        ],
    )
    return pl.pallas_call(
        _dispatch_kernel,
        grid_spec=grid_spec,
        out_shape=jax.ShapeDtypeStruct((rows, SLAB, LANES), F32),
        compiler_params=pltpu.CompilerParams(dimension_semantics=("arbitrary",),
                                             has_side_effects=True),
        name="moe_dispatch",
    )(starts, counts, pos_flat, x_slab3)


def _moe_kernel(layer, te_ref, first_ref, slot_ref, nxt_ref, nu_ref,
                x_ref, b1a_ref, b2a_ref, b1b_ref, b2b_ref, w1_hbm, w2_hbm, o_ref,
                w1s_ref, w2s_ref, w1p_ref, w2b_ref, sem):
    tm = MOE_TM
    g = tm // SUBLANES
    pw = 2 * LANES

    def weight_copies(expert, slot):
        return (pltpu.make_async_copy(w1_hbm.at[layer, expert], w1s_ref.at[slot], sem.at[0, slot]),
                pltpu.make_async_copy(w2_hbm.at[layer, expert], w2s_ref.at[slot], sem.at[1, slot]))

    @pl.when(pl.program_id(0) == 0)
    def _():
        for c in weight_copies(te_ref[0], 0):
            c.start()

    def tile(sub, b1_ref, b2_ref):
        i = pl.program_id(0) * MOE_TILES_PER_STEP + sub
        row0 = sub * tm
        used = i < nu_ref[0]
        e = te_ref[i]

        @pl.when(used & (first_ref[i] == 1))
        def _():
            slot = slot_ref[i]
            nxt = nxt_ref[i]

            @pl.when(nxt >= 0)
            def _():
                for c in weight_copies(nxt, 1 - slot):
                    c.start()

            for c in weight_copies(e, slot):
                c.wait()
            r = lax.broadcasted_iota(jnp.int32, (pw, pw), 0)
            cidx = lax.broadcasted_iota(jnp.int32, (pw, pw), 1)
            src = jnp.where(cidx < LANES, 2 * cidx, 2 * (cidx - LANES) + 1)
            perm = (r == src).astype(BF16)
            for blk in range(2 * D_EXPERT // pw):
                wb = w1s_ref[slot, :, blk * pw:(blk + 1) * pw].astype(BF16)
                w1p_ref[:, blk * pw:(blk + 1) * pw] = _dot(wb, perm).astype(BF16)
            w2b_ref[...] = w2s_ref[slot].astype(BF16)

        @pl.when(used)
        def _():
            slabs = [x_ref[pl.ds(row0 + j, g, stride=SUBLANES)] for j in range(SUBLANES)]
            chunks = _swap_vreg_sublane(slabs)
            x = jnp.concatenate([c.reshape(tm, LANES) for c in chunks], axis=1).astype(BF16)
            acts = []
            for blk in range(2 * D_EXPERT // pw):
                cols = slice(blk * pw, (blk + 1) * pw)
                h = _dot(x, w1p_ref[:, cols]) + b1_ref[0, 0, :, cols]
                hg = jnp.minimum(h[:, :LANES], SWIGLU_LIMIT)
                hl = jnp.clip(h[:, LANES:], -SWIGLU_LIMIT, SWIGLU_LIMIT)
                acts.append((hg * _sigmoid(SWIGLU_ALPHA * hg) * (hl + 1.0)).astype(BF16))
            act = jnp.concatenate(acts, axis=1)
            ys = []
            for nb in range(D_MODEL // pw):
                cols = slice(nb * pw, (nb + 1) * pw)
                y = _dot(act, w2b_ref[:, cols]) + b2_ref[0, 0, :, cols]
                ys += [y[:, s * LANES:(s + 1) * LANES].reshape(g, SUBLANES, LANES) for s in range(pw // LANES)]
            out_slabs = _swap_vreg_sublane(ys, order=(1, 2, 4))
            for j in range(SUBLANES):
                o_ref[pl.ds(row0 + j, g, stride=SUBLANES)] = out_slabs[j]

        @pl.when(jnp.logical_not(used))
        def _():
            o_ref[pl.ds(row0, tm)] = jnp.zeros((tm, SLAB, LANES), F32)

    tile(0, b1a_ref, b2a_ref)
    tile(1, b1b_ref, b2b_ref)


def _moe_mlp(layer, meta, xs3, w1, b1p, w2, b2, n_tiles):
    tps = MOE_TILES_PER_STEP
    tm = MOE_TM * tps
    tile_expert, first, slot, nxt, n_used = meta

    def x_map(i, te, fi, sl, nx, nu):
        return (jnp.minimum(i, (nu[0] - 1) // tps), 0, 0)

    def bias_map(sub):
        return lambda i, te, fi, sl, nx, nu: (layer, te[i * tps + sub], 0, 0)

    grid_spec = pltpu.PrefetchScalarGridSpec(
        num_scalar_prefetch=5,
        grid=(n_tiles // tps,),
        in_specs=[
            pl.BlockSpec((tm, SLAB, LANES), x_map),
            pl.BlockSpec((1, 1, 1, 2 * D_EXPERT), bias_map(0)),
            pl.BlockSpec((1, 1, 1, D_MODEL), bias_map(0)),
            pl.BlockSpec((1, 1, 1, 2 * D_EXPERT), bias_map(1)),
            pl.BlockSpec((1, 1, 1, D_MODEL), bias_map(1)),
            pl.BlockSpec(memory_space=pl.ANY),
            pl.BlockSpec(memory_space=pl.ANY),
        ],
        out_specs=pl.BlockSpec((tm, SLAB, LANES), lambda i, te, fi, sl, nx, nu: (i, 0, 0)),
        scratch_shapes=[
            pltpu.VMEM((2, D_MODEL, 2 * D_EXPERT), F32),
            pltpu.VMEM((2, D_EXPERT, D_MODEL), F32),
            pltpu.VMEM((D_MODEL, 2 * D_EXPERT), BF16),
            pltpu.VMEM((D_EXPERT, D_MODEL), BF16),
            pltpu.SemaphoreType.DMA((2, 2)),
        ],
    )
    return pl.pallas_call(
        functools.partial(_moe_kernel, layer),
        grid_spec=grid_spec,
        out_shape=jax.ShapeDtypeStruct((n_tiles * MOE_TM, SLAB, LANES), F32),
        compiler_params=pltpu.CompilerParams(dimension_semantics=("arbitrary",),
                                             vmem_limit_bytes=VMEM_LIMIT),
        name="moe_grouped_mlp",
    )(tile_expert, first, slot, nxt, n_used, xs3, b1p, b2, b1p, b2, w1, w2)


def _combine_kernel(pos_ref, pos_next_ref, gate_ref, os_hbm, x1_ref, g_ref, b_ref, o_ref,
                    buf_ref, mix_ref, sem):
    tc = x1_ref.shape[0]
    g = tc // SUBLANES
    i = pl.program_id(0)
    slot = i % 2

    def issue(p_ref, sl):
        def body(t, carry):
            for k in range(TOP_K):
                a = t * TOP_K + k
                pltpu.make_async_copy(os_hbm.at[p_ref[a]], buf_ref.at[sl, a], sem.at[sl]).start(priority=k % 2)
            return carry

        lax.fori_loop(0, tc, body, 0)

    @pl.when(i == 0)
    def _():
        issue(pos_ref, 0)

    @pl.when(i + 1 < pl.num_programs(0))
    def _():
        issue(pos_next_ref, 1 - slot)

    pltpu.make_async_copy(buf_ref.at[slot], buf_ref.at[slot], sem.at[slot]).wait()

    def mix_group(grp, carry):
        for j in range(SUBLANES):
            t = grp * SUBLANES + j
            a0 = t * TOP_K
            acc = gate_ref[a0] * buf_ref[slot, a0]
            for k in range(1, TOP_K):
                acc = acc + gate_ref[a0 + k] * buf_ref[slot, a0 + k]
            mix_ref[t] = acc
        return carry

    lax.fori_loop(0, g, mix_group, 0)
    slabs = [mix_ref[pl.ds(j, g, stride=SUBLANES)] for j in range(SUBLANES)]
    chunks = _swap_vreg_sublane(slabs)
    f = jnp.concatenate([c.reshape(tc, LANES) for c in chunks], axis=1)
    o_ref[...] = _layer_norm(DEEPNORM_ALPHA * x1_ref[...] + f, g_ref[...], b_ref[...])


def _combine(pos_flat, out_sorted3, gate, x1, ln_g, ln_b):
    n = x1.shape[0]
    tc = 512
    steps = n // tc
    vec = lambda v: v.astype(F32).reshape(1, D_MODEL)
    return pl.pallas_call(
        _combine_kernel,
        grid=(steps,),
        in_specs=[
            pl.BlockSpec((tc * TOP_K,), lambda i: (i,), memory_space=pltpu.SMEM),
            pl.BlockSpec((tc * TOP_K,), lambda i: (jnp.minimum(i + 1, steps - 1),), memory_space=pltpu.SMEM),
            pl.BlockSpec((tc * TOP_K,), lambda i: (i,), memory_space=pltpu.SMEM),
            pl.BlockSpec(memory_space=pl.ANY),
            pl.BlockSpec((tc, D_MODEL), lambda i: (i, 0)),
            pl.BlockSpec((1, D_MODEL), lambda i: (0, 0)),
            pl.BlockSpec((1, D_MODEL), lambda i: (0, 0)),
        ],
        out_specs=pl.BlockSpec((tc, D_MODEL), lambda i: (i, 0)),
        out_shape=jax.ShapeDtypeStruct((n, D_MODEL), F32),
        scratch_shapes=[
            pltpu.VMEM((2, tc * TOP_K, SLAB, LANES), F32),
            pltpu.VMEM((tc, SLAB, LANES), F32),
            pltpu.SemaphoreType.DMA((2,)),
        ],
        compiler_params=pltpu.CompilerParams(dimension_semantics=("arbitrary",),
                                             vmem_limit_bytes=VMEM_LIMIT),
        name="moe_combine_ln",
    )(pos_flat, pos_flat, gate.reshape(-1), out_sorted3, x1, vec(ln_g), vec(ln_b))


def _group_metadata(counts, n_tiles):
    tm = MOE_TM
    experts = jnp.arange(N_EXPERTS, dtype=jnp.int32)
    cnt = counts[0, :N_EXPERTS]
    tiles_per = (cnt + tm - 1) // tm
    tile_end = jnp.cumsum(tiles_per)
    starts = ((tile_end - tiles_per) * tm).astype(jnp.int32)
    n_used = tile_end[-1:].astype(jnp.int32)
    active = tiles_per > 0
    te_last = jnp.max(jnp.where(active, experts, 0))
    tidx = jnp.arange(n_tiles, dtype=jnp.int32)
    te = jnp.sum((tidx[:, None] >= tile_end[None, :]).astype(jnp.int32), axis=1)
    is_used = tidx < n_used[0]
    tile_expert = jnp.where(is_used, jnp.minimum(te, N_EXPERTS - 1), te_last).astype(jnp.int32)
    prev = jnp.concatenate([tile_expert[:1] - 1, tile_expert[:-1]])
    first = (is_used & (tile_expert != prev)).astype(jnp.int32)
    slot_e = ((jnp.cumsum(active.astype(jnp.int32)) - 1) % 2).astype(jnp.int32)
    later = jnp.where(active, experts, N_EXPERTS)
    suffix_min = lax.cummin(later[::-1])[::-1]
    nxt_e = jnp.concatenate([suffix_min[1:], jnp.full((1,), N_EXPERTS, jnp.int32)])
    nxt_e = jnp.where(nxt_e >= N_EXPERTS, -1, nxt_e).astype(jnp.int32)
    onehot = tile_expert[:, None] == experts[None, :]
    slot = jnp.sum(jnp.where(onehot, slot_e[None, :], 0), axis=1).astype(jnp.int32)
    nxt = jnp.sum(jnp.where(onehot, nxt_e[None, :], 0), axis=1).astype(jnp.int32)
    return starts, cnt.astype(jnp.int32), (tile_expert, first, slot, nxt, n_used)


def _moe_layer(layer, routed, w1, b1p, w2, b2, ln_g, ln_b):
    x1, x1_slab, ti, gate, rank, counts = routed
    n = x1.shape[0]
    n_tiles = n * TOP_K // MOE_TM + N_EXPERTS
    rows = n_tiles * MOE_TM
    starts, cnt, meta = _group_metadata(counts, n_tiles)
    assert n_tiles % MOE_TILES_PER_STEP == 0
    experts = jnp.arange(N_EXPERTS, dtype=jnp.int32)
    start_of = jnp.sum(jnp.where(ti[:, :, None] == experts[None, None, :], starts[None, None, :], 0), axis=-1)
    pos = (start_of + rank).reshape(-1).astype(jnp.int32)
    xs = _dispatch(starts, cnt, pos, x1_slab.reshape(n, SLAB, LANES), rows)
    out_sorted = _moe_mlp(layer, meta, xs, w1, b1p, w2, b2, n_tiles)
    return _combine(pos, out_sorted, gate, x1, ln_g, ln_b)


def kernel(x, positions, mix_w_in, mix_b_in, dn_conv_w, dn_a_log, dn_dt_bias, dn_norm_w, swa_sinks,
           mix_w_out, mix_b_out, pool_w, pool_b, pool_scale, ln1_g, ln1_b, router_w, router_b,
           moe_w1, moe_b1, moe_w2, moe_b2, ln2_g, ln2_b):
    batch, t, d = x.shape
    n = batch * t
    x2d = x.reshape(n, d)
    pos2d = positions.reshape(n, 1).astype(jnp.int32)
    pw = 2 * LANES
    b1p = moe_b1.astype(F32).reshape(DEPTH, N_EXPERTS, 2 * D_EXPERT // pw, LANES, 2)
    b1p = jnp.swapaxes(b1p, 3, 4).reshape(DEPTH, N_EXPERTS, 1, 2 * D_EXPERT)
    b2r = moe_b2.astype(F32).reshape(DEPTH, N_EXPERTS, 1, D_MODEL)
    for layer in range(DEPTH):
        i = layer // 2
        if layer % 2 == 0:
            q, kv, dq, dk, dv, dz, ab = _inproj(x2d, pos2d, mix_w_in[i], mix_b_in[i])
            a_out = _swa(q, kv, swa_sinks[i], batch, t)
            dn_out = _deltanet(dq, dk, dv, dz, ab, dn_conv_w[i], dn_a_log[i], dn_dt_bias[i],
                               dn_norm_w[i], batch, t)
            routed = _post(a_out, dn_out, x2d, mix_w_out[i], mix_b_out[i], ln1_g[layer], ln1_b[layer],
                           router_w[layer], router_b[layer])
        else:
            routed = _pool(x2d, pool_w[i], pool_b[i], pool_scale[i], ln1_g[layer], ln1_b[layer],
                           router_w[layer], router_b[layer], batch, t)
        x2d = _moe_layer(layer, routed, moe_w1, b1p, moe_w2, b2r, ln2_g[layer], ln2_b[layer])
    return x2d.reshape(batch, t, d)
```

```python
import functools
import math

import numpy as np
import jax
import jax.numpy as jnp
from jax import lax
from jax.experimental import pallas as pl
from jax.experimental.pallas import tpu as pltpu

D_MODEL = 1024
DEPTH = 2
SWA_Q_HEADS = 8
SWA_KV_HEADS = 2
SWA_HEAD_DIM = 64
SWA_BLOCK = 128
ROPE_THETA = 500000.0
ROPE_DIM = SWA_HEAD_DIM // 4
DN_HEADS = 4
DN_HEAD_DIM = 128
DN_CONV = 4
POOL_WINDOWS = (2, 4, 8, 16)
POOL_GROUP = D_MODEL // 4
N_EXPERTS = 32
TOP_K = 4
D_EXPERT = D_MODEL
SWIGLU_LIMIT = 7.0
SWIGLU_ALPHA = 1.702
LN_EPS = 1e-5
RMS_EPS = 1e-6
DEEPNORM_ALPHA = (2 * DEPTH) ** 0.25
SWA_Q_WIDTH = SWA_Q_HEADS * SWA_HEAD_DIM
SWA_KV_WIDTH = SWA_KV_HEADS * SWA_HEAD_DIM
DN_WIDTH = DN_HEADS * DN_HEAD_DIM

LANES = 128
SUBLANES = 8
SLAB = D_MODEL // LANES
IN_PAD_WIDTH = 3328
VMEM_LIMIT = 56 * 1024 * 1024

F32 = jnp.float32
BF16 = jnp.bfloat16
HIGHEST = lax.Precision.HIGHEST

ROUTER_TM = 512
MOE_TM = 256
MOE_TILES_PER_STEP = 2
DN_CHUNK = 128
POOL_HALO = SUBLANES * len(POOL_WINDOWS)


def _sigmoid(x):
    return 0.5 + 0.5 * jnp.tanh(0.5 * x)


def _layer_norm(z, g, b):
    mu = jnp.mean(z, axis=-1, keepdims=True)
    zc = z - mu
    var = jnp.mean(zc * zc, axis=-1, keepdims=True)
    return zc * lax.rsqrt(var + LN_EPS) * g + b


def _dot(a, b):
    return jnp.dot(a, b, preferred_element_type=F32)


def _dot_nt(a, b):
    return lax.dot_general(a, b, (((1,), (1,)), ((), ())), preferred_element_type=F32)


def _route_tail(x1, rw_ref, rb_ref, carry_ref, x1_ref, slab_ref, ti_ref, gate_ref, rank_ref, cnt_ref):
    tm = x1.shape[0]
    x1_ref[...] = x1
    for s in range(SLAB):
        slab_ref[pl.ds(s, tm, stride=SLAB), :] = x1[:, s * LANES:(s + 1) * LANES]

    xh = x1.astype(BF16)
    xl = (x1 - xh.astype(F32)).astype(BF16)
    logits = _dot(xh, rw_ref[0]) + (_dot(xh, rw_ref[1]) + _dot(xl, rw_ref[0])) + rb_ref[...]
    lane = lax.broadcasted_iota(jnp.int32, (tm, LANES), 1)
    lane_f = lane.astype(F32)
    l = jnp.where(lane < N_EXPERTS, logits, -jnp.inf)
    vals, hits = [], []
    ti_out = jnp.zeros((tm, LANES), jnp.int32)
    for k in range(TOP_K):
        m = jnp.max(l, axis=-1, keepdims=True)
        idx = jnp.min(jnp.where(l == m, lane_f, float(LANES)), axis=-1, keepdims=True)
        hit = lane_f == idx
        l = jnp.where(hit, -jnp.inf, l)
        vals.append(m)
        hits.append(hit)
        ti_out = jnp.where(lane == k, idx.astype(jnp.int32), ti_out)
    exps = [jnp.exp(v - vals[0]) for v in vals]
    den = exps[0] + exps[1] + exps[2] + exps[3]
    gate_out = jnp.zeros((tm, LANES), F32)
    for k in range(TOP_K):
        gate_out = jnp.where(lane == k, exps[k] / den, gate_out)

    sel = jnp.zeros((tm, LANES), F32)
    for k in range(TOP_K):
        sel = sel + hits[k].astype(F32)
    ri = lax.broadcasted_iota(jnp.int32, (tm, tm), 0)
    ci = lax.broadcasted_iota(jnp.int32, (tm, tm), 1)
    tri = (ri > ci).astype(BF16)
    carry = carry_ref[...]
    prefix = _dot(tri, sel.astype(BF16)) + carry
    rank_out = jnp.zeros((tm, LANES), jnp.int32)
    for k in range(TOP_K):
        r = jnp.sum(jnp.where(hits[k], prefix, 0.0), axis=-1, keepdims=True)
        rank_out = jnp.where(lane == k, r.astype(jnp.int32), rank_out)
    new_carry = carry + jnp.sum(sel, axis=0, keepdims=True)
    carry_ref[...] = new_carry
    cnt_ref[...] = new_carry.astype(jnp.int32)
    ti_ref[...] = ti_out[:, :TOP_K]
    gate_ref[...] = gate_out[:, :TOP_K]
    rank_ref[...] = rank_out[:, :TOP_K]


def _route_out_shapes(n):
    return (
        jax.ShapeDtypeStruct((n, D_MODEL), F32),
        jax.ShapeDtypeStruct((n * SLAB, LANES), F32),
        jax.ShapeDtypeStruct((n, TOP_K), jnp.int32),
        jax.ShapeDtypeStruct((n, TOP_K), F32),
        jax.ShapeDtypeStruct((n, TOP_K), jnp.int32),
        jax.ShapeDtypeStruct((1, LANES), jnp.int32),
    )


def _route_out_specs(tm, row_map):
    return (
        pl.BlockSpec((tm, D_MODEL), lambda *a: (row_map(*a), 0)),
        pl.BlockSpec((tm * SLAB, LANES), lambda *a: (row_map(*a), 0)),
        pl.BlockSpec((tm, TOP_K), lambda *a: (row_map(*a), 0)),
        pl.BlockSpec((tm, TOP_K), lambda *a: (row_map(*a), 0)),
        pl.BlockSpec((tm, TOP_K), lambda *a: (row_map(*a), 0)),
        pl.BlockSpec((1, LANES), lambda *a: (0, 0)),
    )


def _inproj_kernel(x_ref, pos_ref, invf_ref, w_ref, b_ref,
                   q_ref, kv_ref, dq_ref, dk_ref, dv_ref, dz_ref, ab_ref):
    x = x_ref[...].astype(BF16)
    proj = _dot(x, w_ref[...]) + b_ref[...]
    tm = x.shape[0]
    ang = pos_ref[...].astype(F32) * invf_ref[...]
    cos = jnp.cos(ang)
    sin = jnp.sin(ang)
    d = lax.broadcasted_iota(jnp.int32, (tm, LANES), 1) % SWA_HEAD_DIM
    half = ROPE_DIM // 2
    c_tab = jnp.where(d < ROPE_DIM, cos, 1.0)
    s_lo = jnp.where(d < half, -sin, 0.0)
    s_hi = jnp.where((d >= half) & (d < ROPE_DIM), sin, 0.0)

    def rot(xc):
        return (xc * c_tab + pltpu.roll(xc, LANES - half, 1) * s_lo
                + pltpu.roll(xc, half, 1) * s_hi)

    for c in range(SWA_Q_WIDTH // LANES):
        q_ref[:, c * LANES:(c + 1) * LANES] = rot(proj[:, c * LANES:(c + 1) * LANES])
    o = SWA_Q_WIDTH
    kv_ref[:, :LANES] = rot(proj[:, o:o + LANES])
    kv_ref[:, LANES:] = proj[:, o + LANES:o + 2 * LANES]
    o += 2 * SWA_KV_WIDTH
    dq_ref[...] = proj[:, o:o + DN_WIDTH]
    dk_ref[...] = proj[:, o + DN_WIDTH:o + 2 * DN_WIDTH]
    dv_ref[...] = proj[:, o + 2 * DN_WIDTH:o + 3 * DN_WIDTH]
    o += 3 * DN_WIDTH
    dz_ref[...] = proj[:, o:o + DN_WIDTH]
    o += DN_WIDTH
    ab_ref[...] = proj[:, o:o + LANES]


def _inproj(x2d, pos2d, w_in, b_in):
    n = x2d.shape[0]
    tm = 256
    in_width = w_in.shape[1]
    w = jnp.pad(w_in, ((0, 0), (0, IN_PAD_WIDTH - in_width))).astype(BF16)
    b = jnp.pad(b_in, (0, IN_PAD_WIDTH - in_width)).reshape(1, IN_PAD_WIDTH)
    half = ROPE_DIM // 2
    lane_d = np.arange(LANES) % SWA_HEAD_DIM
    invf = (ROPE_THETA ** (-(lane_d % half).astype(np.float64) / half)).astype(np.float32)
    invf = jnp.asarray(invf.reshape(1, LANES))
    row = lambda i: (i, 0)
    const = lambda i: (0, 0)
    widths = (SWA_Q_WIDTH, 2 * SWA_KV_WIDTH, DN_WIDTH, DN_WIDTH, DN_WIDTH, DN_WIDTH, LANES)
    return pl.pallas_call(
        _inproj_kernel,
        grid=(n // tm,),
        in_specs=[
            pl.BlockSpec((tm, D_MODEL), row),
            pl.BlockSpec((tm, 1), row),
            pl.BlockSpec((1, LANES), const),
            pl.BlockSpec((D_MODEL, IN_PAD_WIDTH), const),
            pl.BlockSpec((1, IN_PAD_WIDTH), const),
        ],
        out_specs=tuple(pl.BlockSpec((tm, wd), row) for wd in widths),
        out_shape=tuple(jax.ShapeDtypeStruct((n, wd), F32) for wd in widths),
        compiler_params=pltpu.CompilerParams(dimension_semantics=("arbitrary",),
                                             vmem_limit_bytes=VMEM_LIMIT),
        name="inproj_rotary",
    )(x2d, pos2d, invf, w, b)


def _swa_kernel(sink_ref, q_ref, kv_ref, o_ref):
    t = q_ref.shape[0]
    blk = SWA_BLOCK
    lane = lax.broadcasted_iota(jnp.int32, (2 * blk, LANES), 1)
    qi = lax.broadcasted_iota(jnp.int32, (blk, 2 * blk), 0)
    kj = lax.broadcasted_iota(jnp.int32, (blk, 2 * blk), 1)
    rel = qi + blk - kj
    in_window = (rel >= 0) & (rel < blk)
    is_cur = kj >= blk
    scale = SWA_HEAD_DIM ** -0.5

    group = SWA_Q_HEADS // SWA_KV_HEADS
    n_chunks = SWA_Q_WIDTH // LANES
    lane_q = lax.broadcasted_iota(jnp.int32, (blk, LANES), 1)
    blocks_per_iter = 2

    def body(it, carry):
        items = []
        for sub in range(blocks_per_iter):
            n = it * blocks_per_iter + sub
            r0 = pl.multiple_of(n * blk, blk)
            p0 = pl.multiple_of(jnp.maximum(n - 1, 0) * blk, blk)
            kvc = kv_ref[pl.ds(r0, blk), :]
            kvp = kv_ref[pl.ds(p0, blk), :]
            kband = jnp.concatenate([kvp[:, :LANES], kvc[:, :LANES]], axis=0)
            vband = jnp.concatenate([kvp[:, LANES:], kvc[:, LANES:]], axis=0)
            valid = in_window & (is_cur | (n > 0))
            k_rhs, v_rhs = [], []
            for h in range(SWA_KV_HEADS):
                in_head = (lane >= h * SWA_HEAD_DIM) & (lane < (h + 1) * SWA_HEAD_DIM)
                km = jnp.where(in_head, kband, 0.0)
                vm = jnp.where(in_head, vband, 0.0)
                kr = pltpu.roll(km, SWA_HEAD_DIM, 1)
                vr = pltpu.roll(vm, SWA_HEAD_DIM, 1)
                lo_k, hi_k = (km, kr) if h == 0 else (kr, km)
                lo_v, hi_v = (vm, vr) if h == 0 else (vr, vm)
                k_rhs.append(jnp.concatenate([lo_k, hi_k], axis=0).astype(BF16))
                v_rhs.append(jnp.concatenate([lo_v, hi_v], axis=0).astype(BF16))
            for c in range(n_chunks):
                items.append((r0, c, valid, k_rhs[(2 * c) // group], v_rhs[(2 * c) // group]))

        scores = [_dot_nt((q_ref[pl.ds(r0, blk), c * LANES:(c + 1) * LANES] * scale).astype(BF16), kr_)
                  for r0, c, _, kr_, _ in items]
        masked = [[jnp.where(valid, s[:, j * 2 * blk:(j + 1) * 2 * blk], -jnp.inf) for j in range(2)]
                  for s, (_, _, valid, _, _) in zip(scores, items)]
        maxes = [[jnp.maximum(jnp.max(s[j], axis=-1, keepdims=True), sink_ref[2 * c + j]) for j in range(2)]
                 for s, (_, c, _, _, _) in zip(masked, items)]
        probs = [[jnp.exp(s[j] - m[j]) for j in range(2)] for s, m in zip(masked, maxes)]
        dens = [[jnp.sum(p[j], axis=-1, keepdims=True) + jnp.exp(sink_ref[2 * c + j] - m[j]) for j in range(2)]
                for p, m, (_, c, _, _, _) in zip(probs, maxes, items)]
        for p, dn, (r0, c, _, _, vr_) in zip(probs, dens, items):
            o = _dot(jnp.concatenate([p[0].astype(BF16), p[1].astype(BF16)], axis=1), vr_)
            inv = jnp.where(lane_q < SWA_HEAD_DIM, 1.0 / dn[0], 1.0 / dn[1])
            o_ref[pl.ds(r0, blk), c * LANES:(c + 1) * LANES] = o * inv
        return carry

    lax.fori_loop(0, t // (blk * blocks_per_iter), body, 0)


def _swa(q, kv, sinks, batch, t):
    return pl.pallas_call(
        _swa_kernel,
        grid=(batch,),
        in_specs=[
            pl.BlockSpec(memory_space=pltpu.SMEM),
            pl.BlockSpec((t, SWA_Q_WIDTH), lambda b: (b, 0)),
            pl.BlockSpec((t, 2 * SWA_KV_WIDTH), lambda b: (b, 0)),
        ],
        out_specs=pl.BlockSpec((t, SWA_Q_WIDTH), lambda b: (b, 0)),
        out_shape=jax.ShapeDtypeStruct((batch * t, SWA_Q_WIDTH), F32),
        compiler_params=pltpu.CompilerParams(dimension_semantics=("arbitrary",),
                                             vmem_limit_bytes=VMEM_LIMIT),
        name="swa_attention",
    )(sinks.astype(F32), q, kv)


def _dn_kernel(convw_ref, hp_ref, normw_ref, sel_ref, ones_ref, dq_ref, dk_ref, dv_ref, dz_ref, ab_ref, o_ref,
               state_ref, halo_ref, pad_ref):
    tc = dq_ref.shape[0]
    c = DN_CHUNK
    halo = SUBLANES

    @pl.when(pl.program_id(1) == 0)
    def _():
        state_ref[...] = jnp.zeros_like(state_ref)
        halo_ref[...] = jnp.zeros_like(halo_ref)

    def conv_silu(x_ref, idx):
        pad_ref[0:halo, :] = halo_ref[idx]
        pad_ref[halo:, :] = x_ref[...]
        halo_ref[idx] = x_ref[tc - halo:tc, :]
        acc = jnp.zeros((tc, DN_WIDTH), F32)
        for j in range(DN_CONV):
            wj = convw_ref[j:j + 1, idx * DN_WIDTH:(idx + 1) * DN_WIDTH]
            acc = acc + wj * pad_ref[pl.ds(halo - DN_CONV + 1 + j, tc), :]
        return acc * _sigmoid(acc)

    q_all = conv_silu(dq_ref, 0)
    k_all = conv_silu(dk_ref, 1)
    v_all = conv_silu(dv_ref, 2)

    ab = ab_ref[...]
    neg_a = hp_ref[0:1, :]
    dt_b = hp_ref[1:2, :]
    sp_arg = ab + dt_b
    softplus = jnp.maximum(sp_arg, 0.0) + jnp.log(1.0 + jnp.exp(-jnp.abs(sp_arg)))
    g_all = neg_a * softplus
    beta_all = _sigmoid(ab)

    d = DN_HEAD_DIM
    ri = lax.broadcasted_iota(jnp.int32, (c, 2 * c), 0)
    ci = lax.broadcasted_iota(jnp.int32, (c, 2 * c), 1) % c
    lower_incl = ri >= ci
    lower_strict = ri > ci
    eye2 = (ri == ci).astype(F32)
    r1 = lax.broadcasted_iota(jnp.int32, (c, c), 0)
    c1 = lax.broadcasted_iota(jnp.int32, (c, c), 1)
    ltri = (r1 >= c1).astype(F32)
    ones_bd = ones_ref[...]

    def lane_bcast(x, first_lane):
        sel = sel_ref[first_lane // 2]
        hi = x.astype(BF16)
        lo = (x - hi.astype(F32)).astype(BF16)
        return _dot(hi, sel) + _dot(lo, sel)

    def blockdiag(x2):
        n = x2.shape[1] // 2
        z = jnp.zeros((x2.shape[0], n), BF16)
        return jnp.concatenate([jnp.concatenate([x2[:, :n], z], axis=1),
                                jnp.concatenate([z, x2[:, n:]], axis=1)], axis=0)

    n_chunks = tc // c
    items = [(ch, pr) for ch in range(n_chunks) for pr in range(DN_HEADS // 2)]

    gcs, gcts = [], []
    for ch in range(n_chunks):
        gc = jnp.dot(ltri, g_all[ch * c:(ch + 1) * c, :], precision=HIGHEST, preferred_element_type=F32)
        gcs.append(gc)
        gcts.append(gc.T)
    pre = []
    for ch, pr in items:
        rows = slice(ch * c, (ch + 1) * c)
        cols2 = slice(pr * 2 * d, (pr + 1) * 2 * d)
        gct = gcts[ch]
        g2 = lane_bcast(gcs[ch], 2 * pr)
        b2 = lane_bcast(beta_all[rows, :], DN_HEADS + 2 * pr)
        grow2 = jnp.concatenate([gct[2 * pr:2 * pr + 1, :], gct[2 * pr + 1:2 * pr + 2, :]], axis=1)
        gl2 = g2[c - 1:c, :]
        q2 = q_all[rows, cols2]
        k2 = k_all[rows, cols2]
        ss = _dot(jnp.concatenate([q2 * q2, k2 * k2], axis=0).astype(BF16), ones_bd)
        qn2 = q2 * lax.rsqrt(ss[:c] + RMS_EPS) * (d ** -0.5)
        kn2 = k2 * lax.rsqrt(ss[c:] + RMS_EPS)
        eg2 = jnp.exp(g2)
        kb2 = kn2 * b2
        decay2 = jnp.exp(jnp.where(lower_incl, g2 - grow2, -jnp.inf))
        lhs = jnp.concatenate([kb2, qn2], axis=0).astype(BF16)
        kn_t2 = jnp.concatenate([kn2[:, :d].T, kn2[:, d:].T], axis=1).astype(BF16)
        kk_qk = _dot(lhs, blockdiag(kn_t2))
        vb2 = (v_all[rows, cols2] * b2).astype(BF16)
        kbeg2 = (kb2 * eg2).astype(BF16)
        zero2 = jnp.zeros((c, 2 * d), BF16)
        rhs_bd = jnp.concatenate(
            [jnp.concatenate([vb2[:, :d], kbeg2[:, :d], zero2], axis=1),
             jnp.concatenate([zero2, vb2[:, d:], kbeg2[:, d:]], axis=1)], axis=0)
        kdec2 = kn2 * jnp.exp(gl2 - g2)
        pre.append(dict(
            a2=jnp.where(lower_strict, kk_qk[:c] * decay2, 0.0),
            qk2=(kk_qk[c:] * decay2).astype(BF16),
            rhs_bd=rhs_bd,
            qdec2=qn2 * eg2,
            kdec_t2=jnp.concatenate([kdec2[:, :d].T, kdec2[:, d:].T], axis=1).astype(BF16),
            glast2=jnp.exp(gl2),
        ))

    tinv = [eye2 - p["a2"] for p in pre]
    apow = [p["a2"] for p in pre]
    for _ in range(int(math.log2(c)) - 1):
        a16 = [a.astype(BF16) for a in apow]
        apow = [_dot(a, blockdiag(a)) for a in a16]
        tinv = [t + _dot(t.astype(BF16), blockdiag(a.astype(BF16))) for t, a in zip(tinv, apow)]

    for p, t in zip(pre, tinv):
        sol16 = _dot(t.astype(BF16), p["rhs_bd"]).astype(BF16)
        sol_bd = jnp.concatenate(
            [jnp.concatenate([sol16[:, :2 * d], jnp.zeros((c, 2 * d), BF16)], axis=1),
             jnp.concatenate([jnp.zeros((c, 2 * d), BF16), sol16[:, 2 * d:]], axis=1)], axis=0)
        ks = _dot(p["kdec_t2"], sol_bd)
        qs = _dot(p["qk2"], sol_bd)
        p["n2"] = jnp.concatenate([ks[:, :d], ks[:, 2 * d:3 * d]], axis=1)
        p["k2"] = jnp.concatenate([ks[:, d:2 * d], ks[:, 3 * d:]], axis=1).astype(BF16)
        p["o2"] = jnp.concatenate([qs[:, :d], qs[:, 2 * d:3 * d]], axis=1)
        p["q2"] = (p["qdec2"] - jnp.concatenate([qs[:, d:2 * d], qs[:, 3 * d:]], axis=1)).astype(BF16)

    normw2 = jnp.concatenate([normw_ref[...], normw_ref[...]], axis=1)
    for (ch, pr), p in zip(items, pre):
        rows = slice(ch * c, (ch + 1) * c)
        s2 = state_ref[pr]
        s_bd = blockdiag(s2.astype(BF16))
        o2 = _dot(p["q2"], s_bd) + p["o2"]
        state_ref[pr] = s2 * p["glast2"] - _dot(p["k2"], s_bd) + p["n2"]
        cols2 = slice(pr * 2 * d, (pr + 1) * 2 * d)
        ms = _dot((o2 * o2).astype(BF16), ones_bd) * (1.0 / d)
        z2 = dz_ref[rows, cols2]
        o_ref[rows, cols2] = o2 * lax.rsqrt(ms + RMS_EPS) * normw2 * (z2 * _sigmoid(z2))


def _deltanet(dq, dk, dv, dz, ab, conv_w, a_log, dt_bias, norm_w, batch, t):
    tc = 512
    hp = jnp.zeros((SUBLANES, LANES), F32)
    hp = hp.at[0, :DN_HEADS].set(-jnp.exp(a_log.astype(F32)))
    hp = hp.at[1, :DN_HEADS].set(dt_bias.astype(F32))
    steps = t // tc
    row = lambda b, s: (b * steps + s, 0)
    const = lambda b, s: (0, 0)
    d = DN_HEAD_DIM
    lane_head = np.arange(2 * d) // d
    sel = np.stack([(np.arange(LANES)[:, None] == 2 * idx + lane_head[None, :]) for idx in range(DN_HEADS)])
    ones_bd = lane_head[:, None] == lane_head[None, :]
    sel = jnp.asarray(sel.astype(np.float32), dtype=BF16)
    ones_bd = jnp.asarray(ones_bd.astype(np.float32), dtype=BF16)
    return pl.pallas_call(
        _dn_kernel,
        grid=(batch, steps),
        in_specs=[
            pl.BlockSpec((DN_CONV, 3 * DN_WIDTH), const),
            pl.BlockSpec((SUBLANES, LANES), const),
            pl.BlockSpec((1, DN_HEAD_DIM), const),
            pl.BlockSpec((DN_HEADS, LANES, 2 * DN_HEAD_DIM), lambda b, s: (0, 0, 0)),
            pl.BlockSpec((2 * DN_HEAD_DIM, 2 * DN_HEAD_DIM), const),
            pl.BlockSpec((tc, DN_WIDTH), row),
            pl.BlockSpec((tc, DN_WIDTH), row),
            pl.BlockSpec((tc, DN_WIDTH), row),
            pl.BlockSpec((tc, DN_WIDTH), row),
            pl.BlockSpec((tc, LANES), row),
        ],
        out_specs=pl.BlockSpec((tc, DN_WIDTH), row),
        out_shape=jax.ShapeDtypeStruct((batch * t, DN_WIDTH), F32),
        scratch_shapes=[
            pltpu.VMEM((DN_HEADS // 2, DN_HEAD_DIM, 2 * DN_HEAD_DIM), F32),
            pltpu.VMEM((3, SUBLANES, DN_WIDTH), F32),
            pltpu.VMEM((tc + SUBLANES, DN_WIDTH), F32),
        ],
        compiler_params=pltpu.CompilerParams(dimension_semantics=("arbitrary", "arbitrary"),
                                             vmem_limit_bytes=VMEM_LIMIT),
        name="gated_deltanet",
    )(conv_w.astype(F32), hp, norm_w.astype(F32).reshape(1, DN_HEAD_DIM), sel, ones_bd, dq, dk, dv, dz, ab)


def _post_kernel(a_ref, dn_ref, x_ref, wo_ref, bo_ref, g_ref, b_ref, rw_ref, rb_ref,
                 x1_ref, slab_ref, ti_ref, gate_ref, rank_ref, cnt_ref, carry_ref):
    @pl.when(pl.program_id(0) == 0)
    def _():
        carry_ref[...] = jnp.zeros_like(carry_ref)

    mix = jnp.concatenate([a_ref[...], dn_ref[...]], axis=1).astype(BF16)
    h = _dot(mix, wo_ref[...]) + bo_ref[...]
    x1 = _layer_norm(DEEPNORM_ALPHA * x_ref[...] + h, g_ref[...], b_ref[...])
    _route_tail(x1, rw_ref, rb_ref, carry_ref, x1_ref, slab_ref, ti_ref, gate_ref, rank_ref, cnt_ref)


def _router_params(router_w, router_b):
    rw = jnp.pad(router_w.astype(F32), ((0, 0), (0, LANES - N_EXPERTS)))
    rw_hi = rw.astype(BF16)
    rw_lo = (rw - rw_hi.astype(F32)).astype(BF16)
    rb = jnp.pad(router_b.astype(F32), (0, LANES - N_EXPERTS)).reshape(1, LANES)
    return jnp.stack([rw_hi, rw_lo]), rb


def _post(a_out, dn_out, x2d, w_out, b_out, ln_g, ln_b, router_w, router_b):
    n = x2d.shape[0]
    tm = ROUTER_TM
    rw, rb = _router_params(router_w, router_b)
    row = lambda i: (i, 0)
    const = lambda i: (0, 0)
    vec = lambda v: v.astype(F32).reshape(1, D_MODEL)
    return pl.pallas_call(
        _post_kernel,
        grid=(n // tm,),
        in_specs=[
            pl.BlockSpec((tm, SWA_Q_WIDTH), row),
            pl.BlockSpec((tm, DN_WIDTH), row),
            pl.BlockSpec((tm, D_MODEL), row),
            pl.BlockSpec((SWA_Q_WIDTH + DN_WIDTH, D_MODEL), const),
            pl.BlockSpec((1, D_MODEL), const),
            pl.BlockSpec((1, D_MODEL), const),
            pl.BlockSpec((1, D_MODEL), const),
            pl.BlockSpec((2, D_MODEL, LANES), lambda i: (0, 0, 0)),
            pl.BlockSpec((1, LANES), const),
        ],
        out_specs=_route_out_specs(tm, lambda i: i),
        out_shape=_route_out_shapes(n),
        scratch_shapes=[pltpu.VMEM((1, LANES), F32)],
        compiler_params=pltpu.CompilerParams(dimension_semantics=("arbitrary",),
                                             vmem_limit_bytes=VMEM_LIMIT),
        name="outproj_ln_router",
    )(a_out, dn_out, x2d, w_out.astype(BF16), vec(b_out), vec(ln_g), vec(ln_b), rw, rb)


def _pool_kernel(x_ref, pw_ref, pb_ref, ps_ref, g_ref, b_ref, rw_ref, rb_ref,
                 x1_ref, slab_ref, ti_ref, gate_ref, rank_ref, cnt_ref,
                 carry_ref, halo_ref, pad_ref):
    tm = x_ref.shape[0]
    halo = POOL_HALO
    b = pl.program_id(0)
    s = pl.program_id(1)

    @pl.when((b == 0) & (s == 0))
    def _():
        carry_ref[...] = jnp.zeros_like(carry_ref)

    @pl.when(s == 0)
    def _():
        halo_ref[...] = jnp.zeros_like(halo_ref)

    x = x_ref[...]
    pad_ref[0:halo, :] = halo_ref[...]
    pad_ref[halo:, :] = x
    halo_ref[...] = x[tm - halo:tm, :]
    tpos = s * tm + lax.broadcasted_iota(jnp.int32, (tm, 1), 0)
    outs = []
    for level, win in enumerate(POOL_WINDOWS, start=1):
        lo = SUBLANES * level
        rows = tm + halo - lo
        cols = slice((level - 1) * POOL_GROUP, D_MODEL)
        shift = win // 2
        pad_ref[pl.ds(lo, rows), cols] = (pad_ref[pl.ds(lo, rows), cols]
                                          + pad_ref[pl.ds(lo - shift, rows), cols])
        gcols = slice((level - 1) * POOL_GROUP, level * POOL_GROUP)
        cnt = jnp.minimum(tpos + 1, win).astype(F32)
        pooled = pad_ref[pl.ds(halo, tm), gcols] / cnt - x[:, gcols]
        y = _dot(pooled.astype(BF16), pw_ref[level - 1]) + pb_ref[level - 1:level, :]
        outs.append(y)
    h = jnp.concatenate(outs, axis=1) * ps_ref[...]
    x1 = _layer_norm(DEEPNORM_ALPHA * x + h, g_ref[...], b_ref[...])
    _route_tail(x1, rw_ref, rb_ref, carry_ref, x1_ref, slab_ref, ti_ref, gate_ref, rank_ref, cnt_ref)


def _pool(x2d, pool_w, pool_b, pool_scale, ln_g, ln_b, router_w, router_b, batch, t):
    n = x2d.shape[0]
    tm = ROUTER_TM
    steps = t // tm
    rw, rb = _router_params(router_w, router_b)
    row = lambda b, s: (b * steps + s, 0)
    const2 = lambda b, s: (0, 0)
    vec = lambda v: v.astype(F32).reshape(1, D_MODEL)
    ng = len(POOL_WINDOWS)
    return pl.pallas_call(
        _pool_kernel,
        grid=(batch, steps),
        in_specs=[
            pl.BlockSpec((tm, D_MODEL), row),
            pl.BlockSpec((ng, POOL_GROUP, POOL_GROUP), lambda b, s: (0, 0, 0)),
            pl.BlockSpec((ng, POOL_GROUP), const2),
            pl.BlockSpec((1, D_MODEL), const2),
            pl.BlockSpec((1, D_MODEL), const2),
            pl.BlockSpec((1, D_MODEL), const2),
            pl.BlockSpec((2, D_MODEL, LANES), lambda b, s: (0, 0, 0)),
            pl.BlockSpec((1, LANES), const2),
        ],
        out_specs=_route_out_specs(tm, lambda b, s: b * steps + s),
        out_shape=_route_out_shapes(n),
        scratch_shapes=[
            pltpu.VMEM((1, LANES), F32),
            pltpu.VMEM((POOL_HALO, D_MODEL), F32),
            pltpu.VMEM((tm + POOL_HALO, D_MODEL), F32),
        ],
        compiler_params=pltpu.CompilerParams(dimension_semantics=("arbitrary", "arbitrary"),
                                             vmem_limit_bytes=VMEM_LIMIT),
        name="pool_ln_router",
    )(x2d, pool_w.astype(BF16), pool_b.astype(F32), vec(pool_scale), vec(ln_g), vec(ln_b), rw, rb)


def _swap_vreg_sublane(vs, order=(4, 2, 1)):
    sub = lax.broadcasted_iota(jnp.int32, vs[0].shape, 1)
    vs = list(vs)
    for d in order:
        keep = (sub & d) == 0
        nxt = list(vs)
        for j in range(SUBLANES):
            if j & d:
                continue
            a, b = vs[j], vs[j + d]
            nxt[j] = jnp.where(keep, a, pltpu.roll(b, d, 1))
            nxt[j + d] = jnp.where(keep, pltpu.roll(a, SUBLANES - d, 1), b)
        vs = nxt
    return vs


def _dispatch_kernel(starts_ref, cnt_ref, pos_ref, xv_ref, xs_hbm, zero_ref, sem, zsem):
    td = xv_ref.shape[0]

    @pl.when(pl.program_id(0) == 0)
    def _():
        zero_ref[...] = jnp.zeros_like(zero_ref)

        def per_expert(e, carry):
            c0 = cnt_ref[e]
            c1 = jnp.bitwise_and(c0 + (MOE_TM - 1), -MOE_TM)
            base = starts_ref[e]

            def start(r, c):
                pltpu.make_async_copy(zero_ref.at[0], xs_hbm.at[base + r], zsem).start()
                return c

            def wait(r, c):
                pltpu.make_async_copy(zero_ref.at[0], xs_hbm.at[base + r], zsem).wait()
                return c

            lax.fori_loop(c0, c1, start, 0)
            lax.fori_loop(c0, c1, wait, 0)
            return carry

        lax.fori_loop(0, N_EXPERTS, per_expert, 0)

        last = N_EXPERTS - 1
        used_rows = starts_ref[last] + jnp.bitwise_and(cnt_ref[last] + (MOE_TM - 1), -MOE_TM)
        first_free = used_rows // MOE_TM
        n_tiles = xs_hbm.shape[0] // MOE_TM

        def tail_copy(tile):
            dst = xs_hbm.at[pl.ds(pl.multiple_of(tile * MOE_TM, MOE_TM), MOE_TM)]
            return pltpu.make_async_copy(zero_ref, dst, zsem)

        def tail_start(tile, c):
            tail_copy(tile).start()
            return c

        def tail_wait(tile, c):
            tail_copy(tile).wait()
            return c

        lax.fori_loop(first_free, n_tiles, tail_start, 0)
        lax.fori_loop(first_free, n_tiles, tail_wait, 0)

    def body(t, carry):
        for k in range(TOP_K):
            pltpu.make_async_copy(xv_ref.at[t], xs_hbm.at[pos_ref[t * TOP_K + k]], sem).start(priority=k % 2)
        return carry

    lax.fori_loop(0, td, body, 0)
    pltpu.make_async_copy(xs_hbm.at[pl.ds(0, td * TOP_K)], xs_hbm.at[pl.ds(0, td * TOP_K)], sem).wait()


def _dispatch(starts, counts, pos_flat, x_slab3, rows):
    n = x_slab3.shape[0]
    td = min(1024, n)
    grid_spec = pltpu.PrefetchScalarGridSpec(
        num_scalar_prefetch=2,
        grid=(n // td,),
        in_specs=[
            pl.BlockSpec((td * TOP_K,), lambda i, st, ct: (i,), memory_space=pltpu.SMEM),
            pl.BlockSpec((td, SLAB, LANES), lambda i, st, ct: (i, 0, 0)),
        ],
        out_specs=pl.BlockSpec(memory_space=pl.ANY),
        scratch_shapes=[
            pltpu.VMEM((MOE_TM, SLAB, LANES), F32),
            pltpu.SemaphoreType.DMA(()),
            pltpu.SemaphoreType.DMA(()),
        ],
    )
    return pl.pallas_call(
        _dispatch_kernel,
        grid_spec=grid_spec,
        out_shape=jax.ShapeDtypeStruct((rows, SLAB, LANES), F32),
        compiler_params=pltpu.CompilerParams(dimension_semantics=("arbitrary",),
                                             has_side_effects=True),
        name="moe_dispatch",
    )(starts, counts, pos_flat, x_slab3)


def _moe_kernel(layer, te_ref, first_ref, slot_ref, nxt_ref, nu_ref,
                x_ref, b1a_ref, b2a_ref, b1b_ref, b2b_ref, w1_hbm, w2_hbm, o_ref,
                w1s_ref, w2s_ref, w1p_ref, w2b_ref, sem):
    tm = MOE_TM
    g = tm // SUBLANES
    pw = 2 * LANES

    def weight_copies(expert, slot):
        return (pltpu.make_async_copy(w1_hbm.at[layer, expert], w1s_ref.at[slot], sem.at[0, slot]),
                pltpu.make_async_copy(w2_hbm.at[layer, expert], w2s_ref.at[slot], sem.at[1, slot]))

    @pl.when(pl.program_id(0) == 0)
    def _():
        for c in weight_copies(te_ref[0], 0):
            c.start()

    def tile(sub, b1_ref, b2_ref):
        i = pl.program_id(0) * MOE_TILES_PER_STEP + sub
        row0 = sub * tm
        used = i < nu_ref[0]
        e = te_ref[i]

        @pl.when(used & (first_ref[i] == 1))
        def _():
            slot = slot_ref[i]
            nxt = nxt_ref[i]

            @pl.when(nxt >= 0)
            def _():
                for c in weight_copies(nxt, 1 - slot):
                    c.start()

            for c in weight_copies(e, slot):
                c.wait()
            r = lax.broadcasted_iota(jnp.int32, (pw, pw), 0)
            cidx = lax.broadcasted_iota(jnp.int32, (pw, pw), 1)
            src = jnp.where(cidx < LANES, 2 * cidx, 2 * (cidx - LANES) + 1)
            perm = (r == src).astype(BF16)
            for blk in range(2 * D_EXPERT // pw):
                wb = w1s_ref[slot, :, blk * pw:(blk + 1) * pw].astype(BF16)
                w1p_ref[:, blk * pw:(blk + 1) * pw] = _dot(wb, perm).astype(BF16)
            w2b_ref[...] = w2s_ref[slot].astype(BF16)

        @pl.when(used)
        def _():
            slabs = [x_ref[pl.ds(row0 + j, g, stride=SUBLANES)] for j in range(SUBLANES)]
            chunks = _swap_vreg_sublane(slabs)
            x = jnp.concatenate([c.reshape(tm, LANES) for c in chunks], axis=1).astype(BF16)
            acts = []
            for blk in range(2 * D_EXPERT // pw):
                cols = slice(blk * pw, (blk + 1) * pw)
                h = _dot(x, w1p_ref[:, cols]) + b1_ref[0, 0, :, cols]
                hg = jnp.minimum(h[:, :LANES], SWIGLU_LIMIT)
                hl = jnp.clip(h[:, LANES:], -SWIGLU_LIMIT, SWIGLU_LIMIT)
                acts.append((hg * _sigmoid(SWIGLU_ALPHA * hg) * (hl + 1.0)).astype(BF16))
            act = jnp.concatenate(acts, axis=1)
            ys = []
            for nb in range(D_MODEL // pw):
                cols = slice(nb * pw, (nb + 1) * pw)
                y = _dot(act, w2b_ref[:, cols]) + b2_ref[0, 0, :, cols]
                ys += [y[:, s * LANES:(s + 1) * LANES].reshape(g, SUBLANES, LANES) for s in range(pw // LANES)]
            out_slabs = _swap_vreg_sublane(ys, order=(1, 2, 4))
            for j in range(SUBLANES):
                o_ref[pl.ds(row0 + j, g, stride=SUBLANES)] = out_slabs[j]

        @pl.when(jnp.logical_not(used))
        def _():
            o_ref[pl.ds(row0, tm)] = jnp.zeros((tm, SLAB, LANES), F32)

    tile(0, b1a_ref, b2a_ref)
    tile(1, b1b_ref, b2b_ref)


def _moe_mlp(layer, meta, xs3, w1, b1p, w2, b2, n_tiles):
    tps = MOE_TILES_PER_STEP
    tm = MOE_TM * tps
    tile_expert, first, slot, nxt, n_used = meta

    def x_map(i, te, fi, sl, nx, nu):
        return (jnp.minimum(i, (nu[0] - 1) // tps), 0, 0)

    def bias_map(sub):
        return lambda i, te, fi, sl, nx, nu: (layer, te[i * tps + sub], 0, 0)

    grid_spec = pltpu.PrefetchScalarGridSpec(
        num_scalar_prefetch=5,
        grid=(n_tiles // tps,),
        in_specs=[
            pl.BlockSpec((tm, SLAB, LANES), x_map),
            pl.BlockSpec((1, 1, 1, 2 * D_EXPERT), bias_map(0)),
            pl.BlockSpec((1, 1, 1, D_MODEL), bias_map(0)),
            pl.BlockSpec((1, 1, 1, 2 * D_EXPERT), bias_map(1)),
            pl.BlockSpec((1, 1, 1, D_MODEL), bias_map(1)),
            pl.BlockSpec(memory_space=pl.ANY),
            pl.BlockSpec(memory_space=pl.ANY),
        ],
        out_specs=pl.BlockSpec((tm, SLAB, LANES), lambda i, te, fi, sl, nx, nu: (i, 0, 0)),
        scratch_shapes=[
            pltpu.VMEM((2, D_MODEL, 2 * D_EXPERT), F32),
            pltpu.VMEM((2, D_EXPERT, D_MODEL), F32),
            pltpu.VMEM((D_MODEL, 2 * D_EXPERT), BF16),
            pltpu.VMEM((D_EXPERT, D_MODEL), BF16),
            pltpu.SemaphoreType.DMA((2, 2)),
        ],
    )
    return pl.pallas_call(
        functools.partial(_moe_kernel, layer),
        grid_spec=grid_spec,
        out_shape=jax.ShapeDtypeStruct((n_tiles * MOE_TM, SLAB, LANES), F32),
        compiler_params=pltpu.CompilerParams(dimension_semantics=("arbitrary",),
                                             vmem_limit_bytes=VMEM_LIMIT),
        name="moe_grouped_mlp",
    )(tile_expert, first, slot, nxt, n_used, xs3, b1p, b2, b1p, b2, w1, w2)


def _combine_kernel(pos_ref, pos_next_ref, gate_ref, os_hbm, x1_ref, g_ref, b_ref, o_ref,
                    buf_ref, mix_ref, sem):
    tc = x1_ref.shape[0]
    g = tc // SUBLANES
    i = pl.program_id(0)
    slot = i % 2

    def issue(p_ref, sl):
        def body(t, carry):
            for k in range(TOP_K):
                a = t * TOP_K + k
                pltpu.make_async_copy(os_hbm.at[p_ref[a]], buf_ref.at[sl, a], sem.at[sl]).start(priority=k % 2)
            return carry

        lax.fori_loop(0, tc, body, 0)

    @pl.when(i == 0)
    def _():
        issue(pos_ref, 0)

    @pl.when(i + 1 < pl.num_programs(0))
    def _():
        issue(pos_next_ref, 1 - slot)

    pltpu.make_async_copy(buf_ref.at[slot], buf_ref.at[slot], sem.at[slot]).wait()

    def mix_group(grp, carry):
        for j in range(SUBLANES):
            t = grp * SUBLANES + j
            a0 = t * TOP_K
            acc = gate_ref[a0] * buf_ref[slot, a0]
            for k in range(1, TOP_K):
                acc = acc + gate_ref[a0 + k] * buf_ref[slot, a0 + k]
            mix_ref[t] = acc
        return carry

    lax.fori_loop(0, g, mix_group, 0)
    slabs = [mix_ref[pl.ds(j, g, stride=SUBLANES)] for j in range(SUBLANES)]
    chunks = _swap_vreg_sublane(slabs)
    f = jnp.concatenate([c.reshape(tc, LANES) for c in chunks], axis=1)
    o_ref[...] = _layer_norm(DEEPNORM_ALPHA * x1_ref[...] + f, g_ref[...], b_ref[...])


def _combine(pos_flat, out_sorted3, gate, x1, ln_g, ln_b):
    n = x1.shape[0]
    tc = 512
    steps = n // tc
    vec = lambda v: v.astype(F32).reshape(1, D_MODEL)
    return pl.pallas_call(
        _combine_kernel,
        grid=(steps,),
        in_specs=[
            pl.BlockSpec((tc * TOP_K,), lambda i: (i,), memory_space=pltpu.SMEM),
            pl.BlockSpec((tc * TOP_K,), lambda i: (jnp.minimum(i + 1, steps - 1),), memory_space=pltpu.SMEM),
            pl.BlockSpec((tc * TOP_K,), lambda i: (i,), memory_space=pltpu.SMEM),
            pl.BlockSpec(memory_space=pl.ANY),
            pl.BlockSpec((tc, D_MODEL), lambda i: (i, 0)),
            pl.BlockSpec((1, D_MODEL), lambda i: (0, 0)),
            pl.BlockSpec((1, D_MODEL), lambda i: (0, 0)),
        ],
        out_specs=pl.BlockSpec((tc, D_MODEL), lambda i: (i, 0)),
        out_shape=jax.ShapeDtypeStruct((n, D_MODEL), F32),
        scratch_shapes=[
            pltpu.VMEM((2, tc * TOP_K, SLAB, LANES), F32),
            pltpu.VMEM((tc, SLAB, LANES), F32),
            pltpu.SemaphoreType.DMA((2,)),
        ],
        compiler_params=pltpu.CompilerParams(dimension_semantics=("arbitrary",),
                                             vmem_limit_bytes=VMEM_LIMIT),
        name="moe_combine_ln",
    )(pos_flat, pos_flat, gate.reshape(-1), out_sorted3, x1, vec(ln_g), vec(ln_b))


def _group_metadata(counts, n_tiles):
    tm = MOE_TM
    experts = jnp.arange(N_EXPERTS, dtype=jnp.int32)
    cnt = counts[0, :N_EXPERTS]
    tiles_per = (cnt + tm - 1) // tm
    tile_end = jnp.cumsum(tiles_per)
    starts = ((tile_end - tiles_per) * tm).astype(jnp.int32)
    n_used = tile_end[-1:].astype(jnp.int32)
    active = tiles_per > 0
    te_last = jnp.max(jnp.where(active, experts, 0))
    tidx = jnp.arange(n_tiles, dtype=jnp.int32)
    te = jnp.sum((tidx[:, None] >= tile_end[None, :]).astype(jnp.int32), axis=1)
    is_used = tidx < n_used[0]
    tile_expert = jnp.where(is_used, jnp.minimum(te, N_EXPERTS - 1), te_last).astype(jnp.int32)
    prev = jnp.concatenate([tile_expert[:1] - 1, tile_expert[:-1]])
    first = (is_used & (tile_expert != prev)).astype(jnp.int32)
    slot_e = ((jnp.cumsum(active.astype(jnp.int32)) - 1) % 2).astype(jnp.int32)
    later = jnp.where(active, experts, N_EXPERTS)
    suffix_min = lax.cummin(later[::-1])[::-1]
    nxt_e = jnp.concatenate([suffix_min[1:], jnp.full((1,), N_EXPERTS, jnp.int32)])
    nxt_e = jnp.where(nxt_e >= N_EXPERTS, -1, nxt_e).astype(jnp.int32)
    onehot = tile_expert[:, None] == experts[None, :]
    slot = jnp.sum(jnp.where(onehot, slot_e[None, :], 0), axis=1).astype(jnp.int32)
    nxt = jnp.sum(jnp.where(onehot, nxt_e[None, :], 0), axis=1).astype(jnp.int32)
    return starts, cnt.astype(jnp.int32), (tile_expert, first, slot, nxt, n_used)


def _moe_layer(layer, routed, w1, b1p, w2, b2, ln_g, ln_b):
    x1, x1_slab, ti, gate, rank, counts = routed
    n = x1.shape[0]
    n_tiles = n * TOP_K // MOE_TM + N_EXPERTS
    rows = n_tiles * MOE_TM
    starts, cnt, meta = _group_metadata(counts, n_tiles)
    assert n_tiles % MOE_TILES_PER_STEP == 0
    experts = jnp.arange(N_EXPERTS, dtype=jnp.int32)
    start_of = jnp.sum(jnp.where(ti[:, :, None] == experts[None, None, :], starts[None, None, :], 0), axis=-1)
    pos = (start_of + rank).reshape(-1).astype(jnp.int32)
    xs = _dispatch(starts, cnt, pos, x1_slab.reshape(n, SLAB, LANES), rows)
    out_sorted = _moe_mlp(layer, meta, xs, w1, b1p, w2, b2, n_tiles)
    return _combine(pos, out_sorted, gate, x1, ln_g, ln_b)


def kernel(x, positions, mix_w_in, mix_b_in, dn_conv_w, dn_a_log, dn_dt_bias, dn_norm_w, swa_sinks,
           mix_w_out, mix_b_out, pool_w, pool_b, pool_scale, ln1_g, ln1_b, router_w, router_b,
           moe_w1, moe_b1, moe_w2, moe_b2, ln2_g, ln2_b):
    batch, t, d = x.shape
    n = batch * t
    x2d = x.reshape(n, d)
    pos2d = positions.reshape(n, 1).astype(jnp.int32)
    pw = 2 * LANES
    b1p = moe_b1.astype(F32).reshape(DEPTH, N_EXPERTS, 2 * D_EXPERT // pw, LANES, 2)
    b1p = jnp.swapaxes(b1p, 3, 4).reshape(DEPTH, N_EXPERTS, 1, 2 * D_EXPERT)
    b2r = moe_b2.astype(F32).reshape(DEPTH, N_EXPERTS, 1, D_MODEL)
    for layer in range(DEPTH):
        i = layer // 2
        if layer % 2 == 0:
            q, kv, dq, dk, dv, dz, ab = _inproj(x2d, pos2d, mix_w_in[i], mix_b_in[i])
            a_out = _swa(q, kv, swa_sinks[i], batch, t)
            dn_out = _deltanet(dq, dk, dv, dz, ab, dn_conv_w[i], dn_a_log[i], dn_dt_bias[i],
                               dn_norm_w[i], batch, t)
            routed = _post(a_out, dn_out, x2d, mix_w_out[i], mix_b_out[i], ln1_g[layer], ln1_b[layer],
                           router_w[layer], router_b[layer])
        else:
            routed = _pool(x2d, pool_w[i], pool_b[i], pool_scale[i], ln1_g[layer], ln1_b[layer],
                           router_w[layer], router_b[layer], batch, t)
        x2d = _moe_layer(layer, routed, moe_w1, b1p, moe_w2, b2r, ln2_g[layer], ln2_b[layer])
    return x2d.reshape(batch, t, d)
```

```python
import functools
import math

import numpy as np
import jax
import jax.numpy as jnp
from jax import lax
from jax.experimental import pallas as pl
from jax.experimental.pallas import tpu as pltpu

D_MODEL = 1024
DEPTH = 2
SWA_Q_HEADS = 8
SWA_KV_HEADS = 2
SWA_HEAD_DIM = 64
SWA_BLOCK = 128
ROPE_THETA = 500000.0
ROPE_DIM = SWA_HEAD_DIM // 4
DN_HEADS = 4
DN_HEAD_DIM = 128
DN_CONV = 4
POOL_WINDOWS = (2, 4, 8, 16)
POOL_GROUP = D_MODEL // 4
N_EXPERTS = 32
TOP_K = 4
D_EXPERT = D_MODEL
SWIGLU_LIMIT = 7.0
SWIGLU_ALPHA = 1.702
LN_EPS = 1e-5
RMS_EPS = 1e-6
DEEPNORM_ALPHA = (2 * DEPTH) ** 0.25
SWA_Q_WIDTH = SWA_Q_HEADS * SWA_HEAD_DIM
SWA_KV_WIDTH = SWA_KV_HEADS * SWA_HEAD_DIM
DN_WIDTH = DN_HEADS * DN_HEAD_DIM

LANES = 128
SUBLANES = 8
SLAB = D_MODEL // LANES
IN_PAD_WIDTH = 3328
VMEM_LIMIT = 56 * 1024 * 1024

F32 = jnp.float32
BF16 = jnp.bfloat16
HIGHEST = lax.Precision.HIGHEST

ROUTER_TM = 512
MOE_TM = 256
MOE_TILES_PER_STEP = 2
DN_CHUNK = 128
POOL_HALO = SUBLANES * len(POOL_WINDOWS)


def _sigmoid(x):
    return 0.5 + 0.5 * jnp.tanh(0.5 * x)


def _layer_norm(z, g, b):
    mu = jnp.mean(z, axis=-1, keepdims=True)
    zc = z - mu
    var = jnp.mean(zc * zc, axis=-1, keepdims=True)
    return zc * lax.rsqrt(var + LN_EPS) * g + b


def _dot(a, b):
    return jnp.dot(a, b, preferred_element_type=F32)


def _dot_nt(a, b):
    return lax.dot_general(a, b, (((1,), (1,)), ((), ())), preferred_element_type=F32)


def _route_tail(x1, rw_ref, rb_ref, carry_ref, x1_ref, slab_ref, ti_ref, gate_ref, rank_ref, cnt_ref):
    tm = x1.shape[0]
    x1_ref[...] = x1
    for s in range(SLAB):
        slab_ref[pl.ds(s, tm, stride=SLAB), :] = x1[:, s * LANES:(s + 1) * LANES]

    xh = x1.astype(BF16)
    xl = (x1 - xh.astype(F32)).astype(BF16)
    logits = _dot(xh, rw_ref[0]) + (_dot(xh, rw_ref[1]) + _dot(xl, rw_ref[0])) + rb_ref[...]
    lane = lax.broadcasted_iota(jnp.int32, (tm, LANES), 1)
    lane_f = lane.astype(F32)
    l = jnp.where(lane < N_EXPERTS, logits, -jnp.inf)
    vals, hits = [], []
    ti_out = jnp.zeros((tm, LANES), F32)
    for k in range(TOP_K):
        m = jnp.max(l, axis=-1, keepdims=True)
        idx = jnp.min(jnp.where(l == m, lane_f, float(LANES)), axis=-1, keepdims=True)
        hit = lane_f == idx
        l = jnp.where(hit, -jnp.inf, l)
        vals.append(m)
        hits.append(hit)
        ti_out = jnp.where(lane == k, idx, ti_out)
    exps = [jnp.exp(v - vals[0]) for v in vals]
    den = exps[0] + exps[1] + exps[2] + exps[3]
    gate_out = jnp.zeros((tm, LANES), F32)
    for k in range(TOP_K):
        gate_out = jnp.where(lane == k, exps[k] / den, gate_out)

    sel = jnp.zeros((tm, LANES), F32)
    for k in range(TOP_K):
        sel = sel + hits[k].astype(F32)
    ri = lax.broadcasted_iota(jnp.int32, (tm, tm), 0)
    ci = lax.broadcasted_iota(jnp.int32, (tm, tm), 1)
    tri = (ri > ci).astype(BF16)
    carry = carry_ref[...]
    prefix = _dot(tri, sel.astype(BF16)) + carry
    rank_out = jnp.zeros((tm, LANES), F32)
    for k in range(TOP_K):
        r = jnp.sum(jnp.where(hits[k], prefix, 0.0), axis=-1, keepdims=True)
        rank_out = jnp.where(lane == k, r, rank_out)
    new_carry = carry + jnp.sum(sel, axis=0, keepdims=True)
    carry_ref[...] = new_carry
    cnt_ref[...] = new_carry.astype(jnp.int32)
    ti_ref[...] = ti_out.T[:SUBLANES, :].astype(jnp.int32)
    gate_ref[...] = gate_out.T[:SUBLANES, :]
    rank_ref[...] = rank_out.T[:SUBLANES, :].astype(jnp.int32)


def _route_out_shapes(n):
    return (
        jax.ShapeDtypeStruct((n, D_MODEL), F32),
        jax.ShapeDtypeStruct((n * SLAB, LANES), F32),
        jax.ShapeDtypeStruct((SUBLANES, n), jnp.int32),
        jax.ShapeDtypeStruct((SUBLANES, n), F32),
        jax.ShapeDtypeStruct((SUBLANES, n), jnp.int32),
        jax.ShapeDtypeStruct((1, LANES), jnp.int32),
    )


def _route_out_specs(tm, row_map):
    return (
        pl.BlockSpec((tm, D_MODEL), lambda *a: (row_map(*a), 0)),
        pl.BlockSpec((tm * SLAB, LANES), lambda *a: (row_map(*a), 0)),
        pl.BlockSpec((SUBLANES, tm), lambda *a: (0, row_map(*a))),
        pl.BlockSpec((SUBLANES, tm), lambda *a: (0, row_map(*a))),
        pl.BlockSpec((SUBLANES, tm), lambda *a: (0, row_map(*a))),
        pl.BlockSpec((1, LANES), lambda *a: (0, 0)),
    )


def _inproj_kernel(x_ref, pos_ref, invf_ref, w_ref, b_ref,
                   q_ref, kv_ref, dq_ref, dk_ref, dv_ref, dz_ref, ab_ref):
    x = x_ref[...].astype(BF16)
    proj = _dot(x, w_ref[...]) + b_ref[...]
    tm = x.shape[0]
    ang = pos_ref[...].astype(F32) * invf_ref[...]
    cos = jnp.cos(ang)
    sin = jnp.sin(ang)
    d = lax.broadcasted_iota(jnp.int32, (tm, LANES), 1) % SWA_HEAD_DIM
    half = ROPE_DIM // 2
    c_tab = jnp.where(d < ROPE_DIM, cos, 1.0)
    s_lo = jnp.where(d < half, -sin, 0.0)
    s_hi = jnp.where((d >= half) & (d < ROPE_DIM), sin, 0.0)

    def rot(xc):
        return (xc * c_tab + pltpu.roll(xc, LANES - half, 1) * s_lo
                + pltpu.roll(xc, half, 1) * s_hi)

    for c in range(SWA_Q_WIDTH // LANES):
        q_ref[:, c * LANES:(c + 1) * LANES] = rot(proj[:, c * LANES:(c + 1) * LANES])
    o = SWA_Q_WIDTH
    kv_ref[:, :LANES] = rot(proj[:, o:o + LANES])
    kv_ref[:, LANES:] = proj[:, o + LANES:o + 2 * LANES]
    o += 2 * SWA_KV_WIDTH
    dq_ref[...] = proj[:, o:o + DN_WIDTH]
    dk_ref[...] = proj[:, o + DN_WIDTH:o + 2 * DN_WIDTH]
    dv_ref[...] = proj[:, o + 2 * DN_WIDTH:o + 3 * DN_WIDTH]
    o += 3 * DN_WIDTH
    dz_ref[...] = proj[:, o:o + DN_WIDTH]
    o += DN_WIDTH
    ab_ref[...] = proj[:, o:o + LANES]


def _inproj(x2d, pos2d, w_in, b_in):
    n = x2d.shape[0]
    tm = 512
    in_width = w_in.shape[1]
    w = jnp.pad(w_in, ((0, 0), (0, IN_PAD_WIDTH - in_width))).astype(BF16)
    b = jnp.pad(b_in, (0, IN_PAD_WIDTH - in_width)).reshape(1, IN_PAD_WIDTH)
    half = ROPE_DIM // 2
    lane_d = np.arange(LANES) % SWA_HEAD_DIM
    invf = (ROPE_THETA ** (-(lane_d % half).astype(np.float64) / half)).astype(np.float32)
    invf = jnp.asarray(invf.reshape(1, LANES))
    row = lambda i: (i, 0)
    const = lambda i: (0, 0)
    widths = (SWA_Q_WIDTH, 2 * SWA_KV_WIDTH, DN_WIDTH, DN_WIDTH, DN_WIDTH, DN_WIDTH, LANES)
    return pl.pallas_call(
        _inproj_kernel,
        grid=(n // tm,),
        in_specs=[
            pl.BlockSpec((tm, D_MODEL), row),
            pl.BlockSpec((tm, 1), row),
            pl.BlockSpec((1, LANES), const),
            pl.BlockSpec((D_MODEL, IN_PAD_WIDTH), const),
            pl.BlockSpec((1, IN_PAD_WIDTH), const),
        ],
        out_specs=tuple(pl.BlockSpec((tm, wd), row) for wd in widths),
        out_shape=tuple(jax.ShapeDtypeStruct((n, wd), F32) for wd in widths),
        compiler_params=pltpu.CompilerParams(dimension_semantics=("arbitrary",),
                                             vmem_limit_bytes=VMEM_LIMIT),
        name="inproj_rotary",
    )(x2d, pos2d, invf, w, b)


def _swa_kernel(sink_ref, q_ref, kv_ref, o_ref):
    t = q_ref.shape[0]
    blk = SWA_BLOCK
    lane = lax.broadcasted_iota(jnp.int32, (2 * blk, LANES), 1)
    qi = lax.broadcasted_iota(jnp.int32, (blk, 2 * blk), 0)
    kj = lax.broadcasted_iota(jnp.int32, (blk, 2 * blk), 1)
    rel = qi + blk - kj
    in_window = (rel >= 0) & (rel < blk)
    is_cur = kj >= blk
    scale = SWA_HEAD_DIM ** -0.5

    group = SWA_Q_HEADS // SWA_KV_HEADS
    n_chunks = SWA_Q_WIDTH // LANES
    lane_q = lax.broadcasted_iota(jnp.int32, (blk, LANES), 1)
    blocks_per_iter = 2

    def body(it, carry):
        items = []
        for sub in range(blocks_per_iter):
            n = it * blocks_per_iter + sub
            r0 = pl.multiple_of(n * blk, blk)
            p0 = pl.multiple_of(jnp.maximum(n - 1, 0) * blk, blk)
            kvc = kv_ref[pl.ds(r0, blk), :]
            kvp = kv_ref[pl.ds(p0, blk), :]
            kband = jnp.concatenate([kvp[:, :LANES], kvc[:, :LANES]], axis=0)
            vband = jnp.concatenate([kvp[:, LANES:], kvc[:, LANES:]], axis=0)
            valid = in_window & (is_cur | (n > 0))
            k_rhs, v_rhs = [], []
            for h in range(SWA_KV_HEADS):
                in_head = (lane >= h * SWA_HEAD_DIM) & (lane < (h + 1) * SWA_HEAD_DIM)
                km = jnp.where(in_head, kband, 0.0)
                vm = jnp.where(in_head, vband, 0.0)
                kr = pltpu.roll(km, SWA_HEAD_DIM, 1)
                vr = pltpu.roll(vm, SWA_HEAD_DIM, 1)
                lo_k, hi_k = (km, kr) if h == 0 else (kr, km)
                lo_v, hi_v = (vm, vr) if h == 0 else (vr, vm)
                k_rhs.append(jnp.concatenate([lo_k, hi_k], axis=0).astype(BF16))
                v_rhs.append(jnp.concatenate([lo_v, hi_v], axis=0).astype(BF16))
            for c in range(n_chunks):
                items.append((r0, c, valid, k_rhs[(2 * c) // group], v_rhs[(2 * c) // group]))

        scores = [_dot_nt((q_ref[pl.ds(r0, blk), c * LANES:(c + 1) * LANES] * scale).astype(BF16), kr_)
                  for r0, c, _, kr_, _ in items]
        masked = [[jnp.where(valid, s[:, j * 2 * blk:(j + 1) * 2 * blk], -jnp.inf) for j in range(2)]
                  for s, (_, _, valid, _, _) in zip(scores, items)]
        maxes = [[jnp.maximum(jnp.max(s[j], axis=-1, keepdims=True), sink_ref[2 * c + j]) for j in range(2)]
                 for s, (_, c, _, _, _) in zip(masked, items)]
        probs = [[jnp.exp(s[j] - m[j]) for j in range(2)] for s, m in zip(masked, maxes)]
        dens = [[jnp.sum(p[j], axis=-1, keepdims=True) + jnp.exp(sink_ref[2 * c + j] - m[j]) for j in range(2)]
                for p, m, (_, c, _, _, _) in zip(probs, maxes, items)]
        for p, dn, (r0, c, _, _, vr_) in zip(probs, dens, items):
            o = _dot(jnp.concatenate([p[0].astype(BF16), p[1].astype(BF16)], axis=1), vr_)
            inv = jnp.where(lane_q < SWA_HEAD_DIM, 1.0 / dn[0], 1.0 / dn[1])
            o_ref[pl.ds(r0, blk), c * LANES:(c + 1) * LANES] = o * inv
        return carry

    lax.fori_loop(0, t // (blk * blocks_per_iter), body, 0)


def _swa(q, kv, sinks, batch, t):
    return pl.pallas_call(
        _swa_kernel,
        grid=(batch,),
        in_specs=[
            pl.BlockSpec(memory_space=pltpu.SMEM),
            pl.BlockSpec((t, SWA_Q_WIDTH), lambda b: (b, 0)),
            pl.BlockSpec((t, 2 * SWA_KV_WIDTH), lambda b: (b, 0)),
        ],
        out_specs=pl.BlockSpec((t, SWA_Q_WIDTH), lambda b: (b, 0)),
        out_shape=jax.ShapeDtypeStruct((batch * t, SWA_Q_WIDTH), F32),
        compiler_params=pltpu.CompilerParams(dimension_semantics=("arbitrary",),
                                             vmem_limit_bytes=VMEM_LIMIT),
        name="swa_attention",
    )(sinks.astype(F32), q, kv)


def _dn_kernel(convw_ref, hp_ref, normw_ref, sel_ref, ones_ref, dq_ref, dk_ref, dv_ref, dz_ref, ab_ref, o_ref,
               state_ref, halo_ref, pad_ref):
    tc = dq_ref.shape[0]
    c = DN_CHUNK
    halo = SUBLANES

    @pl.when(pl.program_id(1) == 0)
    def _():
        state_ref[...] = jnp.zeros_like(state_ref)
        halo_ref[...] = jnp.zeros_like(halo_ref)

    def conv_silu(x_ref, idx):
        pad_ref[0:halo, :] = halo_ref[idx]
        pad_ref[halo:, :] = x_ref[...]
        halo_ref[idx] = x_ref[tc - halo:tc, :]
        acc = jnp.zeros((tc, DN_WIDTH), F32)
        for j in range(DN_CONV):
            wj = convw_ref[j:j + 1, idx * DN_WIDTH:(idx + 1) * DN_WIDTH]
            acc = acc + wj * pad_ref[pl.ds(halo - DN_CONV + 1 + j, tc), :]
        return acc * _sigmoid(acc)

    q_all = conv_silu(dq_ref, 0)
    k_all = conv_silu(dk_ref, 1)
    v_all = conv_silu(dv_ref, 2)

    ab = ab_ref[...]
    neg_a = hp_ref[0:1, :]
    dt_b = hp_ref[1:2, :]
    sp_arg = ab + dt_b
    softplus = jnp.maximum(sp_arg, 0.0) + jnp.log(1.0 + jnp.exp(-jnp.abs(sp_arg)))
    g_all = neg_a * softplus
    beta_all = _sigmoid(ab)

    d = DN_HEAD_DIM
    ri = lax.broadcasted_iota(jnp.int32, (c, 2 * c), 0)
    ci = lax.broadcasted_iota(jnp.int32, (c, 2 * c), 1) % c
    lower_incl = ri >= ci
    lower_strict = ri > ci
    eye2 = (ri == ci).astype(F32)
    r1 = lax.broadcasted_iota(jnp.int32, (c, c), 0)
    c1 = lax.broadcasted_iota(jnp.int32, (c, c), 1)
    ltri = (r1 >= c1).astype(F32)
    ones_bd = ones_ref[...]

    def lane_bcast(x, first_lane):
        sel = sel_ref[first_lane // 2]
        hi = x.astype(BF16)
        lo = (x - hi.astype(F32)).astype(BF16)
        return _dot(hi, sel) + _dot(lo, sel)

    def blockdiag(x2):
        n = x2.shape[1] // 2
        z = jnp.zeros((x2.shape[0], n), BF16)
        return jnp.concatenate([jnp.concatenate([x2[:, :n], z], axis=1),
                                jnp.concatenate([z, x2[:, n:]], axis=1)], axis=0)

    n_chunks = tc // c
    items = [(ch, pr) for ch in range(n_chunks) for pr in range(DN_HEADS // 2)]

    gcs, gcts = [], []
    for ch in range(n_chunks):
        gc = jnp.dot(ltri, g_all[ch * c:(ch + 1) * c, :], precision=HIGHEST, preferred_element_type=F32)
        gcs.append(gc)
        gcts.append(gc.T)
    pre = []
    for ch, pr in items:
        rows = slice(ch * c, (ch + 1) * c)
        cols2 = slice(pr * 2 * d, (pr + 1) * 2 * d)
        gct = gcts[ch]
        g2 = lane_bcast(gcs[ch], 2 * pr)
        b2 = lane_bcast(beta_all[rows, :], DN_HEADS + 2 * pr)
        grow2 = jnp.concatenate([gct[2 * pr:2 * pr + 1, :], gct[2 * pr + 1:2 * pr + 2, :]], axis=1)
        gl2 = g2[c - 1:c, :]
        q2 = q_all[rows, cols2]
        k2 = k_all[rows, cols2]
        ss = _dot(jnp.concatenate([q2 * q2, k2 * k2], axis=0).astype(BF16), ones_bd)
        qn2 = q2 * lax.rsqrt(ss[:c] + RMS_EPS) * (d ** -0.5)
        kn2 = k2 * lax.rsqrt(ss[c:] + RMS_EPS)
        eg2 = jnp.exp(g2)
        kb2 = kn2 * b2
        decay2 = jnp.exp(jnp.where(lower_incl, g2 - grow2, -jnp.inf))
        lhs = jnp.concatenate([kb2, qn2], axis=0).astype(BF16)
        kn_t2 = jnp.concatenate([kn2[:, :d].T, kn2[:, d:].T], axis=1).astype(BF16)
        kk_qk = _dot(lhs, blockdiag(kn_t2))
        vb2 = (v_all[rows, cols2] * b2).astype(BF16)
        kbeg2 = (kb2 * eg2).astype(BF16)
        zero2 = jnp.zeros((c, 2 * d), BF16)
        rhs_bd = jnp.concatenate(
            [jnp.concatenate([vb2[:, :d], kbeg2[:, :d], zero2], axis=1),
             jnp.concatenate([zero2, vb2[:, d:], kbeg2[:, d:]], axis=1)], axis=0)
        kdec2 = kn2 * jnp.exp(gl2 - g2)
        pre.append(dict(
            a2=jnp.where(lower_strict, kk_qk[:c] * decay2, 0.0),
            qk2=(kk_qk[c:] * decay2).astype(BF16),
            rhs_bd=rhs_bd,
            qdec2=qn2 * eg2,
            kdec_t2=jnp.concatenate([kdec2[:, :d].T, kdec2[:, d:].T], axis=1).astype(BF16),
            glast2=jnp.exp(gl2),
        ))

    tinv = [eye2 - p["a2"] for p in pre]
    apow = [p["a2"] for p in pre]
    for _ in range(int(math.log2(c)) - 1):
        a16 = [a.astype(BF16) for a in apow]
        apow = [_dot(a, blockdiag(a)) for a in a16]
        tinv = [t + _dot(t.astype(BF16), blockdiag(a.astype(BF16))) for t, a in zip(tinv, apow)]

    for p, t in zip(pre, tinv):
        sol16 = _dot(t.astype(BF16), p["rhs_bd"]).astype(BF16)
        sol_bd = jnp.concatenate(
            [jnp.concatenate([sol16[:, :2 * d], jnp.zeros((c, 2 * d), BF16)], axis=1),
             jnp.concatenate([jnp.zeros((c, 2 * d), BF16), sol16[:, 2 * d:]], axis=1)], axis=0)
        ks = _dot(p["kdec_t2"], sol_bd)
        qs = _dot(p["qk2"], sol_bd)
        p["n2"] = jnp.concatenate([ks[:, :d], ks[:, 2 * d:3 * d]], axis=1)
        p["k2"] = jnp.concatenate([ks[:, d:2 * d], ks[:, 3 * d:]], axis=1).astype(BF16)
        p["o2"] = jnp.concatenate([qs[:, :d], qs[:, 2 * d:3 * d]], axis=1)
        p["q2"] = (p["qdec2"] - jnp.concatenate([qs[:, d:2 * d], qs[:, 3 * d:]], axis=1)).astype(BF16)

    normw2 = jnp.concatenate([normw_ref[...], normw_ref[...]], axis=1)
    for (ch, pr), p in zip(items, pre):
        rows = slice(ch * c, (ch + 1) * c)
        s2 = state_ref[pr]
        s_bd = blockdiag(s2.astype(BF16))
        o2 = _dot(p["q2"], s_bd) + p["o2"]
        state_ref[pr] = s2 * p["glast2"] - _dot(p["k2"], s_bd) + p["n2"]
        cols2 = slice(pr * 2 * d, (pr + 1) * 2 * d)
        ms = _dot((o2 * o2).astype(BF16), ones_bd) * (1.0 / d)
        z2 = dz_ref[rows, cols2]
        o_ref[rows, cols2] = o2 * lax.rsqrt(ms + RMS_EPS) * normw2 * (z2 * _sigmoid(z2))


def _deltanet(dq, dk, dv, dz, ab, conv_w, a_log, dt_bias, norm_w, batch, t):
    tc = 512
    hp = jnp.zeros((SUBLANES, LANES), F32)
    hp = hp.at[0, :DN_HEADS].set(-jnp.exp(a_log.astype(F32)))
    hp = hp.at[1, :DN_HEADS].set(dt_bias.astype(F32))
    steps = t // tc
    row = lambda b, s: (b * steps + s, 0)
    const = lambda b, s: (0, 0)
    d = DN_HEAD_DIM
    lane_head = np.arange(2 * d) // d
    sel = np.stack([(np.arange(LANES)[:, None] == 2 * idx + lane_head[None, :]) for idx in range(DN_HEADS)])
    ones_bd = lane_head[:, None] == lane_head[None, :]
    sel = jnp.asarray(sel.astype(np.float32), dtype=BF16)
    ones_bd = jnp.asarray(ones_bd.astype(np.float32), dtype=BF16)
    return pl.pallas_call(
        _dn_kernel,
        grid=(batch, steps),
        in_specs=[
            pl.BlockSpec((DN_CONV, 3 * DN_WIDTH), const),
            pl.BlockSpec((SUBLANES, LANES), const),
            pl.BlockSpec((1, DN_HEAD_DIM), const),
            pl.BlockSpec((DN_HEADS, LANES, 2 * DN_HEAD_DIM), lambda b, s: (0, 0, 0)),
            pl.BlockSpec((2 * DN_HEAD_DIM, 2 * DN_HEAD_DIM), const),
            pl.BlockSpec((tc, DN_WIDTH), row),
            pl.BlockSpec((tc, DN_WIDTH), row),
            pl.BlockSpec((tc, DN_WIDTH), row),
            pl.BlockSpec((tc, DN_WIDTH), row),
            pl.BlockSpec((tc, LANES), row),
        ],
        out_specs=pl.BlockSpec((tc, DN_WIDTH), row),
        out_shape=jax.ShapeDtypeStruct((batch * t, DN_WIDTH), F32),
        scratch_shapes=[
            pltpu.VMEM((DN_HEADS // 2, DN_HEAD_DIM, 2 * DN_HEAD_DIM), F32),
            pltpu.VMEM((3, SUBLANES, DN_WIDTH), F32),
            pltpu.VMEM((tc + SUBLANES, DN_WIDTH), F32),
        ],
        compiler_params=pltpu.CompilerParams(dimension_semantics=("arbitrary", "arbitrary"),
                                             vmem_limit_bytes=VMEM_LIMIT),
        name="gated_deltanet",
    )(conv_w.astype(F32), hp, norm_w.astype(F32).reshape(1, DN_HEAD_DIM), sel, ones_bd, dq, dk, dv, dz, ab)


def _post_kernel(a_ref, dn_ref, x_ref, wo_ref, bo_ref, g_ref, b_ref, rw_ref, rb_ref,
                 x1_ref, slab_ref, ti_ref, gate_ref, rank_ref, cnt_ref, carry_ref):
    @pl.when(pl.program_id(0) == 0)
    def _():
        carry_ref[...] = jnp.zeros_like(carry_ref)

    mix = jnp.concatenate([a_ref[...], dn_ref[...]], axis=1).astype(BF16)
    h = _dot(mix, wo_ref[...]) + bo_ref[...]
    x1 = _layer_norm(DEEPNORM_ALPHA * x_ref[...] + h, g_ref[...], b_ref[...])
    _route_tail(x1, rw_ref, rb_ref, carry_ref, x1_ref, slab_ref, ti_ref, gate_ref, rank_ref, cnt_ref)


def _router_params(router_w, router_b):
    rw = jnp.pad(router_w.astype(F32), ((0, 0), (0, LANES - N_EXPERTS)))
    rw_hi = rw.astype(BF16)
    rw_lo = (rw - rw_hi.astype(F32)).astype(BF16)
    rb = jnp.pad(router_b.astype(F32), (0, LANES - N_EXPERTS)).reshape(1, LANES)
    return jnp.stack([rw_hi, rw_lo]), rb


def _post(a_out, dn_out, x2d, w_out, b_out, ln_g, ln_b, router_w, router_b):
    n = x2d.shape[0]
    tm = ROUTER_TM
    rw, rb = _router_params(router_w, router_b)
    row = lambda i: (i, 0)
    const = lambda i: (0, 0)
    vec = lambda v: v.astype(F32).reshape(1, D_MODEL)
    return pl.pallas_call(
        _post_kernel,
        grid=(n // tm,),
        in_specs=[
            pl.BlockSpec((tm, SWA_Q_WIDTH), row),
            pl.BlockSpec((tm, DN_WIDTH), row),
            pl.BlockSpec((tm, D_MODEL), row),
            pl.BlockSpec((SWA_Q_WIDTH + DN_WIDTH, D_MODEL), const),
            pl.BlockSpec((1, D_MODEL), const),
            pl.BlockSpec((1, D_MODEL), const),
            pl.BlockSpec((1, D_MODEL), const),
            pl.BlockSpec((2, D_MODEL, LANES), lambda i: (0, 0, 0)),
            pl.BlockSpec((1, LANES), const),
        ],
        out_specs=_route_out_specs(tm, lambda i: i),
        out_shape=_route_out_shapes(n),
        scratch_shapes=[pltpu.VMEM((1, LANES), F32)],
        compiler_params=pltpu.CompilerParams(dimension_semantics=("arbitrary",),
                                             vmem_limit_bytes=VMEM_LIMIT),
        name="outproj_ln_router",
    )(a_out, dn_out, x2d, w_out.astype(BF16), vec(b_out), vec(ln_g), vec(ln_b), rw, rb)


def _pool_kernel(x_ref, pw_ref, pb_ref, ps_ref, g_ref, b_ref, rw_ref, rb_ref,
                 x1_ref, slab_ref, ti_ref, gate_ref, rank_ref, cnt_ref,
                 carry_ref, halo_ref, pad_ref):
    tm = x_ref.shape[0]
    halo = POOL_HALO
    b = pl.program_id(0)
    s = pl.program_id(1)

    @pl.when((b == 0) & (s == 0))
    def _():
        carry_ref[...] = jnp.zeros_like(carry_ref)

    @pl.when(s == 0)
    def _():
        halo_ref[...] = jnp.zeros_like(halo_ref)

    x = x_ref[...]
    pad_ref[0:halo, :] = halo_ref[...]
    pad_ref[halo:, :] = x
    halo_ref[...] = x[tm - halo:tm, :]
    tpos = s * tm + lax.broadcasted_iota(jnp.int32, (tm, 1), 0)
    outs = []
    for level, win in enumerate(POOL_WINDOWS, start=1):
        lo = SUBLANES * level
        rows = tm + halo - lo
        cols = slice((level - 1) * POOL_GROUP, D_MODEL)
        shift = win // 2
        pad_ref[pl.ds(lo, rows), cols] = (pad_ref[pl.ds(lo, rows), cols]
                                          + pad_ref[pl.ds(lo - shift, rows), cols])
        gcols = slice((level - 1) * POOL_GROUP, level * POOL_GROUP)
        cnt = jnp.minimum(tpos + 1, win).astype(F32)
        pooled = pad_ref[pl.ds(halo, tm), gcols] / cnt - x[:, gcols]
        y = _dot(pooled.astype(BF16), pw_ref[level - 1]) + pb_ref[level - 1:level, :]
        outs.append(y)
    h = jnp.concatenate(outs, axis=1) * ps_ref[...]
    x1 = _layer_norm(DEEPNORM_ALPHA * x + h, g_ref[...], b_ref[...])
    _route_tail(x1, rw_ref, rb_ref, carry_ref, x1_ref, slab_ref, ti_ref, gate_ref, rank_ref, cnt_ref)


def _pool(x2d, pool_w, pool_b, pool_scale, ln_g, ln_b, router_w, router_b, batch, t):
    n = x2d.shape[0]
    tm = ROUTER_TM
    steps = t // tm
    rw, rb = _router_params(router_w, router_b)
    row = lambda b, s: (b * steps + s, 0)
    const2 = lambda b, s: (0, 0)
    vec = lambda v: v.astype(F32).reshape(1, D_MODEL)
    ng = len(POOL_WINDOWS)
    return pl.pallas_call(
        _pool_kernel,
        grid=(batch, steps),
        in_specs=[
            pl.BlockSpec((tm, D_MODEL), row),
            pl.BlockSpec((ng, POOL_GROUP, POOL_GROUP), lambda b, s: (0, 0, 0)),
            pl.BlockSpec((ng, POOL_GROUP), const2),
            pl.BlockSpec((1, D_MODEL), const2),
            pl.BlockSpec((1, D_MODEL), const2),
            pl.BlockSpec((1, D_MODEL), const2),
            pl.BlockSpec((2, D_MODEL, LANES), lambda b, s: (0, 0, 0)),
            pl.BlockSpec((1, LANES), const2),
        ],
        out_specs=_route_out_specs(tm, lambda b, s: b * steps + s),
        out_shape=_route_out_shapes(n),
        scratch_shapes=[
            pltpu.VMEM((1, LANES), F32),
            pltpu.VMEM((POOL_HALO, D_MODEL), F32),
            pltpu.VMEM((tm + POOL_HALO, D_MODEL), F32),
        ],
        compiler_params=pltpu.CompilerParams(dimension_semantics=("arbitrary", "arbitrary"),
                                             vmem_limit_bytes=VMEM_LIMIT),
        name="pool_ln_router",
    )(x2d, pool_w.astype(BF16), pool_b.astype(F32), vec(pool_scale), vec(ln_g), vec(ln_b), rw, rb)


def _swap_vreg_sublane(vs, order=(4, 2, 1)):
    sub = lax.broadcasted_iota(jnp.int32, vs[0].shape, 1)
    vs = list(vs)
    for d in order:
        keep = (sub & d) == 0
        nxt = list(vs)
        for j in range(SUBLANES):
            if j & d:
                continue
            a, b = vs[j], vs[j + d]
            nxt[j] = jnp.where(keep, a, pltpu.roll(b, d, 1))
            nxt[j + d] = jnp.where(keep, pltpu.roll(a, SUBLANES - d, 1), b)
        vs = nxt
    return vs


def _dispatch_kernel(starts_ref, cnt_ref, pos_ref, xv_ref, xs_hbm, zero_ref, sem, zsem):
    td = xv_ref.shape[0]

    @pl.when(pl.program_id(0) == 0)
    def _():
        zero_ref[...] = jnp.zeros_like(zero_ref)

        def per_expert(e, carry):
            c0 = cnt_ref[e]
            c1 = jnp.bitwise_and(c0 + (MOE_TM - 1), -MOE_TM)
            base = starts_ref[e]

            def start(r, c):
                pltpu.make_async_copy(zero_ref.at[0], xs_hbm.at[base + r], zsem).start()
                return c

            def wait(r, c):
                pltpu.make_async_copy(zero_ref.at[0], xs_hbm.at[base + r], zsem).wait()
                return c

            lax.fori_loop(c0, c1, start, 0)
            lax.fori_loop(c0, c1, wait, 0)
            return carry

        lax.fori_loop(0, N_EXPERTS, per_expert, 0)

        last = N_EXPERTS - 1
        used_rows = starts_ref[last] + jnp.bitwise_and(cnt_ref[last] + (MOE_TM - 1), -MOE_TM)
        first_free = used_rows // MOE_TM
        n_tiles = xs_hbm.shape[0] // MOE_TM

        def tail_copy(tile):
            dst = xs_hbm.at[pl.ds(pl.multiple_of(tile * MOE_TM, MOE_TM), MOE_TM)]
            return pltpu.make_async_copy(zero_ref, dst, zsem)

        def tail_start(tile, c):
            tail_copy(tile).start()
            return c

        def tail_wait(tile, c):
            tail_copy(tile).wait()
            return c

        lax.fori_loop(first_free, n_tiles, tail_start, 0)
        lax.fori_loop(first_free, n_tiles, tail_wait, 0)

    def body(t, carry):
        for k in range(TOP_K):
            pltpu.make_async_copy(xv_ref.at[t], xs_hbm.at[pos_ref[k, t]], sem).start(priority=k % 2)
        return carry

    lax.fori_loop(0, td, body, 0)
    pltpu.make_async_copy(xs_hbm.at[pl.ds(0, td * TOP_K)], xs_hbm.at[pl.ds(0, td * TOP_K)], sem).wait()


def _dispatch(starts, counts, pos_flat, x_slab3, rows):
    n = x_slab3.shape[0]
    td = min(1024, n)
    grid_spec = pltpu.PrefetchScalarGridSpec(
        num_scalar_prefetch=2,
        grid=(n // td,),
        in_specs=[
            pl.BlockSpec((TOP_K, td), lambda i, st, ct: (0, i), memory_space=pltpu.SMEM),
            pl.BlockSpec((td, SLAB, LANES), lambda i, st, ct: (i, 0, 0)),
        ],
        out_specs=pl.BlockSpec(memory_space=pl.ANY),
        scratch_shapes=[
            pltpu.VMEM((MOE_TM, SLAB, LANES), F32),
            pltpu.SemaphoreType.DMA(()),
            pltpu.SemaphoreType.DMA(()),
        ],
    )
    return pl.pallas_call(
        _dispatch_kernel,
        grid_spec=grid_spec,
        out_shape=jax.ShapeDtypeStruct((rows, SLAB, LANES), F32),
        compiler_params=pltpu.CompilerParams(dimension_semantics=("arbitrary",),
                                             has_side_effects=True),
        name="moe_dispatch",
    )(starts, counts, pos_flat, x_slab3)


def _moe_kernel(layer, te_ref, first_ref, slot_ref, nxt_ref, nu_ref,
                x_ref, b1a_ref, b2a_ref, b1b_ref, b2b_ref, w1_hbm, w2_hbm, o_ref,
                w1s_ref, w2s_ref, w1p_ref, w2b_ref, sem):
    tm = MOE_TM
    g = tm // SUBLANES
    pw = 2 * LANES

    def weight_copies(expert, slot):
        return (pltpu.make_async_copy(w1_hbm.at[layer, expert], w1s_ref.at[slot], sem.at[0, slot]),
                pltpu.make_async_copy(w2_hbm.at[layer, expert], w2s_ref.at[slot], sem.at[1, slot]))

    @pl.when(pl.program_id(0) == 0)
    def _():
        for c in weight_copies(te_ref[0], 0):
            c.start()

    def tile(sub, b1_ref, b2_ref):
        i = pl.program_id(0) * MOE_TILES_PER_STEP + sub
        row0 = sub * tm
        used = i < nu_ref[0]
        e = te_ref[i]

        @pl.when(used & (first_ref[i] == 1))
        def _():
            slot = slot_ref[i]
            nxt = nxt_ref[i]

            @pl.when(nxt >= 0)
            def _():
                for c in weight_copies(nxt, 1 - slot):
                    c.start()

            for c in weight_copies(e, slot):
                c.wait()
            r = lax.broadcasted_iota(jnp.int32, (pw, pw), 0)
            cidx = lax.broadcasted_iota(jnp.int32, (pw, pw), 1)
            src = jnp.where(cidx < LANES, 2 * cidx, 2 * (cidx - LANES) + 1)
            perm = (r == src).astype(BF16)
            for blk in range(2 * D_EXPERT // pw):
                wb = w1s_ref[slot, :, blk * pw:(blk + 1) * pw].astype(BF16)
                w1p_ref[:, blk * pw:(blk + 1) * pw] = _dot(wb, perm).astype(BF16)
            w2b_ref[...] = w2s_ref[slot].astype(BF16)

        @pl.when(used)
        def _():
            slabs = [x_ref[pl.ds(row0 + j, g, stride=SUBLANES)] for j in range(SUBLANES)]
            chunks = _swap_vreg_sublane(slabs)
            x = jnp.concatenate([c.reshape(tm, LANES) for c in chunks], axis=1).astype(BF16)
            acts = []
            for blk in range(2 * D_EXPERT // pw):
                cols = slice(blk * pw, (blk + 1) * pw)
                h = _dot(x, w1p_ref[:, cols]) + b1_ref[0, 0, :, cols]
                hg = jnp.minimum(h[:, :LANES], SWIGLU_LIMIT)
                hl = jnp.clip(h[:, LANES:], -SWIGLU_LIMIT, SWIGLU_LIMIT)
                acts.append((hg * _sigmoid(SWIGLU_ALPHA * hg) * (hl + 1.0)).astype(BF16))
            act = jnp.concatenate(acts, axis=1)
            ys = []
            for nb in range(D_MODEL // pw):
                cols = slice(nb * pw, (nb + 1) * pw)
                y = _dot(act, w2b_ref[:, cols]) + b2_ref[0, 0, :, cols]
                ys += [y[:, s * LANES:(s + 1) * LANES].reshape(g, SUBLANES, LANES) for s in range(pw // LANES)]
            out_slabs = _swap_vreg_sublane(ys, order=(1, 2, 4))
            for j in range(SUBLANES):
                o_ref[pl.ds(row0 + j, g, stride=SUBLANES)] = out_slabs[j]

        @pl.when(jnp.logical_not(used))
        def _():
            o_ref[pl.ds(row0, tm)] = jnp.zeros((tm, SLAB, LANES), F32)

    tile(0, b1a_ref, b2a_ref)
    tile(1, b1b_ref, b2b_ref)


def _moe_mlp(layer, meta, xs3, w1, b1p, w2, b2, n_tiles):
    tps = MOE_TILES_PER_STEP
    tm = MOE_TM * tps
    tile_expert, first, slot, nxt, n_used = meta

    def x_map(i, te, fi, sl, nx, nu):
        return (jnp.minimum(i, (nu[0] - 1) // tps), 0, 0)

    def bias_map(sub):
        return lambda i, te, fi, sl, nx, nu: (layer, te[i * tps + sub], 0, 0)

    grid_spec = pltpu.PrefetchScalarGridSpec(
        num_scalar_prefetch=5,
        grid=(n_tiles // tps,),
        in_specs=[
            pl.BlockSpec((tm, SLAB, LANES), x_map),
            pl.BlockSpec((1, 1, 1, 2 * D_EXPERT), bias_map(0)),
            pl.BlockSpec((1, 1, 1, D_MODEL), bias_map(0)),
            pl.BlockSpec((1, 1, 1, 2 * D_EXPERT), bias_map(1)),
            pl.BlockSpec((1, 1, 1, D_MODEL), bias_map(1)),
            pl.BlockSpec(memory_space=pl.ANY),
            pl.BlockSpec(memory_space=pl.ANY),
        ],
        out_specs=pl.BlockSpec((tm, SLAB, LANES), lambda i, te, fi, sl, nx, nu: (i, 0, 0)),
        scratch_shapes=[
            pltpu.VMEM((2, D_MODEL, 2 * D_EXPERT), F32),
            pltpu.VMEM((2, D_EXPERT, D_MODEL), F32),
            pltpu.VMEM((D_MODEL, 2 * D_EXPERT), BF16),
            pltpu.VMEM((D_EXPERT, D_MODEL), BF16),
            pltpu.SemaphoreType.DMA((2, 2)),
        ],
    )
    return pl.pallas_call(
        functools.partial(_moe_kernel, layer),
        grid_spec=grid_spec,
        out_shape=jax.ShapeDtypeStruct((n_tiles * MOE_TM, SLAB, LANES), F32),
        compiler_params=pltpu.CompilerParams(dimension_semantics=("arbitrary",),
                                             vmem_limit_bytes=VMEM_LIMIT),
        name="moe_grouped_mlp",
    )(tile_expert, first, slot, nxt, n_used, xs3, b1p, b2, b1p, b2, w1, w2)


def _combine_kernel(pos_ref, pos_next_ref, gate_ref, os_hbm, x1_ref, g_ref, b_ref, o_ref,
                    buf_ref, mix_ref, sem):
    tc = x1_ref.shape[0]
    g = tc // SUBLANES
    i = pl.program_id(0)
    slot = i % 2

    def issue(p_ref, sl):
        def body(t, carry):
            for k in range(TOP_K):
                a = t * TOP_K + k
                pltpu.make_async_copy(os_hbm.at[p_ref[k, t]], buf_ref.at[sl, a], sem.at[sl]).start(priority=k % 2)
            return carry

        lax.fori_loop(0, tc, body, 0)

    @pl.when(i == 0)
    def _():
        issue(pos_ref, 0)

    @pl.when(i + 1 < pl.num_programs(0))
    def _():
        issue(pos_next_ref, 1 - slot)

    pltpu.make_async_copy(buf_ref.at[slot], buf_ref.at[slot], sem.at[slot]).wait()

    def mix_group(grp, carry):
        for j in range(SUBLANES):
            t = grp * SUBLANES + j
            a0 = t * TOP_K
            acc = gate_ref[0, t] * buf_ref[slot, a0]
            for k in range(1, TOP_K):
                acc = acc + gate_ref[k, t] * buf_ref[slot, a0 + k]
            mix_ref[t] = acc
        return carry

    lax.fori_loop(0, g, mix_group, 0)
    slabs = [mix_ref[pl.ds(j, g, stride=SUBLANES)] for j in range(SUBLANES)]
    chunks = _swap_vreg_sublane(slabs)
    f = jnp.concatenate([c.reshape(tc, LANES) for c in chunks], axis=1)
    o_ref[...] = _layer_norm(DEEPNORM_ALPHA * x1_ref[...] + f, g_ref[...], b_ref[...])


def _combine(pos_flat, out_sorted3, gate, x1, ln_g, ln_b):
    n = x1.shape[0]
    tc = 512
    steps = n // tc
    vec = lambda v: v.astype(F32).reshape(1, D_MODEL)
    return pl.pallas_call(
        _combine_kernel,
        grid=(steps,),
        in_specs=[
            pl.BlockSpec((TOP_K, tc), lambda i: (0, i), memory_space=pltpu.SMEM),
            pl.BlockSpec((TOP_K, tc), lambda i: (0, jnp.minimum(i + 1, steps - 1)), memory_space=pltpu.SMEM),
            pl.BlockSpec((TOP_K, tc), lambda i: (0, i), memory_space=pltpu.SMEM),
            pl.BlockSpec(memory_space=pl.ANY),
            pl.BlockSpec((tc, D_MODEL), lambda i: (i, 0)),
            pl.BlockSpec((1, D_MODEL), lambda i: (0, 0)),
            pl.BlockSpec((1, D_MODEL), lambda i: (0, 0)),
        ],
        out_specs=pl.BlockSpec((tc, D_MODEL), lambda i: (i, 0)),
        out_shape=jax.ShapeDtypeStruct((n, D_MODEL), F32),
        scratch_shapes=[
            pltpu.VMEM((2, tc * TOP_K, SLAB, LANES), F32),
            pltpu.VMEM((tc, SLAB, LANES), F32),
            pltpu.SemaphoreType.DMA((2,)),
        ],
        compiler_params=pltpu.CompilerParams(dimension_semantics=("arbitrary",),
                                             vmem_limit_bytes=VMEM_LIMIT),
        name="moe_combine_ln",
    )(pos_flat, pos_flat, gate, out_sorted3, x1, vec(ln_g), vec(ln_b))


def _group_metadata(counts, n_tiles):
    tm = MOE_TM
    experts = jnp.arange(N_EXPERTS, dtype=jnp.int32)
    cnt = counts[0, :N_EXPERTS]
    tiles_per = (cnt + tm - 1) // tm
    tile_end = jnp.cumsum(tiles_per)
    starts = ((tile_end - tiles_per) * tm).astype(jnp.int32)
    n_used = tile_end[-1:].astype(jnp.int32)
    active = tiles_per > 0
    te_last = jnp.max(jnp.where(active, experts, 0))
    tidx = jnp.arange(n_tiles, dtype=jnp.int32)
    te = jnp.sum((tidx[:, None] >= tile_end[None, :]).astype(jnp.int32), axis=1)
    is_used = tidx < n_used[0]
    tile_expert = jnp.where(is_used, jnp.minimum(te, N_EXPERTS - 1), te_last).astype(jnp.int32)
    prev = jnp.concatenate([tile_expert[:1] - 1, tile_expert[:-1]])
    first = (is_used & (tile_expert != prev)).astype(jnp.int32)
    slot_e = ((jnp.cumsum(active.astype(jnp.int32)) - 1) % 2).astype(jnp.int32)
    later = jnp.where(active, experts, N_EXPERTS)
    suffix_min = lax.cummin(later[::-1])[::-1]
    nxt_e = jnp.concatenate([suffix_min[1:], jnp.full((1,), N_EXPERTS, jnp.int32)])
    nxt_e = jnp.where(nxt_e >= N_EXPERTS, -1, nxt_e).astype(jnp.int32)
    onehot = tile_expert[:, None] == experts[None, :]
    slot = jnp.sum(jnp.where(onehot, slot_e[None, :], 0), axis=1).astype(jnp.int32)
    nxt = jnp.sum(jnp.where(onehot, nxt_e[None, :], 0), axis=1).astype(jnp.int32)
    return starts, cnt.astype(jnp.int32), (tile_expert, first, slot, nxt, n_used)


def _moe_layer(layer, routed, w1, b1p, w2, b2, ln_g, ln_b):
    x1, x1_slab, ti, gate, rank, counts = routed
    n = x1.shape[0]
    n_tiles = n * TOP_K // MOE_TM + N_EXPERTS
    rows = n_tiles * MOE_TM
    starts, cnt, meta = _group_metadata(counts, n_tiles)
    assert n_tiles % MOE_TILES_PER_STEP == 0
    experts = jnp.arange(N_EXPERTS, dtype=jnp.int32)
    ti4, rank4 = ti[:TOP_K], rank[:TOP_K]
    start_of = jnp.sum(jnp.where(ti4[:, :, None] == experts[None, None, :], starts[None, None, :], 0), axis=-1)
    pos = (start_of + rank4).astype(jnp.int32)
    xs = _dispatch(starts, cnt, pos, x1_slab.reshape(n, SLAB, LANES), rows)
    out_sorted = _moe_mlp(layer, meta, xs, w1, b1p, w2, b2, n_tiles)
    return _combine(pos, out_sorted, gate[:TOP_K], x1, ln_g, ln_b)


def kernel(x, positions, mix_w_in, mix_b_in, dn_conv_w, dn_a_log, dn_dt_bias, dn_norm_w, swa_sinks,
           mix_w_out, mix_b_out, pool_w, pool_b, pool_scale, ln1_g, ln1_b, router_w, router_b,
           moe_w1, moe_b1, moe_w2, moe_b2, ln2_g, ln2_b):
    batch, t, d = x.shape
    n = batch * t
    x2d = x.reshape(n, d)
    pos2d = positions.reshape(n, 1).astype(jnp.int32)
    pw = 2 * LANES
    b1p = moe_b1.astype(F32).reshape(DEPTH, N_EXPERTS, 2 * D_EXPERT // pw, LANES, 2)
    b1p = jnp.swapaxes(b1p, 3, 4).reshape(DEPTH, N_EXPERTS, 1, 2 * D_EXPERT)
    b2r = moe_b2.astype(F32).reshape(DEPTH, N_EXPERTS, 1, D_MODEL)
    for layer in range(DEPTH):
        i = layer // 2
        if layer % 2 == 0:
            q, kv, dq, dk, dv, dz, ab = _inproj(x2d, pos2d, mix_w_in[i], mix_b_in[i])
            a_out = _swa(q, kv, swa_sinks[i], batch, t)
            dn_out = _deltanet(dq, dk, dv, dz, ab, dn_conv_w[i], dn_a_log[i], dn_dt_bias[i],
                               dn_norm_w[i], batch, t)
            routed = _post(a_out, dn_out, x2d, mix_w_out[i], mix_b_out[i], ln1_g[layer], ln1_b[layer],
                           router_w[layer], router_b[layer])
        else:
            routed = _pool(x2d, pool_w[i], pool_b[i], pool_scale[i], ln1_g[layer], ln1_b[layer],
                           router_w[layer], router_b[layer], batch, t)
        x2d = _moe_layer(layer, routed, moe_w1, b1p, moe_w2, b2r, ln2_g[layer], ln2_b[layer])
    return x2d.reshape(batch, t, d)
```

```python
import functools
import math

import numpy as np
import jax
import jax.numpy as jnp
from jax import lax
from jax.experimental import pallas as pl
from jax.experimental.pallas import tpu as pltpu

D_MODEL = 1024
DEPTH = 2
SWA_Q_HEADS = 8
SWA_KV_HEADS = 2
SWA_HEAD_DIM = 64
SWA_BLOCK = 128
ROPE_THETA = 500000.0
ROPE_DIM = SWA_HEAD_DIM // 4
DN_HEADS = 4
DN_HEAD_DIM = 128
DN_CONV = 4
POOL_WINDOWS = (2, 4, 8, 16)
POOL_GROUP = D_MODEL // 4
N_EXPERTS = 32
TOP_K = 4
D_EXPERT = D_MODEL
SWIGLU_LIMIT = 7.0
SWIGLU_ALPHA = 1.702
LN_EPS = 1e-5
RMS_EPS = 1e-6
DEEPNORM_ALPHA = (2 * DEPTH) ** 0.25
SWA_Q_WIDTH = SWA_Q_HEADS * SWA_HEAD_DIM
SWA_KV_WIDTH = SWA_KV_HEADS * SWA_HEAD_DIM
DN_WIDTH = DN_HEADS * DN_HEAD_DIM

LANES = 128
SUBLANES = 8
SLAB = D_MODEL // LANES
IN_PAD_WIDTH = 3328
VMEM_LIMIT = 56 * 1024 * 1024

F32 = jnp.float32
BF16 = jnp.bfloat16
HIGHEST = lax.Precision.HIGHEST

ROUTER_TM = 512
MOE_TM = 256
MOE_TILES_PER_STEP = 2
DN_CHUNK = 128
POOL_HALO = SUBLANES * len(POOL_WINDOWS)


def _sigmoid(x):
    return 0.5 + 0.5 * jnp.tanh(0.5 * x)


def _layer_norm(z, g, b):
    mu = jnp.mean(z, axis=-1, keepdims=True)
    zc = z - mu
    var = jnp.mean(zc * zc, axis=-1, keepdims=True)
    return zc * lax.rsqrt(var + LN_EPS) * g + b


def _dot(a, b):
    return jnp.dot(a, b, preferred_element_type=F32)


def _dot_nt(a, b):
    return lax.dot_general(a, b, (((1,), (1,)), ((), ())), preferred_element_type=F32)


def _route_tail(x1, rw_ref, rb_ref, carry_ref, x1_ref, slab_ref, ti_ref, gate_ref, rank_ref, cnt_ref):
    tm = x1.shape[0]
    x1_ref[...] = x1
    for s in range(SLAB):
        slab_ref[pl.ds(s, tm, stride=SLAB), :] = x1[:, s * LANES:(s + 1) * LANES]

    xh = x1.astype(BF16)
    xl = (x1 - xh.astype(F32)).astype(BF16)
    logits = _dot(xh, rw_ref[0]) + (_dot(xh, rw_ref[1]) + _dot(xl, rw_ref[0])) + rb_ref[...]
    lane = lax.broadcasted_iota(jnp.int32, (tm, LANES), 1)
    lane_f = lane.astype(F32)
    l = jnp.where(lane < N_EXPERTS, logits, -jnp.inf)
    vals, hits = [], []
    ti_out = jnp.zeros((tm, LANES), F32)
    for k in range(TOP_K):
        m = jnp.max(l, axis=-1, keepdims=True)
        idx = jnp.min(jnp.where(l == m, lane_f, float(LANES)), axis=-1, keepdims=True)
        hit = lane_f == idx
        l = jnp.where(hit, -jnp.inf, l)
        vals.append(m)
        hits.append(hit)
        ti_out = jnp.where(lane == k, idx, ti_out)
    exps = [jnp.exp(v - vals[0]) for v in vals]
    den = exps[0] + exps[1] + exps[2] + exps[3]
    gate_out = jnp.zeros((tm, LANES), F32)
    for k in range(TOP_K):
        gate_out = jnp.where(lane == k, exps[k] / den, gate_out)

    sel = jnp.zeros((tm, LANES), F32)
    for k in range(TOP_K):
        sel = sel + hits[k].astype(F32)
    ri = lax.broadcasted_iota(jnp.int32, (tm, tm), 0)
    ci = lax.broadcasted_iota(jnp.int32, (tm, tm), 1)
    tri = (ri > ci).astype(BF16)
    carry = carry_ref[...]
    prefix = _dot(tri, sel.astype(BF16)) + carry
    rank_out = jnp.zeros((tm, LANES), F32)
    for k in range(TOP_K):
        r = jnp.sum(jnp.where(hits[k], prefix, 0.0), axis=-1, keepdims=True)
        rank_out = jnp.where(lane == k, r, rank_out)
    new_carry = carry + jnp.sum(sel, axis=0, keepdims=True)
    carry_ref[...] = new_carry
    cnt_ref[...] = new_carry.astype(jnp.int32)
    ti_ref[...] = ti_out.T[:SUBLANES, :].astype(jnp.int32)
    gate_ref[...] = gate_out.T[:SUBLANES, :]
    rank_ref[...] = rank_out.T[:SUBLANES, :].astype(jnp.int32)


def _route_out_shapes(n):
    return (
        jax.ShapeDtypeStruct((n, D_MODEL), F32),
        jax.ShapeDtypeStruct((n * SLAB, LANES), F32),
        jax.ShapeDtypeStruct((SUBLANES, n), jnp.int32),
        jax.ShapeDtypeStruct((SUBLANES, n), F32),
        jax.ShapeDtypeStruct((SUBLANES, n), jnp.int32),
        jax.ShapeDtypeStruct((1, LANES), jnp.int32),
    )


def _route_out_specs(tm, row_map):
    return (
        pl.BlockSpec((tm, D_MODEL), lambda *a: (row_map(*a), 0)),
        pl.BlockSpec((tm * SLAB, LANES), lambda *a: (row_map(*a), 0)),
        pl.BlockSpec((SUBLANES, tm), lambda *a: (0, row_map(*a))),
        pl.BlockSpec((SUBLANES, tm), lambda *a: (0, row_map(*a))),
        pl.BlockSpec((SUBLANES, tm), lambda *a: (0, row_map(*a))),
        pl.BlockSpec((1, LANES), lambda *a: (0, 0)),
    )


def _inproj_kernel(x_ref, pos_ref, invf_ref, w_ref, b_ref,
                   q_ref, kv_ref, dq_ref, dk_ref, dv_ref, dz_ref, ab_ref):
    x = x_ref[...].astype(BF16)
    proj = _dot(x, w_ref[...]) + b_ref[...]
    tm = x.shape[0]
    ang = pos_ref[...].astype(F32) * invf_ref[...]
    cos = jnp.cos(ang)
    sin = jnp.sin(ang)
    d = lax.broadcasted_iota(jnp.int32, (tm, LANES), 1) % SWA_HEAD_DIM
    half = ROPE_DIM // 2
    c_tab = jnp.where(d < ROPE_DIM, cos, 1.0)
    s_lo = jnp.where(d < half, -sin, 0.0)
    s_hi = jnp.where((d >= half) & (d < ROPE_DIM), sin, 0.0)

    def rot(xc):
        return (xc * c_tab + pltpu.roll(xc, LANES - half, 1) * s_lo
                + pltpu.roll(xc, half, 1) * s_hi)

    for c in range(SWA_Q_WIDTH // LANES):
        q_ref[:, c * LANES:(c + 1) * LANES] = rot(proj[:, c * LANES:(c + 1) * LANES])
    o = SWA_Q_WIDTH
    kv_ref[:, :LANES] = rot(proj[:, o:o + LANES])
    kv_ref[:, LANES:] = proj[:, o + LANES:o + 2 * LANES]
    o += 2 * SWA_KV_WIDTH
    dq_ref[...] = proj[:, o:o + DN_WIDTH]
    dk_ref[...] = proj[:, o + DN_WIDTH:o + 2 * DN_WIDTH]
    dv_ref[...] = proj[:, o + 2 * DN_WIDTH:o + 3 * DN_WIDTH]
    o += 3 * DN_WIDTH
    dz_ref[...] = proj[:, o:o + DN_WIDTH]
    o += DN_WIDTH
    ab_ref[...] = proj[:, o:o + LANES]


def _inproj(x2d, pos2d, w_in, b_in):
    n = x2d.shape[0]
    tm = 512
    in_width = w_in.shape[1]
    w = jnp.pad(w_in, ((0, 0), (0, IN_PAD_WIDTH - in_width))).astype(BF16)
    b = jnp.pad(b_in, (0, IN_PAD_WIDTH - in_width)).reshape(1, IN_PAD_WIDTH)
    half = ROPE_DIM // 2
    lane_d = np.arange(LANES) % SWA_HEAD_DIM
    invf = (ROPE_THETA ** (-(lane_d % half).astype(np.float64) / half)).astype(np.float32)
    invf = jnp.asarray(invf.reshape(1, LANES))
    row = lambda i: (i, 0)
    const = lambda i: (0, 0)
    widths = (SWA_Q_WIDTH, 2 * SWA_KV_WIDTH, DN_WIDTH, DN_WIDTH, DN_WIDTH, DN_WIDTH, LANES)
    return pl.pallas_call(
        _inproj_kernel,
        grid=(n // tm,),
        in_specs=[
            pl.BlockSpec((tm, D_MODEL), row),
            pl.BlockSpec((tm, 1), row),
            pl.BlockSpec((1, LANES), const),
            pl.BlockSpec((D_MODEL, IN_PAD_WIDTH), const),
            pl.BlockSpec((1, IN_PAD_WIDTH), const),
        ],
        out_specs=tuple(pl.BlockSpec((tm, wd), row) for wd in widths),
        out_shape=tuple(jax.ShapeDtypeStruct((n, wd), F32) for wd in widths),
        compiler_params=pltpu.CompilerParams(dimension_semantics=("arbitrary",),
                                             vmem_limit_bytes=VMEM_LIMIT),
        name="inproj_rotary",
    )(x2d, pos2d, invf, w, b)


def _swa_kernel(sink_ref, q_ref, kv_ref, o_ref):
    t = q_ref.shape[0]
    blk = SWA_BLOCK
    lane = lax.broadcasted_iota(jnp.int32, (2 * blk, LANES), 1)
    qi = lax.broadcasted_iota(jnp.int32, (blk, 2 * blk), 0)
    kj = lax.broadcasted_iota(jnp.int32, (blk, 2 * blk), 1)
    rel = qi + blk - kj
    in_window = (rel >= 0) & (rel < blk)
    is_cur = kj >= blk
    scale = SWA_HEAD_DIM ** -0.5

    group = SWA_Q_HEADS // SWA_KV_HEADS
    n_chunks = SWA_Q_WIDTH // LANES
    lane_q = lax.broadcasted_iota(jnp.int32, (blk, LANES), 1)
    blocks_per_iter = 2

    def body(it, carry):
        items = []
        for sub in range(blocks_per_iter):
            n = it * blocks_per_iter + sub
            r0 = pl.multiple_of(n * blk, blk)
            p0 = pl.multiple_of(jnp.maximum(n - 1, 0) * blk, blk)
            kvc = kv_ref[pl.ds(r0, blk), :]
            kvp = kv_ref[pl.ds(p0, blk), :]
            kband = jnp.concatenate([kvp[:, :LANES], kvc[:, :LANES]], axis=0)
            vband = jnp.concatenate([kvp[:, LANES:], kvc[:, LANES:]], axis=0)
            valid = in_window & (is_cur | (n > 0))
            k_rhs, v_rhs = [], []
            for h in range(SWA_KV_HEADS):
                in_head = (lane >= h * SWA_HEAD_DIM) & (lane < (h + 1) * SWA_HEAD_DIM)
                km = jnp.where(in_head, kband, 0.0)
                vm = jnp.where(in_head, vband, 0.0)
                kr = pltpu.roll(km, SWA_HEAD_DIM, 1)
                vr = pltpu.roll(vm, SWA_HEAD_DIM, 1)
                lo_k, hi_k = (km, kr) if h == 0 else (kr, km)
                lo_v, hi_v = (vm, vr) if h == 0 else (vr, vm)
                k_rhs.append(jnp.concatenate([lo_k, hi_k], axis=0).astype(BF16))
                v_rhs.append(jnp.concatenate([lo_v, hi_v], axis=0).astype(BF16))
            for c in range(n_chunks):
                items.append((r0, c, valid, k_rhs[(2 * c) // group], v_rhs[(2 * c) // group]))

        scores = [_dot_nt((q_ref[pl.ds(r0, blk), c * LANES:(c + 1) * LANES] * scale).astype(BF16), kr_)
                  for r0, c, _, kr_, _ in items]
        masked = [[jnp.where(valid, s[:, j * 2 * blk:(j + 1) * 2 * blk], -jnp.inf) for j in range(2)]
                  for s, (_, _, valid, _, _) in zip(scores, items)]
        maxes = [[jnp.maximum(jnp.max(s[j], axis=-1, keepdims=True), sink_ref[2 * c + j]) for j in range(2)]
                 for s, (_, c, _, _, _) in zip(masked, items)]
        probs = [[jnp.exp(s[j] - m[j]) for j in range(2)] for s, m in zip(masked, maxes)]
        dens = [[jnp.sum(p[j], axis=-1, keepdims=True) + jnp.exp(sink_ref[2 * c + j] - m[j]) for j in range(2)]
                for p, m, (_, c, _, _, _) in zip(probs, maxes, items)]
        for p, dn, (r0, c, _, _, vr_) in zip(probs, dens, items):
            o = _dot(jnp.concatenate([p[0].astype(BF16), p[1].astype(BF16)], axis=1), vr_)
            inv = jnp.where(lane_q < SWA_HEAD_DIM, 1.0 / dn[0], 1.0 / dn[1])
            o_ref[pl.ds(r0, blk), c * LANES:(c + 1) * LANES] = o * inv
        return carry

    lax.fori_loop(0, t // (blk * blocks_per_iter), body, 0)


def _swa(q, kv, sinks, batch, t):
    return pl.pallas_call(
        _swa_kernel,
        grid=(batch,),
        in_specs=[
            pl.BlockSpec(memory_space=pltpu.SMEM),
            pl.BlockSpec((t, SWA_Q_WIDTH), lambda b: (b, 0)),
            pl.BlockSpec((t, 2 * SWA_KV_WIDTH), lambda b: (b, 0)),
        ],
        out_specs=pl.BlockSpec((t, SWA_Q_WIDTH), lambda b: (b, 0)),
        out_shape=jax.ShapeDtypeStruct((batch * t, SWA_Q_WIDTH), F32),
        compiler_params=pltpu.CompilerParams(dimension_semantics=("arbitrary",),
                                             vmem_limit_bytes=VMEM_LIMIT),
        name="swa_attention",
    )(sinks.astype(F32), q, kv)


def _dn_kernel(convw_ref, hp_ref, normw_ref, sel_ref, ones_ref, dq_ref, dk_ref, dv_ref, dz_ref, ab_ref, o_ref,
               state_ref, halo_ref, pad_ref):
    tc = dq_ref.shape[0]
    c = DN_CHUNK
    halo = SUBLANES

    @pl.when(pl.program_id(1) == 0)
    def _():
        state_ref[...] = jnp.zeros_like(state_ref)
        halo_ref[...] = jnp.zeros_like(halo_ref)

    def conv_silu(x_ref, idx):
        pad_ref[0:halo, :] = halo_ref[idx]
        pad_ref[halo:, :] = x_ref[...]
        halo_ref[idx] = x_ref[tc - halo:tc, :]
        acc = jnp.zeros((tc, DN_WIDTH), F32)
        for j in range(DN_CONV):
            wj = convw_ref[j:j + 1, idx * DN_WIDTH:(idx + 1) * DN_WIDTH]
            acc = acc + wj * pad_ref[pl.ds(halo - DN_CONV + 1 + j, tc), :]
        return acc * _sigmoid(acc)

    q_all = conv_silu(dq_ref, 0)
    k_all = conv_silu(dk_ref, 1)
    v_all = conv_silu(dv_ref, 2)

    ab = ab_ref[...]
    neg_a = hp_ref[0:1, :]
    dt_b = hp_ref[1:2, :]
    sp_arg = ab + dt_b
    softplus = jnp.maximum(sp_arg, 0.0) + jnp.log(1.0 + jnp.exp(-jnp.abs(sp_arg)))
    g_all = neg_a * softplus
    beta_all = _sigmoid(ab)

    d = DN_HEAD_DIM
    ri = lax.broadcasted_iota(jnp.int32, (c, 2 * c), 0)
    ci = lax.broadcasted_iota(jnp.int32, (c, 2 * c), 1) % c
    lower_incl = ri >= ci
    lower_strict = ri > ci
    eye2 = (ri == ci).astype(F32)
    r1 = lax.broadcasted_iota(jnp.int32, (c, c), 0)
    c1 = lax.broadcasted_iota(jnp.int32, (c, c), 1)
    ltri = (r1 >= c1).astype(F32)
    ones_bd = ones_ref[...]

    def lane_bcast(x, first_lane):
        sel = sel_ref[first_lane // 2]
        hi = x.astype(BF16)
        lo = (x - hi.astype(F32)).astype(BF16)
        return _dot(hi, sel) + _dot(lo, sel)

    def blockdiag(x2):
        n = x2.shape[1] // 2
        z = jnp.zeros((x2.shape[0], n), BF16)
        return jnp.concatenate([jnp.concatenate([x2[:, :n], z], axis=1),
                                jnp.concatenate([z, x2[:, n:]], axis=1)], axis=0)

    n_chunks = tc // c
    items = [(ch, pr) for ch in range(n_chunks) for pr in range(DN_HEADS // 2)]

    gcs, gcts = [], []
    for ch in range(n_chunks):
        gc = jnp.dot(ltri, g_all[ch * c:(ch + 1) * c, :], precision=HIGHEST, preferred_element_type=F32)
        gcs.append(gc)
        gcts.append(gc.T)
    pre = []
    for ch, pr in items:
        rows = slice(ch * c, (ch + 1) * c)
        cols2 = slice(pr * 2 * d, (pr + 1) * 2 * d)
        gct = gcts[ch]
        g2 = lane_bcast(gcs[ch], 2 * pr)
        b2 = lane_bcast(beta_all[rows, :], DN_HEADS + 2 * pr)
        grow2 = jnp.concatenate([gct[2 * pr:2 * pr + 1, :], gct[2 * pr + 1:2 * pr + 2, :]], axis=1)
        gl2 = g2[c - 1:c, :]
        q2 = q_all[rows, cols2]
        k2 = k_all[rows, cols2]
        ss = _dot(jnp.concatenate([q2 * q2, k2 * k2], axis=0).astype(BF16), ones_bd)
        qn2 = q2 * lax.rsqrt(ss[:c] + RMS_EPS) * (d ** -0.5)
        kn2 = k2 * lax.rsqrt(ss[c:] + RMS_EPS)
        eg2 = jnp.exp(g2)
        kb2 = kn2 * b2
        decay2 = jnp.exp(jnp.where(lower_incl, g2 - grow2, -jnp.inf))
        lhs = jnp.concatenate([kb2, qn2], axis=0).astype(BF16)
        kn_t2 = jnp.concatenate([kn2[:, :d].T, kn2[:, d:].T], axis=1).astype(BF16)
        kk_qk = _dot(lhs, blockdiag(kn_t2))
        vb2 = (v_all[rows, cols2] * b2).astype(BF16)
        kbeg2 = (kb2 * eg2).astype(BF16)
        zero2 = jnp.zeros((c, 2 * d), BF16)
        rhs_bd = jnp.concatenate(
            [jnp.concatenate([vb2[:, :d], kbeg2[:, :d], zero2], axis=1),
             jnp.concatenate([zero2, vb2[:, d:], kbeg2[:, d:]], axis=1)], axis=0)
        kdec2 = kn2 * jnp.exp(gl2 - g2)
        pre.append(dict(
            a2=jnp.where(lower_strict, kk_qk[:c] * decay2, 0.0),
            qk2=(kk_qk[c:] * decay2).astype(BF16),
            rhs_bd=rhs_bd,
            qdec2=qn2 * eg2,
            kdec_t2=jnp.concatenate([kdec2[:, :d].T, kdec2[:, d:].T], axis=1).astype(BF16),
            glast2=jnp.exp(gl2),
        ))

    tinv = [eye2 - p["a2"] for p in pre]
    apow = [p["a2"] for p in pre]
    for _ in range(int(math.log2(c)) - 1):
        a16 = [a.astype(BF16) for a in apow]
        apow = [_dot(a, blockdiag(a)) for a in a16]
        tinv = [t + _dot(t.astype(BF16), blockdiag(a.astype(BF16))) for t, a in zip(tinv, apow)]

    for p, t in zip(pre, tinv):
        sol16 = _dot(t.astype(BF16), p["rhs_bd"]).astype(BF16)
        sol_bd = jnp.concatenate(
            [jnp.concatenate([sol16[:, :2 * d], jnp.zeros((c, 2 * d), BF16)], axis=1),
             jnp.concatenate([jnp.zeros((c, 2 * d), BF16), sol16[:, 2 * d:]], axis=1)], axis=0)
        ks = _dot(p["kdec_t2"], sol_bd)
        qs = _dot(p["qk2"], sol_bd)
        p["n2"] = jnp.concatenate([ks[:, :d], ks[:, 2 * d:3 * d]], axis=1)
        p["k2"] = jnp.concatenate([ks[:, d:2 * d], ks[:, 3 * d:]], axis=1).astype(BF16)
        p["o2"] = jnp.concatenate([qs[:, :d], qs[:, 2 * d:3 * d]], axis=1)
        p["q2"] = (p["qdec2"] - jnp.concatenate([qs[:, d:2 * d], qs[:, 3 * d:]], axis=1)).astype(BF16)

    normw2 = jnp.concatenate([normw_ref[...], normw_ref[...]], axis=1)
    for (ch, pr), p in zip(items, pre):
        rows = slice(ch * c, (ch + 1) * c)
        s2 = state_ref[pr]
        s_bd = blockdiag(s2.astype(BF16))
        o2 = _dot(p["q2"], s_bd) + p["o2"]
        state_ref[pr] = s2 * p["glast2"] - _dot(p["k2"], s_bd) + p["n2"]
        cols2 = slice(pr * 2 * d, (pr + 1) * 2 * d)
        ms = _dot((o2 * o2).astype(BF16), ones_bd) * (1.0 / d)
        z2 = dz_ref[rows, cols2]
        o_ref[rows, cols2] = o2 * lax.rsqrt(ms + RMS_EPS) * normw2 * (z2 * _sigmoid(z2))


def _deltanet(dq, dk, dv, dz, ab, conv_w, a_log, dt_bias, norm_w, batch, t):
    tc = 512
    hp = jnp.zeros((SUBLANES, LANES), F32)
    hp = hp.at[0, :DN_HEADS].set(-jnp.exp(a_log.astype(F32)))
    hp = hp.at[1, :DN_HEADS].set(dt_bias.astype(F32))
    steps = t // tc
    row = lambda b, s: (b * steps + s, 0)
    const = lambda b, s: (0, 0)
    d = DN_HEAD_DIM
    lane_head = np.arange(2 * d) // d
    sel = np.stack([(np.arange(LANES)[:, None] == 2 * idx + lane_head[None, :]) for idx in range(DN_HEADS)])
    ones_bd = lane_head[:, None] == lane_head[None, :]
    sel = jnp.asarray(sel.astype(np.float32), dtype=BF16)
    ones_bd = jnp.asarray(ones_bd.astype(np.float32), dtype=BF16)
    return pl.pallas_call(
        _dn_kernel,
        grid=(batch, steps),
        in_specs=[
            pl.BlockSpec((DN_CONV, 3 * DN_WIDTH), const),
            pl.BlockSpec((SUBLANES, LANES), const),
            pl.BlockSpec((1, DN_HEAD_DIM), const),
            pl.BlockSpec((DN_HEADS, LANES, 2 * DN_HEAD_DIM), lambda b, s: (0, 0, 0)),
            pl.BlockSpec((2 * DN_HEAD_DIM, 2 * DN_HEAD_DIM), const),
            pl.BlockSpec((tc, DN_WIDTH), row),
            pl.BlockSpec((tc, DN_WIDTH), row),
            pl.BlockSpec((tc, DN_WIDTH), row),
            pl.BlockSpec((tc, DN_WIDTH), row),
            pl.BlockSpec((tc, LANES), row),
        ],
        out_specs=pl.BlockSpec((tc, DN_WIDTH), row),
        out_shape=jax.ShapeDtypeStruct((batch * t, DN_WIDTH), F32),
        scratch_shapes=[
            pltpu.VMEM((DN_HEADS // 2, DN_HEAD_DIM, 2 * DN_HEAD_DIM), F32),
            pltpu.VMEM((3, SUBLANES, DN_WIDTH), F32),
            pltpu.VMEM((tc + SUBLANES, DN_WIDTH), F32),
        ],
        compiler_params=pltpu.CompilerParams(dimension_semantics=("arbitrary", "arbitrary"),
                                             vmem_limit_bytes=VMEM_LIMIT),
        name="gated_deltanet",
    )(conv_w.astype(F32), hp, norm_w.astype(F32).reshape(1, DN_HEAD_DIM), sel, ones_bd, dq, dk, dv, dz, ab)


def _post_kernel(a_ref, dn_ref, x_ref, wo_ref, bo_ref, g_ref, b_ref, rw_ref, rb_ref,
                 x1_ref, slab_ref, ti_ref, gate_ref, rank_ref, cnt_ref, carry_ref):
    @pl.when(pl.program_id(0) == 0)
    def _():
        carry_ref[...] = jnp.zeros_like(carry_ref)

    mix = jnp.concatenate([a_ref[...], dn_ref[...]], axis=1).astype(BF16)
    h = _dot(mix, wo_ref[...]) + bo_ref[...]
    x1 = _layer_norm(DEEPNORM_ALPHA * x_ref[...] + h, g_ref[...], b_ref[...])
    _route_tail(x1, rw_ref, rb_ref, carry_ref, x1_ref, slab_ref, ti_ref, gate_ref, rank_ref, cnt_ref)


def _router_params(router_w, router_b):
    rw = jnp.pad(router_w.astype(F32), ((0, 0), (0, LANES - N_EXPERTS)))
    rw_hi = rw.astype(BF16)
    rw_lo = (rw - rw_hi.astype(F32)).astype(BF16)
    rb = jnp.pad(router_b.astype(F32), (0, LANES - N_EXPERTS)).reshape(1, LANES)
    return jnp.stack([rw_hi, rw_lo]), rb


def _post(a_out, dn_out, x2d, w_out, b_out, ln_g, ln_b, router_w, router_b):
    n = x2d.shape[0]
    tm = ROUTER_TM
    rw, rb = _router_params(router_w, router_b)
    row = lambda i: (i, 0)
    const = lambda i: (0, 0)
    vec = lambda v: v.astype(F32).reshape(1, D_MODEL)
    return pl.pallas_call(
        _post_kernel,
        grid=(n // tm,),
        in_specs=[
            pl.BlockSpec((tm, SWA_Q_WIDTH), row),
            pl.BlockSpec((tm, DN_WIDTH), row),
            pl.BlockSpec((tm, D_MODEL), row),
            pl.BlockSpec((SWA_Q_WIDTH + DN_WIDTH, D_MODEL), const),
            pl.BlockSpec((1, D_MODEL), const),
            pl.BlockSpec((1, D_MODEL), const),
            pl.BlockSpec((1, D_MODEL), const),
            pl.BlockSpec((2, D_MODEL, LANES), lambda i: (0, 0, 0)),
            pl.BlockSpec((1, LANES), const),
        ],
        out_specs=_route_out_specs(tm, lambda i: i),
        out_shape=_route_out_shapes(n),
        scratch_shapes=[pltpu.VMEM((1, LANES), F32)],
        compiler_params=pltpu.CompilerParams(dimension_semantics=("arbitrary",),
                                             vmem_limit_bytes=VMEM_LIMIT),
        name="outproj_ln_router",
    )(a_out, dn_out, x2d, w_out.astype(BF16), vec(b_out), vec(ln_g), vec(ln_b), rw, rb)


def _pool_kernel(x_ref, pw_ref, pb_ref, ps_ref, g_ref, b_ref, rw_ref, rb_ref,
                 x1_ref, slab_ref, ti_ref, gate_ref, rank_ref, cnt_ref,
                 carry_ref, halo_ref, pad_ref):
    tm = x_ref.shape[0]
    halo = POOL_HALO
    b = pl.program_id(0)
    s = pl.program_id(1)

    @pl.when((b == 0) & (s == 0))
    def _():
        carry_ref[...] = jnp.zeros_like(carry_ref)

    @pl.when(s == 0)
    def _():
        halo_ref[...] = jnp.zeros_like(halo_ref)

    x = x_ref[...]
    pad_ref[0:halo, :] = halo_ref[...]
    pad_ref[halo:, :] = x
    halo_ref[...] = x[tm - halo:tm, :]
    tpos = s * tm + lax.broadcasted_iota(jnp.int32, (tm, 1), 0)
    outs = []
    for level, win in enumerate(POOL_WINDOWS, start=1):
        lo = SUBLANES * level
        rows = tm + halo - lo
        cols = slice((level - 1) * POOL_GROUP, D_MODEL)
        shift = win // 2
        pad_ref[pl.ds(lo, rows), cols] = (pad_ref[pl.ds(lo, rows), cols]
                                          + pad_ref[pl.ds(lo - shift, rows), cols])
        gcols = slice((level - 1) * POOL_GROUP, level * POOL_GROUP)
        cnt = jnp.minimum(tpos + 1, win).astype(F32)
        pooled = pad_ref[pl.ds(halo, tm), gcols] / cnt - x[:, gcols]
        y = _dot(pooled.astype(BF16), pw_ref[level - 1]) + pb_ref[level - 1:level, :]
        outs.append(y)
    h = jnp.concatenate(outs, axis=1) * ps_ref[...]
    x1 = _layer_norm(DEEPNORM_ALPHA * x + h, g_ref[...], b_ref[...])
    _route_tail(x1, rw_ref, rb_ref, carry_ref, x1_ref, slab_ref, ti_ref, gate_ref, rank_ref, cnt_ref)


def _pool(x2d, pool_w, pool_b, pool_scale, ln_g, ln_b, router_w, router_b, batch, t):
    n = x2d.shape[0]
    tm = ROUTER_TM
    steps = t // tm
    rw, rb = _router_params(router_w, router_b)
    row = lambda b, s: (b * steps + s, 0)
    const2 = lambda b, s: (0, 0)
    vec = lambda v: v.astype(F32).reshape(1, D_MODEL)
    ng = len(POOL_WINDOWS)
    return pl.pallas_call(
        _pool_kernel,
        grid=(batch, steps),
        in_specs=[
            pl.BlockSpec((tm, D_MODEL), row),
            pl.BlockSpec((ng, POOL_GROUP, POOL_GROUP), lambda b, s: (0, 0, 0)),
            pl.BlockSpec((ng, POOL_GROUP), const2),
            pl.BlockSpec((1, D_MODEL), const2),
            pl.BlockSpec((1, D_MODEL), const2),
            pl.BlockSpec((1, D_MODEL), const2),
            pl.BlockSpec((2, D_MODEL, LANES), lambda b, s: (0, 0, 0)),
            pl.BlockSpec((1, LANES), const2),
        ],
        out_specs=_route_out_specs(tm, lambda b, s: b * steps + s),
        out_shape=_route_out_shapes(n),
        scratch_shapes=[
            pltpu.VMEM((1, LANES), F32),
            pltpu.VMEM((POOL_HALO, D_MODEL), F32),
            pltpu.VMEM((tm + POOL_HALO, D_MODEL), F32),
        ],
        compiler_params=pltpu.CompilerParams(dimension_semantics=("arbitrary", "arbitrary"),
                                             vmem_limit_bytes=VMEM_LIMIT),
        name="pool_ln_router",
    )(x2d, pool_w.astype(BF16), pool_b.astype(F32), vec(pool_scale), vec(ln_g), vec(ln_b), rw, rb)


def _swap_vreg_sublane(vs, order=(4, 2, 1)):
    sub = lax.broadcasted_iota(jnp.int32, vs[0].shape, 1)
    vs = list(vs)
    for d in order:
        keep = (sub & d) == 0
        nxt = list(vs)
        for j in range(SUBLANES):
            if j & d:
                continue
            a, b = vs[j], vs[j + d]
            nxt[j] = jnp.where(keep, a, pltpu.roll(b, d, 1))
            nxt[j + d] = jnp.where(keep, pltpu.roll(a, SUBLANES - d, 1), b)
        vs = nxt
    return vs


def _dispatch_kernel(starts_ref, cnt_ref, pos_ref, xv_ref, xs_hbm, zero_ref, sem, zsem):
    td = xv_ref.shape[0]

    @pl.when(pl.program_id(0) == 0)
    def _():
        zero_ref[...] = jnp.zeros_like(zero_ref)

        def per_expert(e, carry):
            c0 = cnt_ref[e]
            c1 = jnp.bitwise_and(c0 + (MOE_TM - 1), -MOE_TM)
            base = starts_ref[e]

            def start(r, c):
                pltpu.make_async_copy(zero_ref.at[0], xs_hbm.at[base + r], zsem).start()
                return c

            def wait(r, c):
                pltpu.make_async_copy(zero_ref.at[0], xs_hbm.at[base + r], zsem).wait()
                return c

            lax.fori_loop(c0, c1, start, 0)
            lax.fori_loop(c0, c1, wait, 0)
            return carry

        lax.fori_loop(0, N_EXPERTS, per_expert, 0)

        last = N_EXPERTS - 1
        used_rows = starts_ref[last] + jnp.bitwise_and(cnt_ref[last] + (MOE_TM - 1), -MOE_TM)
        first_free = used_rows // MOE_TM
        n_tiles = xs_hbm.shape[0] // MOE_TM

        def tail_copy(tile):
            dst = xs_hbm.at[pl.ds(pl.multiple_of(tile * MOE_TM, MOE_TM), MOE_TM)]
            return pltpu.make_async_copy(zero_ref, dst, zsem)

        def tail_start(tile, c):
            tail_copy(tile).start()
            return c

        def tail_wait(tile, c):
            tail_copy(tile).wait()
            return c

        lax.fori_loop(first_free, n_tiles, tail_start, 0)
        lax.fori_loop(first_free, n_tiles, tail_wait, 0)

    def body(t, carry):
        for k in range(TOP_K):
            pltpu.make_async_copy(xv_ref.at[t], xs_hbm.at[pos_ref[k, t]], sem).start(priority=k % 2)
        return carry

    lax.fori_loop(0, td, body, 0)
    pltpu.make_async_copy(xs_hbm.at[pl.ds(0, td * TOP_K)], xs_hbm.at[pl.ds(0, td * TOP_K)], sem).wait()


def _dispatch(starts, counts, pos_flat, x_slab3, rows):
    n = x_slab3.shape[0]
    td = min(1024, n)
    grid_spec = pltpu.PrefetchScalarGridSpec(
        num_scalar_prefetch=2,
        grid=(n // td,),
        in_specs=[
            pl.BlockSpec((TOP_K, td), lambda i, st, ct: (0, i), memory_space=pltpu.SMEM),
            pl.BlockSpec((td, SLAB, LANES), lambda i, st, ct: (i, 0, 0)),
        ],
        out_specs=pl.BlockSpec(memory_space=pl.ANY),
        scratch_shapes=[
            pltpu.VMEM((MOE_TM, SLAB, LANES), F32),
            pltpu.SemaphoreType.DMA(()),
            pltpu.SemaphoreType.DMA(()),
        ],
    )
    return pl.pallas_call(
        _dispatch_kernel,
        grid_spec=grid_spec,
        out_shape=jax.ShapeDtypeStruct((rows, SLAB, LANES), F32),
        compiler_params=pltpu.CompilerParams(dimension_semantics=("arbitrary",),
                                             has_side_effects=True),
        name="moe_dispatch",
    )(starts, counts, pos_flat, x_slab3)


def _moe_kernel(layer, te_ref, first_ref, slot_ref, nxt_ref, nu_ref,
                x_ref, b1a_ref, b2a_ref, b1b_ref, b2b_ref, w1_hbm, w2_hbm, o_ref,
                w1s_ref, w2s_ref, w1p_ref, w2b_ref, sem):
    tm = MOE_TM
    g = tm // SUBLANES
    pw = 2 * LANES

    def weight_copies(expert, slot):
        return (pltpu.make_async_copy(w1_hbm.at[layer, expert], w1s_ref.at[slot], sem.at[0, slot]),
                pltpu.make_async_copy(w2_hbm.at[layer, expert], w2s_ref.at[slot], sem.at[1, slot]))

    @pl.when(pl.program_id(0) == 0)
    def _():
        for c in weight_copies(te_ref[0], 0):
            c.start()

    def tile(sub, b1_ref, b2_ref):
        i = pl.program_id(0) * MOE_TILES_PER_STEP + sub
        row0 = sub * tm
        used = i < nu_ref[0]
        e = te_ref[i]

        @pl.when(used & (first_ref[i] == 1))
        def _():
            slot = slot_ref[i]
            nxt = nxt_ref[i]

            @pl.when(nxt >= 0)
            def _():
                for c in weight_copies(nxt, 1 - slot):
                    c.start()

            for c in weight_copies(e, slot):
                c.wait()
            r = lax.broadcasted_iota(jnp.int32, (pw, pw), 0)
            cidx = lax.broadcasted_iota(jnp.int32, (pw, pw), 1)
            src = jnp.where(cidx < LANES, 2 * cidx, 2 * (cidx - LANES) + 1)
            perm = (r == src).astype(BF16)
            for blk in range(2 * D_EXPERT // pw):
                wb = w1s_ref[slot, :, blk * pw:(blk + 1) * pw].astype(BF16)
                w1p_ref[:, blk * pw:(blk + 1) * pw] = _dot(wb, perm).astype(BF16)
            w2b_ref[...] = w2s_ref[slot].astype(BF16)

        @pl.when(used)
        def _():
            slabs = [x_ref[pl.ds(row0 + j, g, stride=SUBLANES)] for j in range(SUBLANES)]
            chunks = _swap_vreg_sublane(slabs)
            x = jnp.concatenate([c.reshape(tm, LANES) for c in chunks], axis=1).astype(BF16)
            acts = []
            for blk in range(2 * D_EXPERT // pw):
                cols = slice(blk * pw, (blk + 1) * pw)
                h = _dot(x, w1p_ref[:, cols]) + b1_ref[0, 0, :, cols]
                hg = jnp.minimum(h[:, :LANES], SWIGLU_LIMIT)
                hl = jnp.clip(h[:, LANES:], -SWIGLU_LIMIT, SWIGLU_LIMIT)
                acts.append((hg * _sigmoid(SWIGLU_ALPHA * hg) * (hl + 1.0)).astype(BF16))
            act = jnp.concatenate(acts, axis=1)
            ys = []
            for nb in range(D_MODEL // pw):
                cols = slice(nb * pw, (nb + 1) * pw)
                y = _dot(act, w2b_ref[:, cols]) + b2_ref[0, 0, :, cols]
                ys += [y[:, s * LANES:(s + 1) * LANES].reshape(g, SUBLANES, LANES) for s in range(pw // LANES)]
            out_slabs = _swap_vreg_sublane(ys, order=(1, 2, 4))
            for j in range(SUBLANES):
                o_ref[pl.ds(row0 + j, g, stride=SUBLANES)] = out_slabs[j]

        @pl.when(jnp.logical_not(used))
        def _():
            o_ref[pl.ds(row0, tm)] = jnp.zeros((tm, SLAB, LANES), F32)

    tile(0, b1a_ref, b2a_ref)
    tile(1, b1b_ref, b2b_ref)


def _moe_mlp(layer, meta, xs3, w1, b1p, w2, b2, n_tiles):
    tps = MOE_TILES_PER_STEP
    tm = MOE_TM * tps
    tile_expert, first, slot, nxt, n_used = meta

    def x_map(i, te, fi, sl, nx, nu):
        return (jnp.minimum(i, (nu[0] - 1) // tps), 0, 0)

    def bias_map(sub):
        return lambda i, te, fi, sl, nx, nu: (layer, te[i * tps + sub], 0, 0)

    grid_spec = pltpu.PrefetchScalarGridSpec(
        num_scalar_prefetch=5,
        grid=(n_tiles // tps,),
        in_specs=[
            pl.BlockSpec((tm, SLAB, LANES), x_map),
            pl.BlockSpec((1, 1, 1, 2 * D_EXPERT), bias_map(0)),
            pl.BlockSpec((1, 1, 1, D_MODEL), bias_map(0)),
            pl.BlockSpec((1, 1, 1, 2 * D_EXPERT), bias_map(1)),
            pl.BlockSpec((1, 1, 1, D_MODEL), bias_map(1)),
            pl.BlockSpec(memory_space=pl.ANY),
            pl.BlockSpec(memory_space=pl.ANY),
        ],
        out_specs=pl.BlockSpec((tm, SLAB, LANES), lambda i, te, fi, sl, nx, nu: (i, 0, 0)),
        scratch_shapes=[
            pltpu.VMEM((2, D_MODEL, 2 * D_EXPERT), F32),
            pltpu.VMEM((2, D_EXPERT, D_MODEL), F32),
            pltpu.VMEM((D_MODEL, 2 * D_EXPERT), BF16),
            pltpu.VMEM((D_EXPERT, D_MODEL), BF16),
            pltpu.SemaphoreType.DMA((2, 2)),
        ],
    )
    return pl.pallas_call(
        functools.partial(_moe_kernel, layer),
        grid_spec=grid_spec,
        out_shape=jax.ShapeDtypeStruct((n_tiles * MOE_TM, SLAB, LANES), F32),
        compiler_params=pltpu.CompilerParams(dimension_semantics=("arbitrary",),
                                             vmem_limit_bytes=VMEM_LIMIT),
        name="moe_grouped_mlp",
    )(tile_expert, first, slot, nxt, n_used, xs3, b1p, b2, b1p, b2, w1, w2)


def _combine_kernel(pos_ref, pos_next_ref, gate0_ref, gate1_ref, gate2_ref, gate3_ref, os_hbm, x1_ref, g_ref, b_ref,
                    o_ref, buf_ref, mix_ref, sem):
    gates = (gate0_ref, gate1_ref, gate2_ref, gate3_ref)
    tc = x1_ref.shape[0]
    g = tc // SUBLANES
    i = pl.program_id(0)
    slot = i % 2

    def issue(p_ref, sl):
        def body(t, carry):
            for k in range(TOP_K):
                a = t * TOP_K + k
                pltpu.make_async_copy(os_hbm.at[p_ref[k, t]], buf_ref.at[sl, a], sem.at[sl]).start(priority=k % 2)
            return carry

        lax.fori_loop(0, tc, body, 0)

    @pl.when(i == 0)
    def _():
        issue(pos_ref, 0)

    @pl.when(i + 1 < pl.num_programs(0))
    def _():
        issue(pos_next_ref, 1 - slot)

    pltpu.make_async_copy(buf_ref.at[slot], buf_ref.at[slot], sem.at[slot]).wait()

    def mix_group(grp, carry):
        for j in range(SUBLANES):
            t = grp * SUBLANES + j
            a0 = t * TOP_K
            acc = gates[0][t] * buf_ref[slot, a0]
            for k in range(1, TOP_K):
                acc = acc + gates[k][t] * buf_ref[slot, a0 + k]
            mix_ref[t] = acc
        return carry

    lax.fori_loop(0, g, mix_group, 0)
    slabs = [mix_ref[pl.ds(j, g, stride=SUBLANES)] for j in range(SUBLANES)]
    chunks = _swap_vreg_sublane(slabs)
    f = jnp.concatenate([c.reshape(tc, LANES) for c in chunks], axis=1)
    o_ref[...] = _layer_norm(DEEPNORM_ALPHA * x1_ref[...] + f, g_ref[...], b_ref[...])


def _combine(pos_flat, out_sorted3, gate, x1, ln_g, ln_b):
    n = x1.shape[0]
    tc = 512
    steps = n // tc
    vec = lambda v: v.astype(F32).reshape(1, D_MODEL)
    return pl.pallas_call(
        _combine_kernel,
        grid=(steps,),
        in_specs=[
            pl.BlockSpec((TOP_K, tc), lambda i: (0, i), memory_space=pltpu.SMEM),
            pl.BlockSpec((TOP_K, tc), lambda i: (0, jnp.minimum(i + 1, steps - 1)), memory_space=pltpu.SMEM),
            pl.BlockSpec((tc,), lambda i: (i,), memory_space=pltpu.SMEM),
            pl.BlockSpec((tc,), lambda i: (i,), memory_space=pltpu.SMEM),
            pl.BlockSpec((tc,), lambda i: (i,), memory_space=pltpu.SMEM),
            pl.BlockSpec((tc,), lambda i: (i,), memory_space=pltpu.SMEM),
            pl.BlockSpec(memory_space=pl.ANY),
            pl.BlockSpec((tc, D_MODEL), lambda i: (i, 0)),
            pl.BlockSpec((1, D_MODEL), lambda i: (0, 0)),
            pl.BlockSpec((1, D_MODEL), lambda i: (0, 0)),
        ],
        out_specs=pl.BlockSpec((tc, D_MODEL), lambda i: (i, 0)),
        out_shape=jax.ShapeDtypeStruct((n, D_MODEL), F32),
        scratch_shapes=[
            pltpu.VMEM((2, tc * TOP_K, SLAB, LANES), F32),
            pltpu.VMEM((tc, SLAB, LANES), F32),
            pltpu.SemaphoreType.DMA((2,)),
        ],
        compiler_params=pltpu.CompilerParams(dimension_semantics=("arbitrary",),
                                             vmem_limit_bytes=VMEM_LIMIT),
        name="moe_combine_ln",
    )(pos_flat, pos_flat, gate[0], gate[1], gate[2], gate[3], out_sorted3, x1, vec(ln_g), vec(ln_b))


def _group_metadata(counts, n_tiles):
    tm = MOE_TM
    experts = jnp.arange(N_EXPERTS, dtype=jnp.int32)
    cnt = counts[0, :N_EXPERTS]
    tiles_per = (cnt + tm - 1) // tm
    tile_end = jnp.cumsum(tiles_per)
    starts = ((tile_end - tiles_per) * tm).astype(jnp.int32)
    n_used = tile_end[-1:].astype(jnp.int32)
    active = tiles_per > 0
    te_last = jnp.max(jnp.where(active, experts, 0))
    tidx = jnp.arange(n_tiles, dtype=jnp.int32)
    te = jnp.sum((tidx[:, None] >= tile_end[None, :]).astype(jnp.int32), axis=1)
    is_used = tidx < n_used[0]
    tile_expert = jnp.where(is_used, jnp.minimum(te, N_EXPERTS - 1), te_last).astype(jnp.int32)
    prev = jnp.concatenate([tile_expert[:1] - 1, tile_expert[:-1]])
    first = (is_used & (tile_expert != prev)).astype(jnp.int32)
    slot_e = ((jnp.cumsum(active.astype(jnp.int32)) - 1) % 2).astype(jnp.int32)
    later = jnp.where(active, experts, N_EXPERTS)
    suffix_min = lax.cummin(later[::-1])[::-1]
    nxt_e = jnp.concatenate([suffix_min[1:], jnp.full((1,), N_EXPERTS, jnp.int32)])
    nxt_e = jnp.where(nxt_e >= N_EXPERTS, -1, nxt_e).astype(jnp.int32)
    onehot = tile_expert[:, None] == experts[None, :]
    slot = jnp.sum(jnp.where(onehot, slot_e[None, :], 0), axis=1).astype(jnp.int32)
    nxt = jnp.sum(jnp.where(onehot, nxt_e[None, :], 0), axis=1).astype(jnp.int32)
    return starts, cnt.astype(jnp.int32), (tile_expert, first, slot, nxt, n_used)


def _moe_layer(layer, routed, w1, b1p, w2, b2, ln_g, ln_b):
    x1, x1_slab, ti, gate, rank, counts = routed
    n = x1.shape[0]
    n_tiles = n * TOP_K // MOE_TM + N_EXPERTS
    rows = n_tiles * MOE_TM
    starts, cnt, meta = _group_metadata(counts, n_tiles)
    assert n_tiles % MOE_TILES_PER_STEP == 0
    experts = jnp.arange(N_EXPERTS, dtype=jnp.int32)
    ti4, rank4 = ti[:TOP_K], rank[:TOP_K]
    start_of = jnp.sum(jnp.where(ti4[:, :, None] == experts[None, None, :], starts[None, None, :], 0), axis=-1)
    pos = (start_of + rank4).astype(jnp.int32)
    xs = _dispatch(starts, cnt, pos, x1_slab.reshape(n, SLAB, LANES), rows)
    out_sorted = _moe_mlp(layer, meta, xs, w1, b1p, w2, b2, n_tiles)
    return _combine(pos, out_sorted, gate[:TOP_K], x1, ln_g, ln_b)


def kernel(x, positions, mix_w_in, mix_b_in, dn_conv_w, dn_a_log, dn_dt_bias, dn_norm_w, swa_sinks,
           mix_w_out, mix_b_out, pool_w, pool_b, pool_scale, ln1_g, ln1_b, router_w, router_b,
           moe_w1, moe_b1, moe_w2, moe_b2, ln2_g, ln2_b):
    batch, t, d = x.shape
    n = batch * t
    x2d = x.reshape(n, d)
    pos2d = positions.reshape(n, 1).astype(jnp.int32)
    pw = 2 * LANES
    b1p = moe_b1.astype(F32).reshape(DEPTH, N_EXPERTS, 2 * D_EXPERT // pw, LANES, 2)
    b1p = jnp.swapaxes(b1p, 3, 4).reshape(DEPTH, N_EXPERTS, 1, 2 * D_EXPERT)
    b2r = moe_b2.astype(F32).reshape(DEPTH, N_EXPERTS, 1, D_MODEL)
    for layer in range(DEPTH):
        i = layer // 2
        if layer % 2 == 0:
            q, kv, dq, dk, dv, dz, ab = _inproj(x2d, pos2d, mix_w_in[i], mix_b_in[i])
            a_out = _swa(q, kv, swa_sinks[i], batch, t)
            dn_out = _deltanet(dq, dk, dv, dz, ab, dn_conv_w[i], dn_a_log[i], dn_dt_bias[i],
                               dn_norm_w[i], batch, t)
            routed = _post(a_out, dn_out, x2d, mix_w_out[i], mix_b_out[i], ln1_g[layer], ln1_b[layer],
                           router_w[layer], router_b[layer])
        else:
            routed = _pool(x2d, pool_w[i], pool_b[i], pool_scale[i], ln1_g[layer], ln1_b[layer],
                           router_w[layer], router_b[layer], batch, t)
        x2d = _moe_layer(layer, routed, moe_w1, b1p, moe_w2, b2r, ln2_g[layer], ln2_b[layer])
    return x2d.reshape(batch, t, d)
```
